```python
import jax, jax.numpy as jnp
from jax import lax
import numpy as np

D_MODEL = 1024
BATCH = 8
SEQ = 4096
DEPTH = 2

CHUNK = 64
HEAD_DIM = 64
D_MIX = D_MODEL
A_WIDTH = D_MIX // 4
R_WIDTH = 3 * D_MIX // 8
F_WIDTH = 3 * D_MIX // 8
A_GROUPS = A_WIDTH // HEAD_DIM
R_HEADS = R_WIDTH // HEAD_DIM
F_HEADS = F_WIDTH // HEAD_DIM
A_BLOCK = 128
Q_BLOCK = 128
ROPE_THETA = 10000.0
EPS = 1e-6

A_COLS = 3 * A_WIDTH
R_COLS = 4 * R_WIDTH
F_COLS = 4 * F_WIDTH + F_HEADS
D_IN = A_COLS + R_COLS + F_COLS

kernel_name = "hybrid_gmlp_retention_fox_block"


def _rms_norm(x, gain):
    xf = x.astype(jnp.float32)
    y = xf * lax.rsqrt(jnp.mean(xf * xf, axis=-1, keepdims=True) + EPS)
    return (y * gain.astype(jnp.float32)).astype(x.dtype)


def _layer_norm(x, gain=None):
    xf = x.astype(jnp.float32)
    mu = jnp.mean(xf, axis=-1, keepdims=True)
    var = jnp.mean(jnp.square(xf - mu), axis=-1, keepdims=True)
    y = (xf - mu) * lax.rsqrt(var + EPS)
    if gain is not None:
        y = y * gain.astype(jnp.float32)
    return y.astype(x.dtype)


def _rope(x):
    s, d = x.shape[1], x.shape[3]
    half = d // 2
    inv = ROPE_THETA ** (-jnp.arange(half, dtype=jnp.float32) / half)
    ang = jnp.arange(s, dtype=jnp.float32)[:, None] * inv[None, :]
    cos = jnp.cos(ang)[:, None, :].astype(x.dtype)
    sin = jnp.sin(ang)[:, None, :].astype(x.dtype)
    x1, x2 = x[..., :half], x[..., half:]
    return jnp.concatenate([x1 * cos - x2 * sin, x1 * sin + x2 * cos], axis=-1)


def _spatial_gating(u, v, ln_gain, w_s, b_s):
    b, s, _ = v.shape
    v = _layer_norm(v.reshape(b, s, A_GROUPS, HEAD_DIM), ln_gain)
    v = v.reshape(b, s // A_BLOCK, A_BLOCK, A_GROUPS, HEAD_DIM)
    pos = jnp.arange(A_BLOCK)
    allowed = (pos[None, :] // CHUNK) <= (pos[:, None] // CHUNK)
    w = jnp.where(allowed[None], w_s, jnp.zeros_like(w_s))
    mixed = jnp.einsum('gij,bnjgc->bnigc', w, v) + b_s.T[:, :, None]
    return u * mixed.reshape(b, s, A_WIDTH)


def _retention(q, k, v):
    b, s, h, d = q.shape
    nc = s // CHUNK
    dt = v.dtype
    q = _rope(q) * (d ** -0.5)
    k = _rope(k)
    log_gamma = jnp.log(1.0 - jnp.exp2(-5.0 - jnp.arange(h, dtype=jnp.float32)))
    pos = jnp.arange(CHUNK, dtype=jnp.float32)
    dist = jnp.abs(pos[:, None] - pos[None, :])
    intra_decay = jnp.exp(log_gamma[:, None, None] * dist).astype(dt)
    k_decay = jnp.exp(log_gamma[None, :] * (CHUNK - 1 - pos)[:, None]).astype(dt)
    q_decay = jnp.exp(log_gamma[None, :] * (pos + 1.0)[:, None]).astype(dt)
    chunk_decay = jnp.exp(log_gamma * CHUNK).astype(dt)
    qc = q.reshape(b, nc, CHUNK, h, d)
    kc = k.reshape(b, nc, CHUNK, h, d)
    vc = v.reshape(b, nc, CHUNK, h, d)
    scores = jnp.einsum('bnihd,bnjhd->bnhij', qc, kc) * intra_decay
    intra = jnp.einsum('bnhij,bnjhe->bnihe', scores, vc)
    kv = jnp.einsum('bnjhd,bnjhe->nbhde', kc * k_decay[:, :, None], vc)

    def step(state, kv_c):
        return state * chunk_decay[:, None, None] + kv_c, state

    _, s_prev = lax.scan(step, jnp.zeros((b, h, d, d), dt), kv)
    cross = jnp.einsum('bnihd,nbhde->bnihe', qc * q_decay[:, :, None], s_prev)
    out = (intra + cross).reshape(b, s, h, d)
    return _layer_norm(out)


def _forgetting_attention(q, k, v, f_logit):
    b, s, h, d = q.shape
    scale = d ** -0.5
    cum_f = jnp.cumsum(jax.nn.log_sigmoid(f_logit.astype(jnp.float32)), axis=1)
    cum_f = jnp.transpose(cum_f, (0, 2, 1))
    outs = []
    for i in range(s // Q_BLOCK):
        q0, q1 = i * Q_BLOCK, (i + 1) * Q_BLOCK
        qb = q[:, q0:q1]
        kb = k[:, :q1]
        vb = v[:, :q1]
        logits = jnp.einsum('bqhd,bkhd->bhqk', qb, kb).astype(jnp.float32) * scale
        logits = logits + cum_f[:, :, q0:q1, None] - cum_f[:, :, None, :q1]
        qpos = q0 + jnp.arange(Q_BLOCK)
        kpos = jnp.arange(q1)
        causal = kpos[None, :] <= qpos[:, None]
        logits = jnp.where(causal, logits, -jnp.inf)
        p = jax.nn.softmax(logits, axis=-1).astype(v.dtype)
        outs.append(jnp.einsum('bhqk,bkhd->bqhd', p, vb))
    return jnp.concatenate(outs, axis=1)


def _hybrid_layer(x, pre_g, post_g, w_in, b_f, a_ln_g, a_ws, a_bs, w_out):
    b, s, _ = x.shape
    h = _rms_norm(x, pre_g)
    z = jnp.einsum('bsd,de->bse', h, w_in)
    a_z, r_z, f_z = jnp.split(z, [A_COLS, A_COLS + R_COLS], axis=-1)

    a_u, a_v, a_g = jnp.split(a_z, 3, axis=-1)
    a_out = _spatial_gating(jax.nn.gelu(a_u), jax.nn.gelu(a_v), a_ln_g, a_ws, a_bs) * jax.nn.silu(a_g)

    r_q, r_k, r_v, r_g = jnp.split(r_z, 4, axis=-1)
    hs = (b, s, R_HEADS, HEAD_DIM)
    r_out = _retention(r_q.reshape(hs), r_k.reshape(hs), r_v.reshape(hs)).reshape(b, s, R_WIDTH)
    r_out = r_out * jax.nn.silu(r_g)

    f_q, f_k, f_v, f_g, f_lg = jnp.split(
        f_z, [F_WIDTH, 2 * F_WIDTH, 3 * F_WIDTH, 4 * F_WIDTH], axis=-1)
    hs = (b, s, F_HEADS, HEAD_DIM)
    f_out = _forgetting_attention(f_q.reshape(hs), f_k.reshape(hs), f_v.reshape(hs), f_lg + b_f)
    f_out = f_out.reshape(b, s, F_WIDTH) * jax.nn.silu(f_g)

    y = jnp.concatenate([a_out, r_out, f_out], axis=-1)
    o = jnp.einsum('bse,ed->bsd', y, w_out)
    return x + _rms_norm(o, post_g)


def setup_inputs(seed: int = 0) -> dict:
    key = jax.random.key(seed)
    ks = jax.random.split(key, 9)
    f32 = jnp.float32
    x = jax.random.normal(ks[0], (BATCH, SEQ, D_MODEL), f32)
    pre_gain = 1.0 + 0.02 * jax.random.normal(ks[1], (DEPTH, D_MODEL), f32)
    post_gain = 1.0 + 0.02 * jax.random.normal(ks[2], (DEPTH, D_MODEL), f32)
    w_in = jax.random.normal(ks[3], (DEPTH, D_MODEL, D_IN), f32) * (D_MODEL ** -0.5)
    b_forget = 3.0 + 0.5 * jax.random.normal(ks[4], (DEPTH, F_HEADS), f32)
    a_norm_gain = 1.0 + 0.02 * jax.random.normal(ks[5], (DEPTH, A_GROUPS, HEAD_DIM), f32)
    a_spatial_w = jax.random.normal(ks[6], (DEPTH, A_GROUPS, A_BLOCK, A_BLOCK), f32) * (A_BLOCK ** -0.5)
    a_spatial_b = 1.0 + 0.02 * jax.random.normal(ks[7], (DEPTH, A_GROUPS, A_BLOCK), f32)
    w_out = jax.random.normal(ks[8], (DEPTH, D_MIX, D_MODEL), f32) * (D_MIX ** -0.5)
    return {"x": x, "pre_gain": pre_gain, "post_gain": post_gain, "w_in": w_in,
            "b_forget": b_forget, "a_norm_gain": a_norm_gain, "a_spatial_w": a_spatial_w,
            "a_spatial_b": a_spatial_b, "w_out": w_out}


def reference(x, pre_gain, post_gain, w_in, b_forget, a_norm_gain, a_spatial_w, a_spatial_b, w_out):
    for l in range(DEPTH):
        x = _hybrid_layer(x, pre_gain[l], post_gain[l], w_in[l], b_forget[l],
                          a_norm_gain[l], a_spatial_w[l], a_spatial_b[l], w_out[l])
    return x
```

```python
import functools
import math

import jax
import jax.numpy as jnp
import numpy as np
from jax import lax
from jax.experimental import pallas as pl
from jax.experimental.pallas import tpu as pltpu

F32 = jnp.float32
BF16 = jnp.bfloat16

D_MODEL = 1024
HEAD_DIM = 64
CHUNK = 64
A_WIDTH = 256
R_WIDTH = 384
F_WIDTH = 384
A_GROUPS = 4
R_HEADS = 6
F_HEADS = 6
A_BLOCK = 128
ROPE_THETA = 10000.0
EPS = 1e-6
LANES = 128
HEAD_PAIRS = R_WIDTH // LANES

A_COLS = 3 * A_WIDTH
R_COLS = 4 * R_WIDTH
F_COLS = 4 * F_WIDTH + F_HEADS
D_IN = A_COLS + R_COLS + F_COLS
D_IN_PAD = A_COLS + R_COLS + 4 * F_WIDTH + LANES

TM_IN = 512
TM_RET = 512
L_RET = 128
TQ = 256
TK = 256
TM_OUT = 512
VMEM_LIMIT = 48 * 1024 * 1024
MASK_VALUE = -1e30


def _silu(x):
    return x / (1.0 + jnp.exp(-x))


def _gelu_tanh(x):
    c = math.sqrt(2.0 / math.pi)
    return 0.5 * x * (1.0 + jnp.tanh(c * (x + 0.044715 * (x * x * x))))


def _log_sigmoid(x):
    return jnp.minimum(x, 0.0) - jnp.log1p(jnp.exp(-jnp.abs(x)))


def _dot(a, b):
    return jnp.dot(a, b, preferred_element_type=F32)


def _group_norm64(x, mavg):
    mean = _dot(x.astype(BF16), mavg)
    d = x - mean
    var = _dot((d * d).astype(BF16), mavg)
    return d * lax.rsqrt(var + EPS)


def _inproj_body(x_ref, pg_ref, w_ref, bf_ref, aln_ref, aws_ref, abias_ref, cos_ref, sin_ref,
                 tri_ref, mavg_ref,
                 a_ref, rq_ref, rk_ref, rv_ref, rg_ref, fq_ref, fk_ref, fv_ref, fg_ref, cf_ref,
                 carry_ref):
    tm = x_ref.shape[0]
    x = x_ref[...]
    ms = jnp.mean(x * x, axis=-1, keepdims=True)
    h = (x * lax.rsqrt(ms + EPS) * pg_ref[...]).astype(BF16)

    def proj(c0, n):
        return _dot(h, w_ref[:, c0:c0 + n])

    u = _gelu_tanh(proj(0, A_WIDTH))
    v = _gelu_tanh(proj(A_WIDTH, A_WIDTH))
    g = _silu(proj(2 * A_WIDTH, A_WIDTH))
    mavg = mavg_ref[...]
    vn = (_group_norm64(v, mavg) * aln_ref[...]).astype(BF16)
    row = lax.broadcasted_iota(jnp.int32, (A_BLOCK, A_BLOCK), 0)
    col = lax.broadcasted_iota(jnp.int32, (A_BLOCK, A_BLOCK), 1)
    allowed = jnp.logical_not(jnp.logical_and(row < CHUNK, col >= CHUNK))
    wcat = jnp.concatenate(
        [jnp.where(allowed, aws_ref[gi], 0.0).astype(BF16) for gi in range(A_GROUPS)], axis=1)
    lane_grp = lax.shift_right_logical(
        lax.broadcasted_iota(jnp.int32, (A_BLOCK, A_WIDTH), 1), HEAD_DIM.bit_length() - 1)
    zero_bf = jnp.zeros((A_BLOCK, A_WIDTH), BF16)
    mixed_blocks = []
    for nb in range(tm // A_BLOCK):
        vb = vn[nb * A_BLOCK:(nb + 1) * A_BLOCK, :]
        vstack = jnp.concatenate(
            [jnp.where(lane_grp == gi, vb, zero_bf) for gi in range(A_GROUPS)], axis=0)
        mixed_blocks.append(_dot(wcat, vstack) + abias_ref[...])
    mixed = jnp.concatenate(mixed_blocks, axis=0)
    a_ref[...] = (u * mixed * g).astype(BF16)

    lane = lax.broadcasted_iota(jnp.int32, (1, LANES), 1)
    first_half = jnp.bitwise_and(lane, HEAD_DIM - 1) < (HEAD_DIM // 2)
    cos_t = cos_ref[...]
    sin_t = sin_ref[...]

    def rope(z):
        outs = []
        for c in range(HEAD_PAIRS):
            zc = z[:, c * LANES:(c + 1) * LANES]
            rot = jnp.where(first_half, pltpu.roll(zc, LANES - HEAD_DIM // 2, 1),
                            pltpu.roll(zc, HEAD_DIM // 2, 1))
            outs.append(zc * cos_t + rot * sin_t)
        return jnp.concatenate(outs, axis=1)

    c0 = A_COLS
    rq_ref[...] = (rope(proj(c0, R_WIDTH)) * (HEAD_DIM ** -0.5)).astype(BF16)
    rk_ref[...] = rope(proj(c0 + R_WIDTH, R_WIDTH)).astype(BF16)
    rv_ref[...] = proj(c0 + 2 * R_WIDTH, R_WIDTH).astype(BF16)
    rg_ref[...] = _silu(proj(c0 + 3 * R_WIDTH, R_WIDTH)).astype(BF16)

    c0 = A_COLS + R_COLS
    fq_ref[...] = (proj(c0, F_WIDTH) * (HEAD_DIM ** -0.5)).astype(BF16)
    fk_ref[...] = proj(c0 + F_WIDTH, F_WIDTH).astype(BF16)
    fv_ref[...] = proj(c0 + 2 * F_WIDTH, F_WIDTH).astype(BF16)
    fg_ref[...] = _silu(proj(c0 + 3 * F_WIDTH, F_WIDTH)).astype(BF16)

    @pl.when(pl.program_id(1) == 0)
    def _():
        carry_ref[...] = jnp.zeros_like(carry_ref)

    ls = _log_sigmoid(proj(c0 + 4 * F_WIDTH, LANES) + bf_ref[...])
    ls_hi = ls.astype(BF16)
    ls_lo = (ls - ls_hi.astype(F32)).astype(BF16)
    tri = tri_ref[...]
    cf = _dot(tri, ls_hi) + _dot(tri, ls_lo) + carry_ref[...]
    cf_ref[...] = cf
    carry_ref[...] = cf[tm - 1:tm, :]


def _inproj(x, pre_g, w_pad, bf_pad, aln, aws, abias, cos_t, sin_t, tri, mavg):
    b, s, _ = x.shape
    tm = TM_IN
    grid = (b, s // tm)
    row_spec = lambda n: pl.BlockSpec((None, tm, n), lambda bi, si: (bi, si, 0))
    const2 = lambda shp: pl.BlockSpec(shp, lambda bi, si: (0, 0))
    bf_sds = lambda n: jax.ShapeDtypeStruct((b, s, n), BF16)
    out_shape = ([bf_sds(A_WIDTH)] + [bf_sds(R_WIDTH)] * 4 + [bf_sds(F_WIDTH)] * 4
                 + [jax.ShapeDtypeStruct((b, s, LANES), F32)])
    out_specs = ([row_spec(A_WIDTH)] + [row_spec(R_WIDTH)] * 4 + [row_spec(F_WIDTH)] * 4
                 + [row_spec(LANES)])
    in_specs = [
        row_spec(D_MODEL),
        const2((1, D_MODEL)),
        const2((D_MODEL, D_IN_PAD)),
        const2((1, LANES)),
        const2((1, A_WIDTH)),
        pl.BlockSpec((A_GROUPS, A_BLOCK, A_BLOCK), lambda bi, si: (0, 0, 0)),
        const2((A_BLOCK, A_WIDTH)),
        pl.BlockSpec((tm, LANES), lambda bi, si: (si, 0)),
        pl.BlockSpec((tm, LANES), lambda bi, si: (si, 0)),
        const2((tm, tm)),
        const2((A_WIDTH, A_WIDTH)),
    ]
    return pl.pallas_call(
        _inproj_body,
        grid=grid,
        in_specs=in_specs,
        out_specs=out_specs,
        out_shape=out_shape,
        scratch_shapes=[pltpu.VMEM((1, LANES), F32)],
        compiler_params=pltpu.CompilerParams(
            dimension_semantics=("arbitrary", "arbitrary"), vmem_limit_bytes=VMEM_LIMIT),
        name="inproj",
    )(x, pre_g, w_pad, bf_pad, aln, aws, abias, cos_t, sin_t, tri, mavg)


def _retention_body(q_ref, k_ref, v_ref, g_ref, dmat_ref, qdec_ref, kdec_ref, sdec_ref, bmask_ref,
                    mavg_ref, o_ref, state_ref):
    tm = q_ref.shape[0]

    @pl.when(pl.program_id(1) == 0)
    def _():
        state_ref[...] = jnp.zeros_like(state_ref)

    lane = lax.broadcasted_iota(jnp.int32, (1, LANES), 1)
    upper = lane >= HEAD_DIM
    mavg = mavg_ref[...]
    bmask = bmask_ref[...]
    contract_last = (((1,), (1,)), ((), ()))
    contract_first = (((0,), (0,)), ((), ()))
    for p in range(HEAD_PAIRS):
        cs = slice(p * LANES, (p + 1) * LANES)
        state = state_ref[p]
        for j in range(tm // L_RET):
            rs = slice(j * L_RET, (j + 1) * L_RET)
            q2 = q_ref[rs, cs]
            k2 = k_ref[rs, cs]
            v2 = v_ref[rs, cs]
            intra = []
            for hh in range(2):
                qh = jnp.where(upper == (hh == 1), q2, jnp.zeros_like(q2))
                sc = lax.dot_general(qh, k2, contract_last, preferred_element_type=F32)
                sc = sc * dmat_ref[2 * p + hh]
                intra.append(_dot(sc.astype(BF16), v2))
            out = jnp.where(upper, intra[1], intra[0])
            out = out + _dot(q2, state.astype(BF16)) * qdec_ref[p]
            kd = (k2.astype(F32) * kdec_ref[p]).astype(BF16)
            kv = lax.dot_general(kd, v2, contract_first, preferred_element_type=F32)
            state = state * sdec_ref[p] + kv * bmask
            y = _group_norm64(out, mavg) * g_ref[rs, cs].astype(F32)
            o_ref[rs, cs] = y.astype(BF16)
        state_ref[p] = state


def _retention(rq, rk, rv, rg, dmat, qdec, kdec, sdec, bmask, mavg128):
    b, s, _ = rq.shape
    tm = TM_RET
    row_spec = pl.BlockSpec((None, tm, R_WIDTH), lambda bi, si: (bi, si, 0))
    c3 = lambda shp: pl.BlockSpec(shp, lambda bi, si: (0, 0, 0))
    c2 = lambda shp: pl.BlockSpec(shp, lambda bi, si: (0, 0))
    return pl.pallas_call(
        _retention_body,
        grid=(b, s // tm),
        in_specs=[row_spec, row_spec, row_spec, row_spec,
                  c3((R_HEADS, L_RET, L_RET)), c3((HEAD_PAIRS, L_RET, LANES)),
                  c3((HEAD_PAIRS, L_RET, LANES)), c3((HEAD_PAIRS, LANES, LANES)),
                  c2((LANES, LANES)), c2((LANES, LANES))],
        out_specs=row_spec,
        out_shape=jax.ShapeDtypeStruct((b, s, R_WIDTH), BF16),
        scratch_shapes=[pltpu.VMEM((HEAD_PAIRS, LANES, LANES), F32)],
        compiler_params=pltpu.CompilerParams(
            dimension_semantics=("arbitrary", "arbitrary"), vmem_limit_bytes=VMEM_LIMIT),
        name="retention",
    )(rq, rk, rv, rg, dmat, qdec, kdec, sdec, bmask, mavg128)


def _fox_body(q_ref, k_ref, v_ref, cfq_ref, cfk_ref, g_ref, o_ref):
    pair = pl.program_id(1)
    qi = pl.program_id(2)
    lane = lax.broadcasted_iota(jnp.int32, (1, LANES), 1)
    upper = lane >= HEAD_DIM
    q2 = q_ref[...]
    cfq = cfq_ref[...]
    qh = []
    fq = []
    for hh in range(2):
        qh.append(jnp.where(upper == (hh == 1), q2, jnp.zeros_like(q2)))
        fq.append(jnp.sum(jnp.where(lane == 2 * pair + hh, cfq, 0.0), axis=-1, keepdims=True))
    contract_last = (((1,), (1,)), ((), ()))
    row = lax.broadcasted_iota(jnp.int32, (TQ, TK), 0)
    col = lax.broadcasted_iota(jnp.int32, (TQ, TK), 1)
    causal = col <= row

    def step(kj, carry, diagonal):
        k0 = pl.multiple_of(kj * TK, TK)
        kt = k_ref[pl.ds(k0, TK), :]
        vt = v_ref[pl.ds(k0, TK), :]
        new = []
        for hh in range(2):
            m, l, acc = carry[hh]
            s = lax.dot_general(qh[hh], kt, contract_last, preferred_element_type=F32)
            s = s - cfk_ref[pl.ds(2 * pair + hh, 1), pl.ds(k0, TK)]
            if diagonal:
                s = jnp.where(causal, s, MASK_VALUE)
            m_new = jnp.maximum(m, jnp.max(s, axis=-1, keepdims=True) + fq[hh])
            p = jnp.exp(s - (m_new - fq[hh]))
            alpha = jnp.exp(m - m_new)
            l = alpha * l + jnp.sum(p, axis=-1, keepdims=True)
            acc = alpha * acc + _dot(p.astype(BF16), vt)
            new.append((m_new, l, acc))
        return tuple(new)

    init = tuple((jnp.full((TQ, 1), MASK_VALUE, F32), jnp.zeros((TQ, 1), F32),
                  jnp.zeros((TQ, LANES), F32)) for _ in range(2))
    carry = lax.fori_loop(0, qi, lambda j, c: step(j, c, False), init)
    carry = step(qi, carry, True)
    out = jnp.where(upper, carry[1][2] / carry[1][1], carry[0][2] / carry[0][1])
    o_ref[...] = (out * g_ref[...].astype(F32)).astype(BF16)


def _fox(fq, fk, fv, fg, cf, cf_t):
    b, s, _ = fq.shape
    q_spec = pl.BlockSpec((None, TQ, LANES), lambda bi, pi, qi: (bi, qi, pi))
    kv_spec = pl.BlockSpec((None, s, LANES), lambda bi, pi, qi: (bi, 0, pi))
    return pl.pallas_call(
        _fox_body,
        grid=(b, HEAD_PAIRS, s // TQ),
        in_specs=[q_spec, kv_spec, kv_spec,
                  pl.BlockSpec((None, TQ, LANES), lambda bi, pi, qi: (bi, qi, 0)),
                  pl.BlockSpec((None, 8, s), lambda bi, pi, qi: (bi, 0, 0)),
                  q_spec],
        out_specs=q_spec,
        out_shape=jax.ShapeDtypeStruct((b, s, F_WIDTH), BF16),
        compiler_params=pltpu.CompilerParams(
            dimension_semantics=("arbitrary", "arbitrary", "arbitrary"),
            vmem_limit_bytes=VMEM_LIMIT),
        name="fox",
    )(fq, fk, fv, cf, cf_t, fg)


def _outproj_body(a_ref, r_ref, f_ref, wa_ref, wr_ref, wf_ref, x_ref, pg_ref, o_ref):
    o = _dot(a_ref[...], wa_ref[...]) + _dot(r_ref[...], wr_ref[...]) + _dot(f_ref[...], wf_ref[...])
    ms = jnp.mean(o * o, axis=-1, keepdims=True)
    o_ref[...] = x_ref[...] + o * lax.rsqrt(ms + EPS) * pg_ref[...]


def _outproj(a, r, f, wa, wr, wf, x, post_g):
    b, s, _ = x.shape
    tm = TM_OUT
    row_spec = lambda n: pl.BlockSpec((None, tm, n), lambda bi, si: (bi, si, 0))
    c2 = lambda shp: pl.BlockSpec(shp, lambda bi, si: (0, 0))
    return pl.pallas_call(
        _outproj_body,
        grid=(b, s // tm),
        in_specs=[row_spec(A_WIDTH), row_spec(R_WIDTH), row_spec(F_WIDTH),
                  c2((A_WIDTH, D_MODEL)), c2((R_WIDTH, D_MODEL)), c2((F_WIDTH, D_MODEL)),
                  row_spec(D_MODEL), c2((1, D_MODEL))],
        out_specs=row_spec(D_MODEL),
        out_shape=jax.ShapeDtypeStruct((b, s, D_MODEL), F32),
        compiler_params=pltpu.CompilerParams(
            dimension_semantics=("arbitrary", "arbitrary"), vmem_limit_bytes=VMEM_LIMIT),
        name="outproj",
    )(a, r, f, wa, wr, wf, x, post_g)


@functools.lru_cache(maxsize=None)
def _tables(seq):
    half = HEAD_DIM // 2
    lane = np.arange(LANES)
    inv = ROPE_THETA ** (-(np.arange(half, dtype=np.float64) / half))
    ang = np.arange(seq, dtype=np.float64)[:, None] * inv[None, :]
    cos = np.cos(ang)
    sin = np.sin(ang)
    cos_t = cos[:, lane % half]
    sin_t = sin[:, lane % half] * np.where((lane % HEAD_DIM) < half, -1.0, 1.0)[None, :]

    gam = 1.0 - np.exp2(-5.0 - np.arange(R_HEADS, dtype=np.float64))
    log_gam = np.log(gam)
    pos = np.arange(L_RET)
    dist = np.abs(pos[:, None] - pos[None, :])
    allowed = (pos[None, :] // CHUNK) <= (pos[:, None] // CHUNK)
    dmat = np.where(allowed[None], np.exp(log_gam[:, None, None] * dist[None]), 0.0)
    head_of_lane = lane // HEAD_DIM
    qdec = np.stack([np.exp(log_gam[2 * p + head_of_lane][None, :] * (pos + 1.0)[:, None])
                     for p in range(HEAD_PAIRS)])
    kdec = np.stack([np.exp(log_gam[2 * p + head_of_lane][None, :] * (L_RET - 1.0 - pos)[:, None])
                     for p in range(HEAD_PAIRS)])
    bmask = (head_of_lane[:, None] == head_of_lane[None, :]).astype(np.float32)
    sdec = np.stack([np.exp(log_gam[2 * p + head_of_lane] * L_RET)[:, None] * bmask
                     for p in range(HEAD_PAIRS)])
    tri = (np.arange(TM_IN)[None, :] <= np.arange(TM_IN)[:, None]).astype(np.float32)
    grp = np.arange(A_WIDTH) // HEAD_DIM
    mavg = (grp[:, None] == grp[None, :]).astype(np.float32) / HEAD_DIM
    f = lambda a: np.asarray(a, np.float32)
    return dict(cos=f(cos_t), sin=f(sin_t), dmat=f(dmat), qdec=f(qdec), kdec=f(kdec),
                sdec=f(sdec), bmask=f(bmask), tri=f(tri), mavg=f(mavg),
                mavg128=f(mavg[:LANES, :LANES]))


def _layer(x, pre_g, post_g, w_in, b_f, a_ln_g, a_ws, a_bs, w_out, t):
    w_pad = jnp.pad(w_in, ((0, 0), (0, D_IN_PAD - D_IN))).astype(BF16)
    bf_pad = jnp.pad(b_f, (0, LANES - F_HEADS)).reshape(1, LANES)
    abias = jnp.repeat(a_bs.T, HEAD_DIM, axis=1)
    (a_out, rq, rk, rv, rg, fq, fk, fv, fg, cf) = _inproj(
        x, pre_g.reshape(1, D_MODEL), w_pad, bf_pad, a_ln_g.reshape(1, A_WIDTH), a_ws, abias,
        t["cos"], t["sin"], t["tri"], t["mavg"])
    r_out = _retention(rq, rk, rv, rg, t["dmat"], t["qdec"], t["kdec"], t["sdec"], t["bmask"],
                       t["mavg128"])
    cf_t = jnp.transpose(cf[:, :, :8], (0, 2, 1))
    f_out = _fox(fq, fk, fv, fg, cf, cf_t)
    w_o = w_out.astype(BF16)
    return _outproj(a_out, r_out, f_out, w_o[:A_WIDTH], w_o[A_WIDTH:A_WIDTH + R_WIDTH],
                    w_o[A_WIDTH + R_WIDTH:], x, post_g.reshape(1, D_MODEL))


def kernel(x, pre_gain, post_gain, w_in, b_forget, a_norm_gain, a_spatial_w, a_spatial_b, w_out):
    bf16_tables = ("tri", "mavg", "mavg128")
    t = {k: jnp.asarray(v, BF16 if k in bf16_tables else F32) for k, v in _tables(x.shape[1]).items()}
    for l in range(pre_gain.shape[0]):
        x = _layer(x, pre_gain[l], post_gain[l], w_in[l], b_forget[l], a_norm_gain[l],
                   a_spatial_w[l], a_spatial_b[l], w_out[l], t)
    return x
```

```python
import functools
import math

import jax
import jax.numpy as jnp
import numpy as np
from jax import lax
from jax.experimental import pallas as pl
from jax.experimental.pallas import tpu as pltpu

F32 = jnp.float32
BF16 = jnp.bfloat16

D_MODEL = 1024
HEAD_DIM = 64
CHUNK = 64
A_WIDTH = 256
R_WIDTH = 384
F_WIDTH = 384
A_GROUPS = 4
R_HEADS = 6
F_HEADS = 6
A_BLOCK = 128
ROPE_THETA = 10000.0
EPS = 1e-6
LANES = 128
HEAD_PAIRS = R_WIDTH // LANES

A_COLS = 3 * A_WIDTH
R_COLS = 4 * R_WIDTH
F_COLS = 4 * F_WIDTH + F_HEADS
D_IN = A_COLS + R_COLS + F_COLS
D_IN_PAD = A_COLS + R_COLS + 4 * F_WIDTH + LANES

TM_IN = 512
TM_RET = 512
L_RET = 128
TQ = 256
TK = 256
TM_OUT = 512
VMEM_LIMIT = 48 * 1024 * 1024
LOG2E = 1.4426950408889634
FB_ROWS = 512
MASK_VALUE = -1e30


def _silu(x):
    return x / (1.0 + jnp.exp(-x))


def _gelu_tanh(x):
    c = math.sqrt(2.0 / math.pi)
    return 0.5 * x * (1.0 + jnp.tanh(c * (x + 0.044715 * (x * x * x))))


def _log_sigmoid(x):
    return jnp.minimum(x, 0.0) - jnp.log1p(jnp.exp(-jnp.abs(x)))


def _dot(a, b):
    return jnp.dot(a, b, preferred_element_type=F32)


def _group_norm64(x, mavg):
    mean = _dot(x.astype(BF16), mavg)
    d = x - mean
    var = _dot((d * d).astype(BF16), mavg)
    return d * lax.rsqrt(var + EPS)


def _inproj_body(x_ref, pg_ref, w_ref, bf_ref, aln_ref, aws_ref, abias_ref, cos_ref, sin_ref,
                 tri_ref, mavg_ref,
                 a_ref, rq_ref, rk_ref, rv_ref, rg_ref, fq_ref, fk_ref, fv_ref, fg_ref, cf_ref,
                 cft_ref, carry_ref):
    tm = x_ref.shape[0]
    x = x_ref[...]
    ms = jnp.mean(x * x, axis=-1, keepdims=True)
    h = (x * lax.rsqrt(ms + EPS) * pg_ref[...]).astype(BF16)

    def proj(c0, n):
        return _dot(h, w_ref[:, c0:c0 + n])

    u = _gelu_tanh(proj(0, A_WIDTH))
    v = _gelu_tanh(proj(A_WIDTH, A_WIDTH))
    g = _silu(proj(2 * A_WIDTH, A_WIDTH))
    mavg = mavg_ref[...]
    vn = (_group_norm64(v, mavg) * aln_ref[...]).astype(BF16)
    row = lax.broadcasted_iota(jnp.int32, (A_BLOCK, A_BLOCK), 0)
    col = lax.broadcasted_iota(jnp.int32, (A_BLOCK, A_BLOCK), 1)
    allowed = jnp.logical_not(jnp.logical_and(row < CHUNK, col >= CHUNK))
    wcat = jnp.concatenate(
        [jnp.where(allowed, aws_ref[gi], 0.0).astype(BF16) for gi in range(A_GROUPS)], axis=1)
    lane_grp = lax.shift_right_logical(
        lax.broadcasted_iota(jnp.int32, (A_BLOCK, A_WIDTH), 1), HEAD_DIM.bit_length() - 1)
    zero_bf = jnp.zeros((A_BLOCK, A_WIDTH), BF16)
    mixed_blocks = []
    for nb in range(tm // A_BLOCK):
        vb = vn[nb * A_BLOCK:(nb + 1) * A_BLOCK, :]
        vstack = jnp.concatenate(
            [jnp.where(lane_grp == gi, vb, zero_bf) for gi in range(A_GROUPS)], axis=0)
        mixed_blocks.append(_dot(wcat, vstack) + abias_ref[...])
    mixed = jnp.concatenate(mixed_blocks, axis=0)
    a_ref[...] = (u * mixed * g).astype(BF16)

    lane = lax.broadcasted_iota(jnp.int32, (1, LANES), 1)
    first_half = jnp.bitwise_and(lane, HEAD_DIM - 1) < (HEAD_DIM // 2)
    cos_t = cos_ref[...]
    sin_t = sin_ref[...]

    def rope(z):
        outs = []
        for c in range(HEAD_PAIRS):
            zc = z[:, c * LANES:(c + 1) * LANES]
            rot = jnp.where(first_half, pltpu.roll(zc, LANES - HEAD_DIM // 2, 1),
                            pltpu.roll(zc, HEAD_DIM // 2, 1))
            outs.append(zc * cos_t + rot * sin_t)
        return jnp.concatenate(outs, axis=1)

    c0 = A_COLS
    rq_ref[...] = (rope(proj(c0, R_WIDTH)) * (HEAD_DIM ** -0.5)).astype(BF16)
    rk_ref[...] = rope(proj(c0 + R_WIDTH, R_WIDTH)).astype(BF16)
    rv_ref[...] = proj(c0 + 2 * R_WIDTH, R_WIDTH).astype(BF16)
    rg_ref[...] = _silu(proj(c0 + 3 * R_WIDTH, R_WIDTH)).astype(BF16)

    c0 = A_COLS + R_COLS
    fq_ref[...] = (proj(c0, F_WIDTH) * (HEAD_DIM ** -0.5 * LOG2E)).T.astype(BF16)
    fk_ref[...] = proj(c0 + F_WIDTH, F_WIDTH).astype(BF16)
    fv_ref[...] = proj(c0 + 2 * F_WIDTH, F_WIDTH).T.astype(BF16)
    fg_ref[...] = _silu(proj(c0 + 3 * F_WIDTH, F_WIDTH)).astype(BF16)

    @pl.when(pl.program_id(1) == 0)
    def _():
        carry_ref[...] = jnp.zeros_like(carry_ref)

    ls = _log_sigmoid(proj(c0 + 4 * F_WIDTH, LANES) + bf_ref[...])
    ls_hi = ls.astype(BF16)
    ls_lo = (ls - ls_hi.astype(F32)).astype(BF16)
    tri = tri_ref[...]
    cf = _dot(tri, ls_hi) + _dot(tri, ls_lo) + carry_ref[...]
    cf2 = cf * LOG2E
    cf_ref[...] = cf2
    cft_ref[...] = cf2.T[:8, :]
    carry_ref[...] = cf[tm - 1:tm, :]


def _inproj(x, pre_g, w_pad, bf_pad, aln, aws, abias, cos_t, sin_t, tri, mavg):
    b, s, _ = x.shape
    tm = TM_IN
    grid = (b, s // tm)
    row_spec = lambda n: pl.BlockSpec((None, tm, n), lambda bi, si: (bi, si, 0))
    const2 = lambda shp: pl.BlockSpec(shp, lambda bi, si: (0, 0))
    col_spec = lambda n: pl.BlockSpec((None, n, tm), lambda bi, si: (bi, 0, si))
    bf_sds = lambda n: jax.ShapeDtypeStruct((b, s, n), BF16)
    bft_sds = lambda n: jax.ShapeDtypeStruct((b, n, s), BF16)
    out_shape = ([bf_sds(A_WIDTH)] + [bf_sds(R_WIDTH)] * 4
                 + [bft_sds(F_WIDTH), bf_sds(F_WIDTH), bft_sds(F_WIDTH), bf_sds(F_WIDTH)]
                 + [jax.ShapeDtypeStruct((b, s, LANES), F32), jax.ShapeDtypeStruct((b, 8, s), F32)])
    out_specs = ([row_spec(A_WIDTH)] + [row_spec(R_WIDTH)] * 4
                 + [col_spec(F_WIDTH), row_spec(F_WIDTH), col_spec(F_WIDTH), row_spec(F_WIDTH)]
                 + [row_spec(LANES), col_spec(8)])
    in_specs = [
        row_spec(D_MODEL),
        const2((1, D_MODEL)),
        const2((D_MODEL, D_IN_PAD)),
        const2((1, LANES)),
        const2((1, A_WIDTH)),
        pl.BlockSpec((A_GROUPS, A_BLOCK, A_BLOCK), lambda bi, si: (0, 0, 0)),
        const2((A_BLOCK, A_WIDTH)),
        pl.BlockSpec((tm, LANES), lambda bi, si: (si, 0)),
        pl.BlockSpec((tm, LANES), lambda bi, si: (si, 0)),
        const2((tm, tm)),
        const2((A_WIDTH, A_WIDTH)),
    ]
    return pl.pallas_call(
        _inproj_body,
        grid=grid,
        in_specs=in_specs,
        out_specs=out_specs,
        out_shape=out_shape,
        scratch_shapes=[pltpu.VMEM((1, LANES), F32)],
        compiler_params=pltpu.CompilerParams(
            dimension_semantics=("arbitrary", "arbitrary"), vmem_limit_bytes=VMEM_LIMIT),
        name="inproj",
    )(x, pre_g, w_pad, bf_pad, aln, aws, abias, cos_t, sin_t, tri, mavg)


def _retention_body(q_ref, k_ref, v_ref, g_ref, dmat_ref, qdec_ref, kdec_ref, sdec_ref, bmask_ref,
                    mavg_ref, o_ref, state_ref):
    tm = q_ref.shape[0]

    @pl.when(pl.program_id(1) == 0)
    def _():
        state_ref[...] = jnp.zeros_like(state_ref)

    lane = lax.broadcasted_iota(jnp.int32, (1, LANES), 1)
    upper = lane >= HEAD_DIM
    mavg = mavg_ref[...]
    bmask = bmask_ref[...]
    contract_last = (((1,), (1,)), ((), ()))
    contract_first = (((0,), (0,)), ((), ()))
    for p in range(HEAD_PAIRS):
        cs = slice(p * LANES, (p + 1) * LANES)
        state = state_ref[p]
        for j in range(tm // L_RET):
            rs = slice(j * L_RET, (j + 1) * L_RET)
            q2 = q_ref[rs, cs]
            k2 = k_ref[rs, cs]
            v2 = v_ref[rs, cs]
            intra = []
            for hh in range(2):
                qh = jnp.where(upper == (hh == 1), q2, jnp.zeros_like(q2))
                sc = lax.dot_general(qh, k2, contract_last, preferred_element_type=F32)
                sc = sc * dmat_ref[2 * p + hh]
                intra.append(_dot(sc.astype(BF16), v2))
            out = jnp.where(upper, intra[1], intra[0])
            out = out + _dot(q2, state.astype(BF16)) * qdec_ref[p]
            kd = (k2.astype(F32) * kdec_ref[p]).astype(BF16)
            kv = lax.dot_general(kd, v2, contract_first, preferred_element_type=F32)
            state = state * sdec_ref[p] + kv * bmask
            y = _group_norm64(out, mavg) * g_ref[rs, cs].astype(F32)
            o_ref[rs, cs] = y.astype(BF16)
        state_ref[p] = state


def _retention(rq, rk, rv, rg, dmat, qdec, kdec, sdec, bmask, mavg128):
    b, s, _ = rq.shape
    tm = TM_RET
    row_spec = pl.BlockSpec((None, tm, R_WIDTH), lambda bi, si: (bi, si, 0))
    c3 = lambda shp: pl.BlockSpec(shp, lambda bi, si: (0, 0, 0))
    c2 = lambda shp: pl.BlockSpec(shp, lambda bi, si: (0, 0))
    return pl.pallas_call(
        _retention_body,
        grid=(b, s // tm),
        in_specs=[row_spec, row_spec, row_spec, row_spec,
                  c3((R_HEADS, L_RET, L_RET)), c3((HEAD_PAIRS, L_RET, LANES)),
                  c3((HEAD_PAIRS, L_RET, LANES)), c3((HEAD_PAIRS, LANES, LANES)),
                  c2((LANES, LANES)), c2((LANES, LANES))],
        out_specs=row_spec,
        out_shape=jax.ShapeDtypeStruct((b, s, R_WIDTH), BF16),
        scratch_shapes=[pltpu.VMEM((HEAD_PAIRS, LANES, LANES), F32)],
        compiler_params=pltpu.CompilerParams(
            dimension_semantics=("arbitrary", "arbitrary"), vmem_limit_bytes=VMEM_LIMIT),
        name="retention",
    )(rq, rk, rv, rg, dmat, qdec, kdec, sdec, bmask, mavg128)


def _fox_body(qt_ref, k_ref, vt_ref, cf_ref, cft_ref, g_ref, o_ref,
              kaug_ref, vaug_ref, m_ref, acc_ref):
    seq = k_ref.shape[0]
    qi = pl.program_id(1)

    @pl.when(qi == 0)
    def _():
        lane = lax.broadcasted_iota(jnp.int32, (1, LANES), 1)
        srow = lax.broadcasted_iota(jnp.int32, (HEAD_DIM, 1), 0)
        ones_rows = jnp.broadcast_to(jnp.where(srow == 0, 1.0, 0.0), (HEAD_DIM, FB_ROWS)).astype(BF16)

        def fill(c, carry):
            r0 = pl.multiple_of(c * FB_ROWS, FB_ROWS)
            cfb = cf_ref[pl.ds(r0, FB_ROWS), :]
            for h in range(F_HEADS):
                p, hh = divmod(h, 2)
                negf = -jnp.broadcast_to(
                    jnp.sum(jnp.where(lane == h, cfb, 0.0), axis=-1, keepdims=True), (FB_ROWS, LANES))
                hi = negf.astype(BF16).astype(F32)
                mid = (negf - hi).astype(BF16).astype(F32)
                lo = (negf - hi - mid).astype(BF16).astype(F32)
                kk = k_ref[pl.ds(r0, FB_ROWS), p * LANES:(p + 1) * LANES].astype(F32)
                if hh == 1:
                    kk = pltpu.roll(kk, HEAD_DIM, 1)
                aug = jnp.where(lane == HEAD_DIM, hi,
                                jnp.where(lane == HEAD_DIM + 1, mid,
                                          jnp.where(lane == HEAD_DIM + 2, lo, 0.0)))
                kaug_ref[h, pl.ds(r0, FB_ROWS), :] = jnp.where(lane < HEAD_DIM, kk, aug).astype(BF16)
                vaug_ref[h, 0:HEAD_DIM, pl.ds(r0, FB_ROWS)] = (
                    vt_ref[h * HEAD_DIM:(h + 1) * HEAD_DIM, pl.ds(r0, FB_ROWS)])
                vaug_ref[h, HEAD_DIM:LANES, pl.ds(r0, FB_ROWS)] = ones_rows
            return carry

        lax.fori_loop(0, seq // FB_ROWS, fill, 0)

    srow_q = lax.broadcasted_iota(jnp.int32, (HEAD_DIM, 1), 0)
    q_ones = jnp.broadcast_to(jnp.where(srow_q < 3, 1.0, 0.0), (HEAD_DIM, TQ)).astype(BF16)
    qta = [jnp.concatenate([qt_ref[h * HEAD_DIM:(h + 1) * HEAD_DIM, :], q_ones], axis=0)
           for h in range(F_HEADS)]
    fq = [cft_ref[h:h + 1, :] for h in range(F_HEADS)]
    krow = lax.broadcasted_iota(jnp.int32, (TK, TQ), 0)
    qcol = lax.broadcasted_iota(jnp.int32, (TK, TQ), 1)
    causal = krow <= qcol

    for h in range(F_HEADS):
        m_ref[h] = jnp.full((8, TQ), MASK_VALUE, F32)
        acc_ref[h] = jnp.zeros((LANES, TQ), F32)

    def step(kj, diagonal):
        k0 = pl.multiple_of(kj * TK, TK)
        sts = [_dot(kaug_ref[h, pl.ds(k0, TK), :], qta[h]) for h in range(F_HEADS)]
        for h in range(F_HEADS):
            st = sts[h]
            if diagonal:
                st = jnp.where(causal, st, MASK_VALUE)
            m_old = m_ref[h][0:1, :]
            m_new = jnp.maximum(m_old, jnp.max(st, axis=0, keepdims=True) + fq[h])
            p = jnp.exp2(st - (m_new - fq[h]))
            alpha = jnp.exp2(m_old - m_new)
            pv = _dot(vaug_ref[h, :, pl.ds(k0, TK)], p.astype(BF16))
            acc_ref[h] = alpha * acc_ref[h] + pv
            m_ref[h] = jnp.broadcast_to(m_new, (8, TQ))

    def body(j, carry):
        step(j, False)
        return carry

    lax.fori_loop(0, qi, body, 0)
    step(qi, True)

    for p in range(HEAD_PAIRS):
        halves = []
        for hh in range(2):
            acc = acc_ref[2 * p + hh]
            halves.append(acc[0:HEAD_DIM, :] / acc[HEAD_DIM:HEAD_DIM + 1, :])
        out_t = jnp.concatenate(halves, axis=0)
        cs = slice(p * LANES, (p + 1) * LANES)
        o_ref[:, cs] = (out_t.T * g_ref[:, cs].astype(F32)).astype(BF16)


def _fox(fqt, fk, fvt, fg, cf, cf_t):
    b, s, _ = fk.shape
    assert TQ == TK
    row_spec = pl.BlockSpec((None, TQ, F_WIDTH), lambda bi, qi: (bi, qi, 0))
    return pl.pallas_call(
        _fox_body,
        grid=(b, s // TQ),
        in_specs=[pl.BlockSpec((None, F_WIDTH, TQ), lambda bi, qi: (bi, 0, qi)),
                  pl.BlockSpec((None, s, F_WIDTH), lambda bi, qi: (bi, 0, 0)),
                  pl.BlockSpec((None, F_WIDTH, s), lambda bi, qi: (bi, 0, 0)),
                  pl.BlockSpec((None, s, LANES), lambda bi, qi: (bi, 0, 0)),
                  pl.BlockSpec((None, 8, TQ), lambda bi, qi: (bi, 0, qi)),
                  row_spec],
        out_specs=row_spec,
        out_shape=jax.ShapeDtypeStruct((b, s, F_WIDTH), BF16),
        scratch_shapes=[pltpu.VMEM((F_HEADS, s, LANES), BF16),
                        pltpu.VMEM((F_HEADS, LANES, s), BF16),
                        pltpu.VMEM((F_HEADS, 8, TQ), F32),
                        pltpu.VMEM((F_HEADS, LANES, TQ), F32)],
        compiler_params=pltpu.CompilerParams(
            dimension_semantics=("arbitrary", "arbitrary"), vmem_limit_bytes=VMEM_LIMIT),
        name="fox",
    )(fqt, fk, fvt, cf, cf_t, fg)


def _outproj_body(a_ref, r_ref, f_ref, wa_ref, wr_ref, wf_ref, x_ref, pg_ref, o_ref):
    o = _dot(a_ref[...], wa_ref[...]) + _dot(r_ref[...], wr_ref[...]) + _dot(f_ref[...], wf_ref[...])
    ms = jnp.mean(o * o, axis=-1, keepdims=True)
    o_ref[...] = x_ref[...] + o * lax.rsqrt(ms + EPS) * pg_ref[...]


def _outproj(a, r, f, wa, wr, wf, x, post_g):
    b, s, _ = x.shape
    tm = TM_OUT
    row_spec = lambda n: pl.BlockSpec((None, tm, n), lambda bi, si: (bi, si, 0))
    c2 = lambda shp: pl.BlockSpec(shp, lambda bi, si: (0, 0))
    return pl.pallas_call(
        _outproj_body,
        grid=(b, s // tm),
        in_specs=[row_spec(A_WIDTH), row_spec(R_WIDTH), row_spec(F_WIDTH),
                  c2((A_WIDTH, D_MODEL)), c2((R_WIDTH, D_MODEL)), c2((F_WIDTH, D_MODEL)),
                  row_spec(D_MODEL), c2((1, D_MODEL))],
        out_specs=row_spec(D_MODEL),
        out_shape=jax.ShapeDtypeStruct((b, s, D_MODEL), F32),
        compiler_params=pltpu.CompilerParams(
            dimension_semantics=("arbitrary", "arbitrary"), vmem_limit_bytes=VMEM_LIMIT),
        name="outproj",
    )(a, r, f, wa, wr, wf, x, post_g)


@functools.lru_cache(maxsize=None)
def _tables(seq):
    half = HEAD_DIM // 2
    lane = np.arange(LANES)
    inv = ROPE_THETA ** (-(np.arange(half, dtype=np.float64) / half))
    ang = np.arange(seq, dtype=np.float64)[:, None] * inv[None, :]
    cos = np.cos(ang)
    sin = np.sin(ang)
    cos_t = cos[:, lane % half]
    sin_t = sin[:, lane % half] * np.where((lane % HEAD_DIM) < half, -1.0, 1.0)[None, :]

    gam = 1.0 - np.exp2(-5.0 - np.arange(R_HEADS, dtype=np.float64))
    log_gam = np.log(gam)
    pos = np.arange(L_RET)
    dist = np.abs(pos[:, None] - pos[None, :])
    allowed = (pos[None, :] // CHUNK) <= (pos[:, None] // CHUNK)
    dmat = np.where(allowed[None], np.exp(log_gam[:, None, None] * dist[None]), 0.0)
    head_of_lane = lane // HEAD_DIM
    qdec = np.stack([np.exp(log_gam[2 * p + head_of_lane][None, :] * (pos + 1.0)[:, None])
                     for p in range(HEAD_PAIRS)])
    kdec = np.stack([np.exp(log_gam[2 * p + head_of_lane][None, :] * (L_RET - 1.0 - pos)[:, None])
                     for p in range(HEAD_PAIRS)])
    bmask = (head_of_lane[:, None] == head_of_lane[None, :]).astype(np.float32)
    sdec = np.stack([np.exp(log_gam[2 * p + head_of_lane] * L_RET)[:, None] * bmask
                     for p in range(HEAD_PAIRS)])
    tri = (np.arange(TM_IN)[None, :] <= np.arange(TM_IN)[:, None]).astype(np.float32)
    grp = np.arange(A_WIDTH) // HEAD_DIM
    mavg = (grp[:, None] == grp[None, :]).astype(np.float32) / HEAD_DIM
    f = lambda a: np.asarray(a, np.float32)
    return dict(cos=f(cos_t), sin=f(sin_t), dmat=f(dmat), qdec=f(qdec), kdec=f(kdec),
                sdec=f(sdec), bmask=f(bmask), tri=f(tri), mavg=f(mavg),
                mavg128=f(mavg[:LANES, :LANES]))


def _layer(x, pre_g, post_g, w_in, b_f, a_ln_g, a_ws, a_bs, w_out, t):
    w_pad = jnp.pad(w_in, ((0, 0), (0, D_IN_PAD - D_IN))).astype(BF16)
    bf_pad = jnp.pad(b_f, (0, LANES - F_HEADS)).reshape(1, LANES)
    abias = jnp.repeat(a_bs.T, HEAD_DIM, axis=1)
    (a_out, rq, rk, rv, rg, fqt, fk, fvt, fg, cf, cf_t) = _inproj(
        x, pre_g.reshape(1, D_MODEL), w_pad, bf_pad, a_ln_g.reshape(1, A_WIDTH), a_ws, abias,
        t["cos"], t["sin"], t["tri"], t["mavg"])
    r_out = _retention(rq, rk, rv, rg, t["dmat"], t["qdec"], t["kdec"], t["sdec"], t["bmask"],
                       t["mavg128"])
    f_out = _fox(fqt, fk, fvt, fg, cf, cf_t)
    w_o = w_out.astype(BF16)
    return _outproj(a_out, r_out, f_out, w_o[:A_WIDTH], w_o[A_WIDTH:A_WIDTH + R_WIDTH],
                    w_o[A_WIDTH + R_WIDTH:], x, post_g.reshape(1, D_MODEL))


def kernel(x, pre_gain, post_gain, w_in, b_forget, a_norm_gain, a_spatial_w, a_spatial_b, w_out):
    bf16_tables = ("tri", "mavg", "mavg128")
    t = {k: jnp.asarray(v, BF16 if k in bf16_tables else F32) for k, v in _tables(x.shape[1]).items()}
    for l in range(pre_gain.shape[0]):
        x = _layer(x, pre_gain[l], post_gain[l], w_in[l], b_forget[l], a_norm_gain[l],
                   a_spatial_w[l], a_spatial_b[l], w_out[l], t)
    return x
```

```python
import functools
import math

import jax
import jax.numpy as jnp
import numpy as np
from jax import lax
from jax.experimental import pallas as pl
from jax.experimental.pallas import tpu as pltpu

F32 = jnp.float32
BF16 = jnp.bfloat16

D_MODEL = 1024
HEAD_DIM = 64
CHUNK = 64
A_WIDTH = 256
R_WIDTH = 384
F_WIDTH = 384
A_GROUPS = 4
R_HEADS = 6
F_HEADS = 6
A_BLOCK = 128
ROPE_THETA = 10000.0
EPS = 1e-6
LANES = 128
HEAD_PAIRS = R_WIDTH // LANES

A_COLS = 3 * A_WIDTH
R_COLS = 4 * R_WIDTH
F_COLS = 4 * F_WIDTH + F_HEADS
D_IN = A_COLS + R_COLS + F_COLS
D_IN_PAD = A_COLS + R_COLS + 4 * F_WIDTH + LANES

TM_IN = 512
RC_IN = 256
TM_RET = 512
L_RET = 128
TQ = 256
TK = 256
TM_OUT = 512
RC_OUT = 256
VMEM_LIMIT = 48 * 1024 * 1024
LOG2E = 1.4426950408889634
MASK_VALUE = -1e30


def _silu(x):
    return x / (1.0 + jnp.exp(-x))


def _gelu_tanh(x):
    c = math.sqrt(2.0 / math.pi)
    return 0.5 * x * (1.0 + jnp.tanh(c * (x + 0.044715 * (x * x * x))))


def _log_sigmoid(x):
    return jnp.minimum(x, 0.0) - jnp.log1p(jnp.exp(-jnp.abs(x)))


def _dot(a, b):
    return jnp.dot(a, b, preferred_element_type=F32)


def _group_norm64(x, mavg):
    mean = _dot(x.astype(BF16), mavg)
    d = x - mean
    var = _dot((d * d).astype(BF16), mavg)
    return d * lax.rsqrt(var + EPS)


def _inproj_body(x_ref, pg_ref, w_ref, bf_ref, aln_ref, aws_ref, abias_ref, cos_ref, sin_ref,
                 tri_ref, mavg_ref,
                 a_ref, rq_ref, rk_ref, rv_ref, rg_ref, fq_ref, kaug_ref, vaug_ref, fg_ref, cft_ref,
                 carry_ref):
    tm = x_ref.shape[0]

    @pl.when(pl.program_id(1) == 0)
    def _():
        carry_ref[...] = jnp.zeros_like(carry_ref)

    mavg = mavg_ref[...]
    tri = tri_ref[...]
    row = lax.broadcasted_iota(jnp.int32, (A_BLOCK, A_BLOCK), 0)
    col = lax.broadcasted_iota(jnp.int32, (A_BLOCK, A_BLOCK), 1)
    allowed = jnp.logical_not(jnp.logical_and(row < CHUNK, col >= CHUNK))
    wcat = jnp.concatenate(
        [jnp.where(allowed, aws_ref[gi], 0.0).astype(BF16) for gi in range(A_GROUPS)], axis=1)
    lane_grp = lax.shift_right_logical(
        lax.broadcasted_iota(jnp.int32, (A_BLOCK, A_WIDTH), 1), HEAD_DIM.bit_length() - 1)
    zero_bf = jnp.zeros((A_BLOCK, A_WIDTH), BF16)
    lane = lax.broadcasted_iota(jnp.int32, (1, LANES), 1)
    first_half = jnp.bitwise_and(lane, HEAD_DIM - 1) < (HEAD_DIM // 2)
    srow = lax.broadcasted_iota(jnp.int32, (HEAD_DIM, 1), 0)
    ones_rows = jnp.broadcast_to(jnp.where(srow == 0, 1.0, 0.0), (HEAD_DIM, RC_IN)).astype(BF16)

    def rope(z, cos_t, sin_t):
        outs = []
        for c in range(HEAD_PAIRS):
            zc = z[:, c * LANES:(c + 1) * LANES]
            rot = jnp.where(first_half, pltpu.roll(zc, LANES - HEAD_DIM // 2, 1),
                            pltpu.roll(zc, HEAD_DIM // 2, 1))
            outs.append(zc * cos_t + rot * sin_t)
        return jnp.concatenate(outs, axis=1)

    for c in range(tm // RC_IN):
        rows = slice(c * RC_IN, (c + 1) * RC_IN)
        x = x_ref[rows, :]
        ms = jnp.mean(x * x, axis=-1, keepdims=True)
        h = (x * lax.rsqrt(ms + EPS) * pg_ref[...]).astype(BF16)
        za = _dot(h, w_ref[:, 0:A_COLS])
        zr = _dot(h, w_ref[:, A_COLS:A_COLS + R_COLS])
        zf = _dot(h, w_ref[:, A_COLS + R_COLS:D_IN_PAD])

        u = _gelu_tanh(za[:, 0:A_WIDTH])
        v = _gelu_tanh(za[:, A_WIDTH:2 * A_WIDTH])
        g = _silu(za[:, 2 * A_WIDTH:3 * A_WIDTH])
        vn = (_group_norm64(v, mavg) * aln_ref[...]).astype(BF16)
        mixed_blocks = []
        for nb in range(RC_IN // A_BLOCK):
            vb = vn[nb * A_BLOCK:(nb + 1) * A_BLOCK, :]
            vstack = jnp.concatenate(
                [jnp.where(lane_grp == gi, vb, zero_bf) for gi in range(A_GROUPS)], axis=0)
            mixed_blocks.append(_dot(wcat, vstack) + abias_ref[...])
        mixed = jnp.concatenate(mixed_blocks, axis=0)
        a_ref[rows, :] = (u * mixed * g).astype(BF16)

        cos_t = cos_ref[rows, :]
        sin_t = sin_ref[rows, :]
        rq_ref[rows, :] = (rope(zr[:, 0:R_WIDTH], cos_t, sin_t) * (HEAD_DIM ** -0.5)).astype(BF16)
        rk_ref[rows, :] = rope(zr[:, R_WIDTH:2 * R_WIDTH], cos_t, sin_t).astype(BF16)
        rv_ref[rows, :] = zr[:, 2 * R_WIDTH:3 * R_WIDTH].astype(BF16)
        rg_ref[rows, :] = _silu(zr[:, 3 * R_WIDTH:4 * R_WIDTH]).astype(BF16)

        fq_ref[:, rows] = (zf[:, 0:F_WIDTH] * (HEAD_DIM ** -0.5 * LOG2E)).T.astype(BF16)
        fg_ref[rows, :] = _silu(zf[:, 3 * F_WIDTH:4 * F_WIDTH]).astype(BF16)

        ls = _log_sigmoid(zf[:, 4 * F_WIDTH:4 * F_WIDTH + LANES] + bf_ref[...])
        ls_hi = ls.astype(BF16)
        ls_lo = (ls - ls_hi.astype(F32)).astype(BF16)
        cf = _dot(tri, ls_hi) + _dot(tri, ls_lo) + carry_ref[...]
        carry_ref[...] = cf[RC_IN - 1:RC_IN, :]
        cf2 = cf * LOG2E
        cft_ref[:, rows] = cf2.T[:8, :]

        fv_t = zf[:, 2 * F_WIDTH:3 * F_WIDTH].T.astype(BF16)
        for hd in range(F_HEADS):
            p, hh = divmod(hd, 2)
            negf = -jnp.broadcast_to(
                jnp.sum(jnp.where(lane == hd, cf2, 0.0), axis=-1, keepdims=True), (RC_IN, LANES))
            hi = negf.astype(BF16).astype(F32)
            mid = (negf - hi).astype(BF16).astype(F32)
            lo = (negf - hi - mid).astype(BF16).astype(F32)
            kk = zf[:, F_WIDTH + p * LANES:F_WIDTH + (p + 1) * LANES]
            if hh == 1:
                kk = pltpu.roll(kk, HEAD_DIM, 1)
            aug = jnp.where(lane == HEAD_DIM, hi,
                            jnp.where(lane == HEAD_DIM + 1, mid,
                                      jnp.where(lane == HEAD_DIM + 2, lo, 0.0)))
            kaug_ref[hd, rows, :] = jnp.where(lane < HEAD_DIM, kk, aug).astype(BF16)
            vaug_ref[hd, 0:HEAD_DIM, rows] = fv_t[hd * HEAD_DIM:(hd + 1) * HEAD_DIM, :]
            vaug_ref[hd, HEAD_DIM:LANES, rows] = ones_rows


def _inproj(x, pre_g, w_pad, bf_pad, aln, aws, abias, cos_t, sin_t, tri, mavg):
    b, s, _ = x.shape
    tm = TM_IN
    grid = (b, s // tm)
    row_spec = lambda n: pl.BlockSpec((None, tm, n), lambda bi, si: (bi, si, 0))
    col_spec = lambda n: pl.BlockSpec((None, n, tm), lambda bi, si: (bi, 0, si))
    const2 = lambda shp: pl.BlockSpec(shp, lambda bi, si: (0, 0))
    bf_sds = lambda n: jax.ShapeDtypeStruct((b, s, n), BF16)
    out_shape = ([bf_sds(A_WIDTH)] + [bf_sds(R_WIDTH)] * 4
                 + [jax.ShapeDtypeStruct((b, F_WIDTH, s), BF16),
                    jax.ShapeDtypeStruct((b, F_HEADS, s, LANES), BF16),
                    jax.ShapeDtypeStruct((b, F_HEADS, LANES, s), BF16),
                    bf_sds(F_WIDTH),
                    jax.ShapeDtypeStruct((b, 8, s), F32)])
    out_specs = ([row_spec(A_WIDTH)] + [row_spec(R_WIDTH)] * 4
                 + [col_spec(F_WIDTH),
                    pl.BlockSpec((None, F_HEADS, tm, LANES), lambda bi, si: (bi, 0, si, 0)),
                    pl.BlockSpec((None, F_HEADS, LANES, tm), lambda bi, si: (bi, 0, 0, si)),
                    row_spec(F_WIDTH),
                    col_spec(8)])
    in_specs = [
        row_spec(D_MODEL),
        const2((1, D_MODEL)),
        const2((D_MODEL, D_IN_PAD)),
        const2((1, LANES)),
        const2((1, A_WIDTH)),
        pl.BlockSpec((A_GROUPS, A_BLOCK, A_BLOCK), lambda bi, si: (0, 0, 0)),
        const2((A_BLOCK, A_WIDTH)),
        pl.BlockSpec((tm, LANES), lambda bi, si: (si, 0)),
        pl.BlockSpec((tm, LANES), lambda bi, si: (si, 0)),
        const2((RC_IN, RC_IN)),
        const2((A_WIDTH, A_WIDTH)),
    ]
    return pl.pallas_call(
        _inproj_body,
        grid=grid,
        in_specs=in_specs,
        out_specs=out_specs,
        out_shape=out_shape,
        scratch_shapes=[pltpu.VMEM((1, LANES), F32)],
        compiler_params=pltpu.CompilerParams(
            dimension_semantics=("arbitrary", "arbitrary"), vmem_limit_bytes=VMEM_LIMIT),
        name="inproj",
    )(x, pre_g, w_pad, bf_pad, aln, aws, abias, cos_t, sin_t, tri, mavg)


def _retention_body(q_ref, k_ref, v_ref, g_ref, dmat_ref, qdec_ref, kdec_ref, sdec_ref, bmask_ref,
                    mavg_ref, o_ref, state_ref):
    tm = q_ref.shape[0]

    @pl.when(pl.program_id(1) == 0)
    def _():
        state_ref[...] = jnp.zeros_like(state_ref)

    lane = lax.broadcasted_iota(jnp.int32, (1, LANES), 1)
    lower = lane < HEAD_DIM
    mavg = mavg_ref[...]
    bmask = bmask_ref[...]
    contract_last = (((1,), (1,)), ((), ()))
    contract_first = (((0,), (0,)), ((), ()))
    nblk = tm // L_RET
    units = [(p, j) for p in range(HEAD_PAIRS) for j in range(nblk)]

    def split_heads(x2):
        zero = jnp.zeros_like(x2)
        return jnp.concatenate([jnp.where(lower, x2, zero), jnp.where(lower, zero, x2)], axis=0)

    q2, v2, sc, kv = {}, {}, {}, {}
    for (p, j) in units:
        rs, cs = slice(j * L_RET, (j + 1) * L_RET), slice(p * LANES, (p + 1) * LANES)
        q2[p, j] = q_ref[rs, cs]
        k2 = k_ref[rs, cs]
        v2[p, j] = v_ref[rs, cs]
        sc[p, j] = lax.dot_general(q2[p, j], split_heads(k2), contract_last,
                                   preferred_element_type=F32)
        kd = (k2.astype(F32) * kdec_ref[p]).astype(BF16)
        kv[p, j] = lax.dot_general(kd, v2[p, j], contract_first, preferred_element_type=F32)

    st = {}
    for p in range(HEAD_PAIRS):
        state = state_ref[p]
        for j in range(nblk):
            st[p, j] = state.astype(BF16)
            state = state * sdec_ref[p] + kv[p, j] * bmask
        state_ref[p] = state

    out = {}
    for (p, j) in units:
        pm = (sc[p, j] * dmat_ref[p]).astype(BF16)
        out[p, j] = _dot(pm, split_heads(v2[p, j])) + _dot(q2[p, j], st[p, j]) * qdec_ref[p]

    tiles = [jnp.concatenate([out[p, j] for j in range(nblk)], axis=0) for p in range(HEAD_PAIRS)]
    means = [_dot(t.astype(BF16), mavg) for t in tiles]
    devs = [t - mu for t, mu in zip(tiles, means)]
    variances = [_dot((d * d).astype(BF16), mavg) for d in devs]
    for p in range(HEAD_PAIRS):
        cs = slice(p * LANES, (p + 1) * LANES)
        y = devs[p] * lax.rsqrt(variances[p] + EPS) * g_ref[:, cs].astype(F32)
        o_ref[:, cs] = y.astype(BF16)


def _retention(rq, rk, rv, rg, dmat, qdec, kdec, sdec, bmask, mavg128):
    b, s, _ = rq.shape
    tm = TM_RET
    row_spec = pl.BlockSpec((None, tm, R_WIDTH), lambda bi, si: (bi, si, 0))
    c3 = lambda shp: pl.BlockSpec(shp, lambda bi, si: (0, 0, 0))
    c2 = lambda shp: pl.BlockSpec(shp, lambda bi, si: (0, 0))
    return pl.pallas_call(
        _retention_body,
        grid=(b, s // tm),
        in_specs=[row_spec, row_spec, row_spec, row_spec,
                  c3((HEAD_PAIRS, L_RET, 2 * L_RET)), c3((HEAD_PAIRS, L_RET, LANES)),
                  c3((HEAD_PAIRS, L_RET, LANES)), c3((HEAD_PAIRS, LANES, LANES)),
                  c2((LANES, LANES)), c2((LANES, LANES))],
        out_specs=row_spec,
        out_shape=jax.ShapeDtypeStruct((b, s, R_WIDTH), BF16),
        scratch_shapes=[pltpu.VMEM((HEAD_PAIRS, LANES, LANES), F32)],
        compiler_params=pltpu.CompilerParams(
            dimension_semantics=("arbitrary", "arbitrary"), vmem_limit_bytes=VMEM_LIMIT),
        name="retention",
    )(rq, rk, rv, rg, dmat, qdec, kdec, sdec, bmask, mavg128)


def _fox_body(qt_ref, kaug_ref, vaug_ref, cft_ref, g_ref, o_ref, m_ref, acc_ref):
    qi = pl.program_id(1)
    srow_q = lax.broadcasted_iota(jnp.int32, (HEAD_DIM, 1), 0)
    q_ones = jnp.broadcast_to(jnp.where(srow_q < 3, 1.0, 0.0), (HEAD_DIM, TQ)).astype(BF16)
    qta = [jnp.concatenate([qt_ref[h * HEAD_DIM:(h + 1) * HEAD_DIM, :], q_ones], axis=0)
           for h in range(F_HEADS)]
    fq = [cft_ref[h:h + 1, :] for h in range(F_HEADS)]
    krow = lax.broadcasted_iota(jnp.int32, (TK, TQ), 0)
    qcol = lax.broadcasted_iota(jnp.int32, (TK, TQ), 1)
    causal = krow <= qcol

    for h in range(F_HEADS):
        m_ref[h] = jnp.full((8, TQ), MASK_VALUE, F32)
        acc_ref[h] = jnp.zeros((LANES, TQ), F32)

    def step(kj, diagonal):
        k0 = pl.multiple_of(kj * TK, TK)
        sts = [_dot(kaug_ref[h, pl.ds(k0, TK), :], qta[h]) for h in range(F_HEADS)]
        for h in range(F_HEADS):
            st = sts[h]
            if diagonal:
                st = jnp.where(causal, st, MASK_VALUE)
            m_old = m_ref[h][0:1, :]
            m_new = jnp.maximum(m_old, jnp.max(st, axis=0, keepdims=True) + fq[h])
            p = jnp.exp2(st - (m_new - fq[h]))
            alpha = jnp.exp2(m_old - m_new)
            pv = _dot(vaug_ref[h, :, pl.ds(k0, TK)], p.astype(BF16))
            acc_ref[h] = alpha * acc_ref[h] + pv
            m_ref[h] = jnp.broadcast_to(m_new, (8, TQ))

    def body(j, carry):
        step(j, False)
        return carry

    lax.fori_loop(0, qi, body, 0)
    step(qi, True)

    for p in range(HEAD_PAIRS):
        halves = []
        for hh in range(2):
            acc = acc_ref[2 * p + hh]
            halves.append(acc[0:HEAD_DIM, :] / acc[HEAD_DIM:HEAD_DIM + 1, :])
        out_t = jnp.concatenate(halves, axis=0)
        cs = slice(p * LANES, (p + 1) * LANES)
        o_ref[:, cs] = (out_t.T * g_ref[:, cs].astype(F32)).astype(BF16)


def _fox(fqt, kaug, vaug, fg, cf_t):
    b, _, s = fqt.shape
    assert TQ == TK
    row_spec = pl.BlockSpec((None, TQ, F_WIDTH), lambda bi, qi: (bi, qi, 0))
    return pl.pallas_call(
        _fox_body,
        grid=(b, s // TQ),
        in_specs=[pl.BlockSpec((None, F_WIDTH, TQ), lambda bi, qi: (bi, 0, qi)),
                  pl.BlockSpec((None, F_HEADS, s, LANES), lambda bi, qi: (bi, 0, 0, 0)),
                  pl.BlockSpec((None, F_HEADS, LANES, s), lambda bi, qi: (bi, 0, 0, 0)),
                  pl.BlockSpec((None, 8, TQ), lambda bi, qi: (bi, 0, qi)),
                  row_spec],
        out_specs=row_spec,
        out_shape=jax.ShapeDtypeStruct((b, s, F_WIDTH), BF16),
        scratch_shapes=[pltpu.VMEM((F_HEADS, 8, TQ), F32),
                        pltpu.VMEM((F_HEADS, LANES, TQ), F32)],
        compiler_params=pltpu.CompilerParams(
            dimension_semantics=("arbitrary", "arbitrary"), vmem_limit_bytes=VMEM_LIMIT),
        name="fox",
    )(fqt, kaug, vaug, cf_t, fg)


def _outproj_body(a_ref, r_ref, f_ref, w_ref, x_ref, pg_ref, o_ref):
    tm = x_ref.shape[0]
    for c in range(tm // RC_OUT):
        rows = slice(c * RC_OUT, (c + 1) * RC_OUT)
        y = jnp.concatenate([a_ref[rows, :], r_ref[rows, :], f_ref[rows, :]], axis=1)
        o = _dot(y, w_ref[...])
        ms = jnp.mean(o * o, axis=-1, keepdims=True)
        o_ref[rows, :] = x_ref[rows, :] + o * lax.rsqrt(ms + EPS) * pg_ref[...]


def _outproj(a, r, f, w, x, post_g):
    b, s, _ = x.shape
    tm = TM_OUT
    row_spec = lambda n: pl.BlockSpec((None, tm, n), lambda bi, si: (bi, si, 0))
    c2 = lambda shp: pl.BlockSpec(shp, lambda bi, si: (0, 0))
    return pl.pallas_call(
        _outproj_body,
        grid=(b, s // tm),
        in_specs=[row_spec(A_WIDTH), row_spec(R_WIDTH), row_spec(F_WIDTH),
                  c2((D_MODEL, D_MODEL)), row_spec(D_MODEL), c2((1, D_MODEL))],
        out_specs=row_spec(D_MODEL),
        out_shape=jax.ShapeDtypeStruct((b, s, D_MODEL), F32),
        compiler_params=pltpu.CompilerParams(
            dimension_semantics=("arbitrary", "arbitrary"), vmem_limit_bytes=VMEM_LIMIT),
        name="outproj",
    )(a, r, f, w, x, post_g)


@functools.lru_cache(maxsize=None)
def _tables(seq):
    half = HEAD_DIM // 2
    lane = np.arange(LANES)
    inv = ROPE_THETA ** (-(np.arange(half, dtype=np.float64) / half))
    ang = np.arange(seq, dtype=np.float64)[:, None] * inv[None, :]
    cos = np.cos(ang)
    sin = np.sin(ang)
    cos_t = cos[:, lane % half]
    sin_t = sin[:, lane % half] * np.where((lane % HEAD_DIM) < half, -1.0, 1.0)[None, :]

    gam = 1.0 - np.exp2(-5.0 - np.arange(R_HEADS, dtype=np.float64))
    log_gam = np.log(gam)
    pos = np.arange(L_RET)
    dist = np.abs(pos[:, None] - pos[None, :])
    allowed = (pos[None, :] // CHUNK) <= (pos[:, None] // CHUNK)
    dmat = np.where(allowed[None], np.exp(log_gam[:, None, None] * dist[None]), 0.0)
    dmat = np.concatenate([dmat[0::2], dmat[1::2]], axis=2)
    head_of_lane = lane // HEAD_DIM
    qdec = np.stack([np.exp(log_gam[2 * p + head_of_lane][None, :] * (pos + 1.0)[:, None])
                     for p in range(HEAD_PAIRS)])
    kdec = np.stack([np.exp(log_gam[2 * p + head_of_lane][None, :] * (L_RET - 1.0 - pos)[:, None])
                     for p in range(HEAD_PAIRS)])
    bmask = (head_of_lane[:, None] == head_of_lane[None, :]).astype(np.float32)
    sdec = np.stack([np.exp(log_gam[2 * p + head_of_lane] * L_RET)[:, None] * bmask
                     for p in range(HEAD_PAIRS)])
    tri = (np.arange(RC_IN)[None, :] <= np.arange(RC_IN)[:, None]).astype(np.float32)
    grp = np.arange(A_WIDTH) // HEAD_DIM
    mavg = (grp[:, None] == grp[None, :]).astype(np.float32) / HEAD_DIM
    f = lambda a: np.asarray(a, np.float32)
    return dict(cos=f(cos_t), sin=f(sin_t), dmat=f(dmat), qdec=f(qdec), kdec=f(kdec),
                sdec=f(sdec), bmask=f(bmask), tri=f(tri), mavg=f(mavg),
                mavg128=f(mavg[:LANES, :LANES]))


def _layer(x, pre_g, post_g, w_in, b_f, a_ln_g, a_ws, a_bs, w_out, t):
    w_pad = jnp.pad(w_in, ((0, 0), (0, D_IN_PAD - D_IN))).astype(BF16)
    bf_pad = jnp.pad(b_f, (0, LANES - F_HEADS)).reshape(1, LANES)
    abias = jnp.repeat(a_bs.T, HEAD_DIM, axis=1)
    (a_out, rq, rk, rv, rg, fqt, kaug, vaug, fg, cf_t) = _inproj(
        x, pre_g.reshape(1, D_MODEL), w_pad, bf_pad, a_ln_g.reshape(1, A_WIDTH), a_ws, abias,
        t["cos"], t["sin"], t["tri"], t["mavg"])
    r_out = _retention(rq, rk, rv, rg, t["dmat"], t["qdec"], t["kdec"], t["sdec"], t["bmask"],
                       t["mavg128"])
    f_out = _fox(fqt, kaug, vaug, fg, cf_t)
    return _outproj(a_out, r_out, f_out, w_out.astype(BF16), x, post_g.reshape(1, D_MODEL))


def kernel(x, pre_gain, post_gain, w_in, b_forget, a_norm_gain, a_spatial_w, a_spatial_b, w_out):
    bf16_tables = ("tri", "mavg", "mavg128")
    t = {k: jnp.asarray(v, BF16 if k in bf16_tables else F32) for k, v in _tables(x.shape[1]).items()}
    for l in range(pre_gain.shape[0]):
        x = _layer(x, pre_gain[l], post_gain[l], w_in[l], b_forget[l], a_norm_gain[l],
                   a_spatial_w[l], a_spatial_b[l], w_out[l], t)
    return x
```

```python
import functools
import math

import jax
import jax.numpy as jnp
import numpy as np
from jax import lax
from jax.experimental import pallas as pl
from jax.experimental.pallas import tpu as pltpu

F32 = jnp.float32
BF16 = jnp.bfloat16

D_MODEL = 1024
HEAD_DIM = 64
CHUNK = 64
A_WIDTH = 256
R_WIDTH = 384
F_WIDTH = 384
A_GROUPS = 4
R_HEADS = 6
F_HEADS = 6
A_BLOCK = 128
ROPE_THETA = 10000.0
EPS = 1e-6
LANES = 128
HEAD_PAIRS = R_WIDTH // LANES

A_COLS = 3 * A_WIDTH
R_COLS = 4 * R_WIDTH
F_COLS = 4 * F_WIDTH + F_HEADS
D_IN = A_COLS + R_COLS + F_COLS
D_IN_PAD = A_COLS + R_COLS + 4 * F_WIDTH + LANES

TM_IN = 512
RC_IN = 256
TM_RET = 512
L_RET = 128
TQ = 256
TK = 256
TM_OUT = 512
RC_OUT = 256
VMEM_LIMIT = 48 * 1024 * 1024
VAUG_ROWS = HEAD_DIM + 16
LOG2E = 1.4426950408889634
MASK_VALUE = -1e30


def _silu(x):
    return x / (1.0 + jnp.exp(-x))


def _gelu_tanh(x):
    c = math.sqrt(2.0 / math.pi)
    return 0.5 * x * (1.0 + jnp.tanh(c * (x + 0.044715 * (x * x * x))))


def _log_sigmoid(x):
    return jnp.minimum(x, 0.0) - jnp.log1p(jnp.exp(-jnp.abs(x)))


def _dot(a, b):
    return jnp.dot(a, b, preferred_element_type=F32)


def _group_norm64(x, mavg):
    mean = _dot(x.astype(BF16), mavg)
    d = x - mean
    var = _dot((d * d).astype(BF16), mavg)
    return d * lax.rsqrt(var + EPS)


def _inproj_body(x_ref, pg_ref, w_ref, bf_ref, aln_ref, aws_ref, abias_ref, cos_ref, sin_ref,
                 tri_ref, mavg_ref,
                 a_ref, rq_ref, rk_ref, rv_ref, rg_ref, fq_ref, kaug_ref, vaug_ref, fg_ref, cft_ref,
                 carry_ref):
    tm = x_ref.shape[0]

    @pl.when(pl.program_id(1) == 0)
    def _():
        carry_ref[...] = jnp.zeros_like(carry_ref)

    mavg = mavg_ref[...]
    tri = tri_ref[...]
    row = lax.broadcasted_iota(jnp.int32, (A_BLOCK, A_BLOCK), 0)
    col = lax.broadcasted_iota(jnp.int32, (A_BLOCK, A_BLOCK), 1)
    allowed = jnp.logical_not(jnp.logical_and(row < CHUNK, col >= CHUNK))
    wcat = jnp.concatenate(
        [jnp.where(allowed, aws_ref[gi], 0.0).astype(BF16) for gi in range(A_GROUPS)], axis=1)
    lane_grp = lax.shift_right_logical(
        lax.broadcasted_iota(jnp.int32, (A_BLOCK, A_WIDTH), 1), HEAD_DIM.bit_length() - 1)
    zero_bf = jnp.zeros((A_BLOCK, A_WIDTH), BF16)
    lane = lax.broadcasted_iota(jnp.int32, (1, LANES), 1)
    first_half = jnp.bitwise_and(lane, HEAD_DIM - 1) < (HEAD_DIM // 2)
    srow = lax.broadcasted_iota(jnp.int32, (VAUG_ROWS - HEAD_DIM, 1), 0)
    ones_rows = jnp.broadcast_to(jnp.where(srow == 0, 1.0, 0.0),
                                 (VAUG_ROWS - HEAD_DIM, RC_IN)).astype(BF16)

    def rope(z, cos_t, sin_t):
        outs = []
        for c in range(HEAD_PAIRS):
            zc = z[:, c * LANES:(c + 1) * LANES]
            rot = jnp.where(first_half, pltpu.roll(zc, LANES - HEAD_DIM // 2, 1),
                            pltpu.roll(zc, HEAD_DIM // 2, 1))
            outs.append(zc * cos_t + rot * sin_t)
        return jnp.concatenate(outs, axis=1)

    for c in range(tm // RC_IN):
        rows = slice(c * RC_IN, (c + 1) * RC_IN)
        x = x_ref[rows, :]
        ms = jnp.mean(x * x, axis=-1, keepdims=True)
        h = (x * lax.rsqrt(ms + EPS) * pg_ref[...]).astype(BF16)
        za = _dot(h, w_ref[:, 0:A_COLS])
        zr = _dot(h, w_ref[:, A_COLS:A_COLS + R_COLS])
        zf = _dot(h, w_ref[:, A_COLS + R_COLS:D_IN_PAD])

        u = _gelu_tanh(za[:, 0:A_WIDTH])
        v = _gelu_tanh(za[:, A_WIDTH:2 * A_WIDTH])
        g = _silu(za[:, 2 * A_WIDTH:3 * A_WIDTH])
        vn = (_group_norm64(v, mavg) * aln_ref[...]).astype(BF16)
        mixed_blocks = []
        for nb in range(RC_IN // A_BLOCK):
            vb = vn[nb * A_BLOCK:(nb + 1) * A_BLOCK, :]
            vstack = jnp.concatenate(
                [jnp.where(lane_grp == gi, vb, zero_bf) for gi in range(A_GROUPS)], axis=0)
            mixed_blocks.append(_dot(wcat, vstack) + abias_ref[...])
        mixed = jnp.concatenate(mixed_blocks, axis=0)
        a_ref[rows, :] = (u * mixed * g).astype(BF16)

        cos_t = cos_ref[rows, :]
        sin_t = sin_ref[rows, :]
        rq_ref[rows, :] = (rope(zr[:, 0:R_WIDTH], cos_t, sin_t) * (HEAD_DIM ** -0.5)).astype(BF16)
        rk_ref[rows, :] = rope(zr[:, R_WIDTH:2 * R_WIDTH], cos_t, sin_t).astype(BF16)
        rv_ref[rows, :] = zr[:, 2 * R_WIDTH:3 * R_WIDTH].astype(BF16)
        rg_ref[rows, :] = _silu(zr[:, 3 * R_WIDTH:4 * R_WIDTH]).astype(BF16)

        fq_ref[:, rows] = (zf[:, 0:F_WIDTH] * (HEAD_DIM ** -0.5 * LOG2E)).T.astype(BF16)
        fg_ref[rows, :] = _silu(zf[:, 3 * F_WIDTH:4 * F_WIDTH]).astype(BF16)

        ls = _log_sigmoid(zf[:, 4 * F_WIDTH:4 * F_WIDTH + LANES] + bf_ref[...])
        ls_hi = ls.astype(BF16)
        ls_lo = (ls - ls_hi.astype(F32)).astype(BF16)
        cf = _dot(tri, ls_hi) + _dot(tri, ls_lo) + carry_ref[...]
        carry_ref[...] = cf[RC_IN - 1:RC_IN, :]
        cf2 = cf * LOG2E
        cft_ref[:, rows] = cf2.T[:8, :]

        fv_t = zf[:, 2 * F_WIDTH:3 * F_WIDTH].T.astype(BF16)
        for hd in range(F_HEADS):
            p, hh = divmod(hd, 2)
            negf = -jnp.broadcast_to(
                jnp.sum(jnp.where(lane == hd, cf2, 0.0), axis=-1, keepdims=True), (RC_IN, LANES))
            hi = negf.astype(BF16).astype(F32)
            mid = (negf - hi).astype(BF16).astype(F32)
            lo = (negf - hi - mid).astype(BF16).astype(F32)
            kk = zf[:, F_WIDTH + p * LANES:F_WIDTH + (p + 1) * LANES]
            if hh == 1:
                kk = pltpu.roll(kk, HEAD_DIM, 1)
            aug = jnp.where(lane == HEAD_DIM, hi,
                            jnp.where(lane == HEAD_DIM + 1, mid,
                                      jnp.where(lane == HEAD_DIM + 2, lo, 0.0)))
            kaug_ref[hd, rows, :] = jnp.where(lane < HEAD_DIM, kk, aug).astype(BF16)
            vaug_ref[hd, 0:HEAD_DIM, rows] = fv_t[hd * HEAD_DIM:(hd + 1) * HEAD_DIM, :]
            vaug_ref[hd, HEAD_DIM:VAUG_ROWS, rows] = ones_rows


def _inproj(x, pre_g, w_pad, bf_pad, aln, aws, abias, cos_t, sin_t, tri, mavg):
    b, s, _ = x.shape
    tm = TM_IN
    grid = (b, s // tm)
    row_spec = lambda n: pl.BlockSpec((None, tm, n), lambda bi, si: (bi, si, 0))
    col_spec = lambda n: pl.BlockSpec((None, n, tm), lambda bi, si: (bi, 0, si))
    const2 = lambda shp: pl.BlockSpec(shp, lambda bi, si: (0, 0))
    bf_sds = lambda n: jax.ShapeDtypeStruct((b, s, n), BF16)
    out_shape = ([bf_sds(A_WIDTH)] + [bf_sds(R_WIDTH)] * 4
                 + [jax.ShapeDtypeStruct((b, F_WIDTH, s), BF16),
                    jax.ShapeDtypeStruct((b, F_HEADS, s, LANES), BF16),
                    jax.ShapeDtypeStruct((b, F_HEADS, VAUG_ROWS, s), BF16),
                    bf_sds(F_WIDTH),
                    jax.ShapeDtypeStruct((b, 8, s), F32)])
    out_specs = ([row_spec(A_WIDTH)] + [row_spec(R_WIDTH)] * 4
                 + [col_spec(F_WIDTH),
                    pl.BlockSpec((None, F_HEADS, tm, LANES), lambda bi, si: (bi, 0, si, 0)),
                    pl.BlockSpec((None, F_HEADS, VAUG_ROWS, tm), lambda bi, si: (bi, 0, 0, si)),
                    row_spec(F_WIDTH),
                    col_spec(8)])
    in_specs = [
        row_spec(D_MODEL),
        const2((1, D_MODEL)),
        const2((D_MODEL, D_IN_PAD)),
        const2((1, LANES)),
        const2((1, A_WIDTH)),
        pl.BlockSpec((A_GROUPS, A_BLOCK, A_BLOCK), lambda bi, si: (0, 0, 0)),
        const2((A_BLOCK, A_WIDTH)),
        pl.BlockSpec((tm, LANES), lambda bi, si: (si, 0)),
        pl.BlockSpec((tm, LANES), lambda bi, si: (si, 0)),
        const2((RC_IN, RC_IN)),
        const2((A_WIDTH, A_WIDTH)),
    ]
    return pl.pallas_call(
        _inproj_body,
        grid=grid,
        in_specs=in_specs,
        out_specs=out_specs,
        out_shape=out_shape,
        scratch_shapes=[pltpu.VMEM((1, LANES), F32)],
        compiler_params=pltpu.CompilerParams(
            dimension_semantics=("arbitrary", "arbitrary"), vmem_limit_bytes=VMEM_LIMIT),
        name="inproj",
    )(x, pre_g, w_pad, bf_pad, aln, aws, abias, cos_t, sin_t, tri, mavg)


def _retention_body(q_ref, k_ref, v_ref, g_ref, dmat_ref, qdec_ref, kdec_ref, sdec_ref, bmask_ref,
                    mavg_ref, o_ref, state_ref):
    tm = q_ref.shape[0]

    @pl.when(pl.program_id(1) == 0)
    def _():
        state_ref[...] = jnp.zeros_like(state_ref)

    lane = lax.broadcasted_iota(jnp.int32, (1, LANES), 1)
    lower = lane < HEAD_DIM
    mavg = mavg_ref[...]
    bmask = bmask_ref[...]
    contract_last = (((1,), (1,)), ((), ()))
    contract_first = (((0,), (0,)), ((), ()))
    nblk = tm // L_RET
    units = [(p, j) for p in range(HEAD_PAIRS) for j in range(nblk)]

    def split_heads(x2):
        zero = jnp.zeros_like(x2)
        return jnp.concatenate([jnp.where(lower, x2, zero), jnp.where(lower, zero, x2)], axis=0)

    q2, v2, sc, kv = {}, {}, {}, {}
    for (p, j) in units:
        rs, cs = slice(j * L_RET, (j + 1) * L_RET), slice(p * LANES, (p + 1) * LANES)
        q2[p, j] = q_ref[rs, cs]
        k2 = k_ref[rs, cs]
        v2[p, j] = v_ref[rs, cs]
        sc[p, j] = lax.dot_general(q2[p, j], split_heads(k2), contract_last,
                                   preferred_element_type=F32)
        kd = (k2.astype(F32) * kdec_ref[p]).astype(BF16)
        kv[p, j] = lax.dot_general(kd, v2[p, j], contract_first, preferred_element_type=F32)

    st = {}
    for p in range(HEAD_PAIRS):
        state = state_ref[p]
        for j in range(nblk):
            st[p, j] = state.astype(BF16)
            state = state * sdec_ref[p] + kv[p, j] * bmask
        state_ref[p] = state

    out = {}
    for (p, j) in units:
        pm = (sc[p, j] * dmat_ref[p]).astype(BF16)
        out[p, j] = _dot(pm, split_heads(v2[p, j])) + _dot(q2[p, j], st[p, j]) * qdec_ref[p]

    tiles = [jnp.concatenate([out[p, j] for j in range(nblk)], axis=0) for p in range(HEAD_PAIRS)]
    means = [_dot(t.astype(BF16), mavg) for t in tiles]
    devs = [t - mu for t, mu in zip(tiles, means)]
    variances = [_dot((d * d).astype(BF16), mavg) for d in devs]
    for p in range(HEAD_PAIRS):
        cs = slice(p * LANES, (p + 1) * LANES)
        y = devs[p] * lax.rsqrt(variances[p] + EPS) * g_ref[:, cs].astype(F32)
        o_ref[:, cs] = y.astype(BF16)


def _retention(rq, rk, rv, rg, dmat, qdec, kdec, sdec, bmask, mavg128):
    b, s, _ = rq.shape
    tm = TM_RET
    row_spec = pl.BlockSpec((None, tm, R_WIDTH), lambda bi, si: (bi, si, 0))
    c3 = lambda shp: pl.BlockSpec(shp, lambda bi, si: (0, 0, 0))
    c2 = lambda shp: pl.BlockSpec(shp, lambda bi, si: (0, 0))
    return pl.pallas_call(
        _retention_body,
        grid=(b, s // tm),
        in_specs=[row_spec, row_spec, row_spec, row_spec,
                  c3((HEAD_PAIRS, L_RET, 2 * L_RET)), c3((HEAD_PAIRS, L_RET, LANES)),
                  c3((HEAD_PAIRS, L_RET, LANES)), c3((HEAD_PAIRS, LANES, LANES)),
                  c2((LANES, LANES)), c2((LANES, LANES))],
        out_specs=row_spec,
        out_shape=jax.ShapeDtypeStruct((b, s, R_WIDTH), BF16),
        scratch_shapes=[pltpu.VMEM((HEAD_PAIRS, LANES, LANES), F32)],
        compiler_params=pltpu.CompilerParams(
            dimension_semantics=("arbitrary", "arbitrary"), vmem_limit_bytes=VMEM_LIMIT),
        name="retention",
    )(rq, rk, rv, rg, dmat, qdec, kdec, sdec, bmask, mavg128)


def _fox_body(qt_ref, kaug_ref, vaug_ref, cft_ref, g_ref, o_ref, m_ref, acc_ref, sa_ref, sb_ref):
    qi = pl.program_id(1)
    srow_q = lax.broadcasted_iota(jnp.int32, (HEAD_DIM, 1), 0)
    q_ones = jnp.broadcast_to(jnp.where(srow_q < 3, 1.0, 0.0), (HEAD_DIM, TQ)).astype(BF16)
    qta = [jnp.concatenate([qt_ref[h * HEAD_DIM:(h + 1) * HEAD_DIM, :], q_ones], axis=0)
           for h in range(F_HEADS)]
    fq = [cft_ref[h:h + 1, :] for h in range(F_HEADS)]
    krow = lax.broadcasted_iota(jnp.int32, (TK, TQ), 0)
    qcol = lax.broadcasted_iota(jnp.int32, (TK, TQ), 1)
    causal = krow <= qcol

    for h in range(F_HEADS):
        m_ref[h] = jnp.full((8, TQ), MASK_VALUE, F32)
        acc_ref[h] = jnp.zeros((VAUG_ROWS, TQ), F32)

    def scores(kj, dst_ref, h):
        k0 = pl.multiple_of(kj * TK, TK)
        dst_ref[h] = _dot(kaug_ref[h, pl.ds(k0, TK), :], qta[h])

    def consume(kj, src_ref, h, diagonal):
        k0 = pl.multiple_of(kj * TK, TK)
        st = src_ref[h]
        if diagonal:
            st = jnp.where(causal, st, MASK_VALUE)
        m_old = m_ref[h][0:1, :]
        m_new = jnp.maximum(m_old, jnp.max(st, axis=0, keepdims=True) + fq[h])
        p = jnp.exp2(st - (m_new - fq[h]))
        alpha = jnp.exp2(m_old - m_new)
        pv = _dot(vaug_ref[h, :, pl.ds(k0, TK)], p.astype(BF16))
        acc_ref[h] = alpha * acc_ref[h] + pv
        m_ref[h] = jnp.broadcast_to(m_new, (8, TQ))

    def overlapped(next_j, dst_ref, cur_j, src_ref):
        lead = 2
        for h in range(lead):
            scores(next_j, dst_ref, h)
        for h in range(F_HEADS):
            consume(cur_j, src_ref, h, False)
            if h + lead < F_HEADS:
                scores(next_j, dst_ref, h + lead)

    for h in range(F_HEADS):
        scores(0, sa_ref, h)

    def body(i, carry):
        j0 = 2 * i
        overlapped(j0 + 1, sb_ref, j0, sa_ref)
        overlapped(j0 + 2, sa_ref, j0 + 1, sb_ref)
        return carry

    lax.fori_loop(0, lax.shift_right_logical(qi, 1), body, 0)
    qi_odd = jnp.bitwise_and(qi, 1)

    @pl.when(qi_odd == 0)
    def _():
        for h in range(F_HEADS):
            consume(qi, sa_ref, h, True)

    @pl.when(qi_odd == 1)
    def _():
        overlapped(qi, sb_ref, qi - 1, sa_ref)
        for h in range(F_HEADS):
            consume(qi, sb_ref, h, True)

    for p in range(HEAD_PAIRS):
        halves = []
        for hh in range(2):
            acc = acc_ref[2 * p + hh]
            halves.append(acc[0:HEAD_DIM, :] / acc[HEAD_DIM:HEAD_DIM + 1, :])
        out_t = jnp.concatenate(halves, axis=0)
        cs = slice(p * LANES, (p + 1) * LANES)
        o_ref[:, cs] = (out_t.T * g_ref[:, cs].astype(F32)).astype(BF16)


def _fox(fqt, kaug, vaug, fg, cf_t):
    b, _, s = fqt.shape
    assert TQ == TK
    row_spec = pl.BlockSpec((None, TQ, F_WIDTH), lambda bi, qi: (bi, qi, 0))
    return pl.pallas_call(
        _fox_body,
        grid=(b, s // TQ),
        in_specs=[pl.BlockSpec((None, F_WIDTH, TQ), lambda bi, qi: (bi, 0, qi)),
                  pl.BlockSpec((None, F_HEADS, s, LANES), lambda bi, qi: (bi, 0, 0, 0)),
                  pl.BlockSpec((None, F_HEADS, VAUG_ROWS, s), lambda bi, qi: (bi, 0, 0, 0)),
                  pl.BlockSpec((None, 8, TQ), lambda bi, qi: (bi, 0, qi)),
                  row_spec],
        out_specs=row_spec,
        out_shape=jax.ShapeDtypeStruct((b, s, F_WIDTH), BF16),
        scratch_shapes=[pltpu.VMEM((F_HEADS, 8, TQ), F32),
                        pltpu.VMEM((F_HEADS, VAUG_ROWS, TQ), F32),
                        pltpu.VMEM((F_HEADS, TK, TQ), F32),
                        pltpu.VMEM((F_HEADS, TK, TQ), F32)],
        compiler_params=pltpu.CompilerParams(
            dimension_semantics=("arbitrary", "arbitrary"), vmem_limit_bytes=VMEM_LIMIT),
        name="fox",
    )(fqt, kaug, vaug, cf_t, fg)


def _outproj_body(a_ref, r_ref, f_ref, w_ref, x_ref, pg_ref, o_ref):
    tm = x_ref.shape[0]
    for c in range(tm // RC_OUT):
        rows = slice(c * RC_OUT, (c + 1) * RC_OUT)
        y = jnp.concatenate([a_ref[rows, :], r_ref[rows, :], f_ref[rows, :]], axis=1)
        o = _dot(y, w_ref[...])
        ms = jnp.mean(o * o, axis=-1, keepdims=True)
        o_ref[rows, :] = x_ref[rows, :] + o * lax.rsqrt(ms + EPS) * pg_ref[...]


def _outproj(a, r, f, w, x, post_g):
    b, s, _ = x.shape
    tm = TM_OUT
    row_spec = lambda n: pl.BlockSpec((None, tm, n), lambda bi, si: (bi, si, 0))
    c2 = lambda shp: pl.BlockSpec(shp, lambda bi, si: (0, 0))
    return pl.pallas_call(
        _outproj_body,
        grid=(b, s // tm),
        in_specs=[row_spec(A_WIDTH), row_spec(R_WIDTH), row_spec(F_WIDTH),
                  c2((D_MODEL, D_MODEL)), row_spec(D_MODEL), c2((1, D_MODEL))],
        out_specs=row_spec(D_MODEL),
        out_shape=jax.ShapeDtypeStruct((b, s, D_MODEL), F32),
        compiler_params=pltpu.CompilerParams(
            dimension_semantics=("arbitrary", "arbitrary"), vmem_limit_bytes=VMEM_LIMIT),
        name="outproj",
    )(a, r, f, w, x, post_g)


@functools.lru_cache(maxsize=None)
def _tables(seq):
    half = HEAD_DIM // 2
    lane = np.arange(LANES)
    inv = ROPE_THETA ** (-(np.arange(half, dtype=np.float64) / half))
    ang = np.arange(seq, dtype=np.float64)[:, None] * inv[None, :]
    cos = np.cos(ang)
    sin = np.sin(ang)
    cos_t = cos[:, lane % half]
    sin_t = sin[:, lane % half] * np.where((lane % HEAD_DIM) < half, -1.0, 1.0)[None, :]

    gam = 1.0 - np.exp2(-5.0 - np.arange(R_HEADS, dtype=np.float64))
    log_gam = np.log(gam)
    pos = np.arange(L_RET)
    dist = np.abs(pos[:, None] - pos[None, :])
    allowed = (pos[None, :] // CHUNK) <= (pos[:, None] // CHUNK)
    dmat = np.where(allowed[None], np.exp(log_gam[:, None, None] * dist[None]), 0.0)
    dmat = np.concatenate([dmat[0::2], dmat[1::2]], axis=2)
    head_of_lane = lane // HEAD_DIM
    qdec = np.stack([np.exp(log_gam[2 * p + head_of_lane][None, :] * (pos + 1.0)[:, None])
                     for p in range(HEAD_PAIRS)])
    kdec = np.stack([np.exp(log_gam[2 * p + head_of_lane][None, :] * (L_RET - 1.0 - pos)[:, None])
                     for p in range(HEAD_PAIRS)])
    bmask = (head_of_lane[:, None] == head_of_lane[None, :]).astype(np.float32)
    sdec = np.stack([np.exp(log_gam[2 * p + head_of_lane] * L_RET)[:, None] * bmask
                     for p in range(HEAD_PAIRS)])
    tri = (np.arange(RC_IN)[None, :] <= np.arange(RC_IN)[:, None]).astype(np.float32)
    grp = np.arange(A_WIDTH) // HEAD_DIM
    mavg = (grp[:, None] == grp[None, :]).astype(np.float32) / HEAD_DIM
    f = lambda a: np.asarray(a, np.float32)
    return dict(cos=f(cos_t), sin=f(sin_t), dmat=f(dmat), qdec=f(qdec), kdec=f(kdec),
                sdec=f(sdec), bmask=f(bmask), tri=f(tri), mavg=f(mavg),
                mavg128=f(mavg[:LANES, :LANES]))


def _layer(x, pre_g, post_g, w_in, b_f, a_ln_g, a_ws, a_bs, w_out, t):
    w_pad = jnp.pad(w_in, ((0, 0), (0, D_IN_PAD - D_IN))).astype(BF16)
    bf_pad = jnp.pad(b_f, (0, LANES - F_HEADS)).reshape(1, LANES)
    abias = jnp.repeat(a_bs.T, HEAD_DIM, axis=1)
    (a_out, rq, rk, rv, rg, fqt, kaug, vaug, fg, cf_t) = _inproj(
        x, pre_g.reshape(1, D_MODEL), w_pad, bf_pad, a_ln_g.reshape(1, A_WIDTH), a_ws, abias,
        t["cos"], t["sin"], t["tri"], t["mavg"])
    r_out = _retention(rq, rk, rv, rg, t["dmat"], t["qdec"], t["kdec"], t["sdec"], t["bmask"],
                       t["mavg128"])
    f_out = _fox(fqt, kaug, vaug, fg, cf_t)
    return _outproj(a_out, r_out, f_out, w_out.astype(BF16), x, post_g.reshape(1, D_MODEL))


def kernel(x, pre_gain, post_gain, w_in, b_forget, a_norm_gain, a_spatial_w, a_spatial_b, w_out):
    bf16_tables = ("tri", "mavg", "mavg128")
    t = {k: jnp.asarray(v, BF16 if k in bf16_tables else F32) for k, v in _tables(x.shape[1]).items()}
    for l in range(pre_gain.shape[0]):
        x = _layer(x, pre_gain[l], post_gain[l], w_in[l], b_forget[l], a_norm_gain[l],
                   a_spatial_w[l], a_spatial_b[l], w_out[l], t)
    return x
```

```python
import functools
import math

import jax
import jax.numpy as jnp
import numpy as np
from jax import lax
from jax.experimental import pallas as pl
from jax.experimental.pallas import tpu as pltpu

F32 = jnp.float32
BF16 = jnp.bfloat16

D_MODEL = 1024
HEAD_DIM = 64
CHUNK = 64
A_WIDTH = 256
R_WIDTH = 384
F_WIDTH = 384
A_GROUPS = 4
R_HEADS = 6
F_HEADS = 6
A_BLOCK = 128
ROPE_THETA = 10000.0
EPS = 1e-6
LANES = 128
HEAD_PAIRS = R_WIDTH // LANES

A_COLS = 3 * A_WIDTH
R_COLS = 4 * R_WIDTH
F_COLS = 4 * F_WIDTH + F_HEADS
D_IN = A_COLS + R_COLS + F_COLS
D_IN_PAD = A_COLS + R_COLS + 4 * F_WIDTH + LANES

TM_IN = 1024
RC_IN = 256
TM_RET = 1024
L_RET = 128
TQ = 256
TK = 256
TM_OUT = 1024
RC_OUT = 256
VMEM_LIMIT = 56 * 1024 * 1024
VAUG_ROWS = HEAD_DIM + 16
LOG2E = 1.4426950408889634
MASK_VALUE = -1e30


def _silu(x):
    return x / (1.0 + jnp.exp(-x))


def _gelu_tanh(x):
    c = math.sqrt(2.0 / math.pi)
    return 0.5 * x * (1.0 + jnp.tanh(c * (x + 0.044715 * (x * x * x))))


def _log_sigmoid(x):
    return jnp.minimum(x, 0.0) - jnp.log1p(jnp.exp(-jnp.abs(x)))


def _dot(a, b):
    return jnp.dot(a, b, preferred_element_type=F32)


def _group_norm64(x, mavg):
    mean = _dot(x.astype(BF16), mavg)
    d = x - mean
    var = _dot((d * d).astype(BF16), mavg)
    return d * lax.rsqrt(var + EPS)


def _inproj_body(x_ref, pg_ref, w_ref, bf_ref, aln_ref, aws_ref, abias_ref, cos_ref, sin_ref,
                 tri_ref, mavg_ref, place_ref,
                 a_ref, rq_ref, rk_ref, rv_ref, rg_ref, fq_ref, kaug_ref, vaug_ref, fg_ref, cft_ref,
                 carry_ref):
    tm = x_ref.shape[0]

    @pl.when(pl.program_id(1) == 0)
    def _():
        carry_ref[...] = jnp.zeros_like(carry_ref)

    mavg = mavg_ref[...]
    tri = tri_ref[...]
    row = lax.broadcasted_iota(jnp.int32, (A_BLOCK, A_BLOCK), 0)
    col = lax.broadcasted_iota(jnp.int32, (A_BLOCK, A_BLOCK), 1)
    allowed = jnp.logical_not(jnp.logical_and(row < CHUNK, col >= CHUNK))
    wcat = jnp.concatenate(
        [jnp.where(allowed, aws_ref[gi], 0.0).astype(BF16) for gi in range(A_GROUPS)], axis=1)
    lane_grp = lax.shift_right_logical(
        lax.broadcasted_iota(jnp.int32, (A_BLOCK, A_WIDTH), 1), HEAD_DIM.bit_length() - 1)
    zero_bf = jnp.zeros((A_BLOCK, A_WIDTH), BF16)
    lane = lax.broadcasted_iota(jnp.int32, (1, LANES), 1)
    first_half = jnp.bitwise_and(lane, HEAD_DIM - 1) < (HEAD_DIM // 2)
    srow = lax.broadcasted_iota(jnp.int32, (VAUG_ROWS - HEAD_DIM, 1), 0)
    ones_rows = jnp.broadcast_to(jnp.where(srow == 0, 1.0, 0.0),
                                 (VAUG_ROWS - HEAD_DIM, RC_IN)).astype(BF16)

    def rope(z, cos_t, sin_t):
        outs = []
        for c in range(HEAD_PAIRS):
            zc = z[:, c * LANES:(c + 1) * LANES]
            rot = jnp.where(first_half, pltpu.roll(zc, LANES - HEAD_DIM // 2, 1),
                            pltpu.roll(zc, HEAD_DIM // 2, 1))
            outs.append(zc * cos_t + rot * sin_t)
        return jnp.concatenate(outs, axis=1)

    def normed(c):
        x = x_ref[c * RC_IN:(c + 1) * RC_IN, :]
        ms = jnp.mean(x * x, axis=-1, keepdims=True)
        return (x * lax.rsqrt(ms + EPS) * pg_ref[...]).astype(BF16)

    wide_cols = ((0, A_COLS), (A_COLS, A_COLS + R_COLS), (A_COLS + R_COLS, D_IN_PAD))

    def wide(h, k):
        return _dot(h, w_ref[:, wide_cols[k][0]:wide_cols[k][1]])

    def stage1(c, z, e):
        za, zr, zf = z
        e["u"] = _gelu_tanh(za[:, 0:A_WIDTH])
        e["v"] = _gelu_tanh(za[:, A_WIDTH:2 * A_WIDTH])
        e["g"] = _silu(za[:, 2 * A_WIDTH:3 * A_WIDTH])
        e["mean"] = _dot(e["v"].astype(BF16), mavg)
        ls = _log_sigmoid(zf[:, 4 * F_WIDTH:4 * F_WIDTH + LANES] + bf_ref[...])
        ls_hi = ls.astype(BF16)
        ls_lo = (ls - ls_hi.astype(F32)).astype(BF16)
        cf = _dot(tri, ls_hi) + _dot(tri, ls_lo) + carry_ref[...]
        carry_ref[...] = cf[RC_IN - 1:RC_IN, :]
        e["cf2"] = cf * LOG2E

    def stage2(c, z, e):
        rows = slice(c * RC_IN, (c + 1) * RC_IN)
        za, zr, zf = z
        e["d"] = e["v"] - e["mean"]
        e["var"] = _dot((e["d"] * e["d"]).astype(BF16), mavg)
        cf2 = e["cf2"]
        cft_ref[:, rows] = cf2.T[:8, :]
        negf = -cf2
        hi = negf.astype(BF16).astype(F32)
        mid = (negf - hi).astype(BF16).astype(F32)
        lo = (negf - hi - mid).astype(BF16).astype(F32)
        packed = jnp.where(lane < 8, hi,
                           jnp.where(lane < 16, pltpu.roll(mid, 8, 1),
                                     jnp.where(lane < 24, pltpu.roll(lo, 16, 1), 0.0)))
        e["placed"] = _dot(packed.astype(BF16), place_ref[...])
        cos_t = cos_ref[rows, :]
        sin_t = sin_ref[rows, :]
        rq_ref[rows, :] = (rope(zr[:, 0:R_WIDTH], cos_t, sin_t) * (HEAD_DIM ** -0.5)).astype(BF16)
        rk_ref[rows, :] = rope(zr[:, R_WIDTH:2 * R_WIDTH], cos_t, sin_t).astype(BF16)
        rv_ref[rows, :] = zr[:, 2 * R_WIDTH:3 * R_WIDTH].astype(BF16)
        rg_ref[rows, :] = _silu(zr[:, 3 * R_WIDTH:4 * R_WIDTH]).astype(BF16)

    def stage3(c, z, e):
        rows = slice(c * RC_IN, (c + 1) * RC_IN)
        za, zr, zf = z
        vn = (e["d"] * lax.rsqrt(e["var"] + EPS) * aln_ref[...]).astype(BF16)
        mixed_blocks = []
        for nb in range(RC_IN // A_BLOCK):
            vb = vn[nb * A_BLOCK:(nb + 1) * A_BLOCK, :]
            vstack = jnp.concatenate(
                [jnp.where(lane_grp == gi, vb, zero_bf) for gi in range(A_GROUPS)], axis=0)
            mixed_blocks.append(_dot(wcat, vstack) + abias_ref[...])
        mixed = jnp.concatenate(mixed_blocks, axis=0)
        a_ref[rows, :] = (e["u"] * mixed * e["g"]).astype(BF16)
        fq_ref[:, rows] = (zf[:, 0:F_WIDTH] * (HEAD_DIM ** -0.5 * LOG2E)).T.astype(BF16)
        fg_ref[rows, :] = _silu(zf[:, 3 * F_WIDTH:4 * F_WIDTH]).astype(BF16)
        fv_t = zf[:, 2 * F_WIDTH:3 * F_WIDTH].T.astype(BF16)
        for hd in range(F_HEADS):
            p, hh = divmod(hd, 2)
            kk = zf[:, F_WIDTH + p * LANES:F_WIDTH + (p + 1) * LANES]
            if hh == 1:
                kk = pltpu.roll(kk, HEAD_DIM, 1)
            aug = e["placed"][:, hd * LANES:(hd + 1) * LANES]
            kaug_ref[hd, rows, :] = jnp.where(lane < HEAD_DIM, kk, aug).astype(BF16)
            vaug_ref[hd, 0:HEAD_DIM, rows] = fv_t[hd * HEAD_DIM:(hd + 1) * HEAD_DIM, :]
            vaug_ref[hd, HEAD_DIM:VAUG_ROWS, rows] = ones_rows

    stages = (stage1, stage2, stage3)
    nchunks = tm // RC_IN
    h0 = normed(0)
    z_next = tuple(wide(h0, k) for k in range(3))
    for c in range(nchunks):
        z, e = z_next, {}
        if c + 1 < nchunks:
            h_next = normed(c + 1)
            z_new = []
            for k in range(3):
                z_new.append(wide(h_next, k))
                stages[k](c, z, e)
            z_next = tuple(z_new)
        else:
            for k in range(3):
                stages[k](c, z, e)


def _inproj(x, pre_g, w_pad, bf_pad, aln, aws, abias, cos_t, sin_t, tri, mavg, place):
    b, s, _ = x.shape
    tm = TM_IN
    grid = (b, s // tm)
    row_spec = lambda n: pl.BlockSpec((None, tm, n), lambda bi, si: (bi, si, 0))
    col_spec = lambda n: pl.BlockSpec((None, n, tm), lambda bi, si: (bi, 0, si))
    const2 = lambda shp: pl.BlockSpec(shp, lambda bi, si: (0, 0))
    bf_sds = lambda n: jax.ShapeDtypeStruct((b, s, n), BF16)
    out_shape = ([bf_sds(A_WIDTH)] + [bf_sds(R_WIDTH)] * 4
                 + [jax.ShapeDtypeStruct((b, F_WIDTH, s), BF16),
                    jax.ShapeDtypeStruct((b, F_HEADS, s, LANES), BF16),
                    jax.ShapeDtypeStruct((b, F_HEADS, VAUG_ROWS, s), BF16),
                    bf_sds(F_WIDTH),
                    jax.ShapeDtypeStruct((b, 8, s), F32)])
    out_specs = ([row_spec(A_WIDTH)] + [row_spec(R_WIDTH)] * 4
                 + [col_spec(F_WIDTH),
                    pl.BlockSpec((None, F_HEADS, tm, LANES), lambda bi, si: (bi, 0, si, 0)),
                    pl.BlockSpec((None, F_HEADS, VAUG_ROWS, tm), lambda bi, si: (bi, 0, 0, si)),
                    row_spec(F_WIDTH),
                    col_spec(8)])
    in_specs = [
        row_spec(D_MODEL),
        const2((1, D_MODEL)),
        pl.BlockSpec((D_MODEL, D_IN_PAD), lambda bi, si: (0, 0), pipeline_mode=pl.Buffered(1)),
        const2((1, LANES)),
        const2((1, A_WIDTH)),
        pl.BlockSpec((A_GROUPS, A_BLOCK, A_BLOCK), lambda bi, si: (0, 0, 0)),
        const2((A_BLOCK, A_WIDTH)),
        pl.BlockSpec((tm, LANES), lambda bi, si: (si, 0)),
        pl.BlockSpec((tm, LANES), lambda bi, si: (si, 0)),
        const2((RC_IN, RC_IN)),
        const2((A_WIDTH, A_WIDTH)),
        const2((LANES, F_HEADS * LANES)),
    ]
    return pl.pallas_call(
        _inproj_body,
        grid=grid,
        in_specs=in_specs,
        out_specs=out_specs,
        out_shape=out_shape,
        scratch_shapes=[pltpu.VMEM((1, LANES), F32)],
        compiler_params=pltpu.CompilerParams(
            dimension_semantics=("arbitrary", "arbitrary"), vmem_limit_bytes=VMEM_LIMIT),
        name="inproj",
    )(x, pre_g, w_pad, bf_pad, aln, aws, abias, cos_t, sin_t, tri, mavg, place)


def _retention_body(q_ref, k_ref, v_ref, g_ref, dmat_ref, qdec_ref, kdec_ref, sdec_ref, bmask_ref,
                    mavg_ref, o_ref, state_ref):
    tm = q_ref.shape[0]

    @pl.when(pl.program_id(1) == 0)
    def _():
        state_ref[...] = jnp.zeros_like(state_ref)

    lane = lax.broadcasted_iota(jnp.int32, (1, LANES), 1)
    lower = lane < HEAD_DIM
    mavg = mavg_ref[...]
    bmask = bmask_ref[...]
    contract_last = (((1,), (1,)), ((), ()))
    contract_first = (((0,), (0,)), ((), ()))
    nblk = tm // L_RET
    units = [(p, j) for p in range(HEAD_PAIRS) for j in range(nblk)]

    def split_heads(x2):
        zero = jnp.zeros_like(x2)
        return jnp.concatenate([jnp.where(lower, x2, zero), jnp.where(lower, zero, x2)], axis=0)

    q2, v2, sc, kv = {}, {}, {}, {}
    for (p, j) in units:
        rs, cs = slice(j * L_RET, (j + 1) * L_RET), slice(p * LANES, (p + 1) * LANES)
        q2[p, j] = q_ref[rs, cs]
        k2 = k_ref[rs, cs]
        v2[p, j] = v_ref[rs, cs]
        sc[p, j] = lax.dot_general(q2[p, j], split_heads(k2), contract_last,
                                   preferred_element_type=F32)
        kd = (k2.astype(F32) * kdec_ref[p]).astype(BF16)
        kv[p, j] = lax.dot_general(kd, v2[p, j], contract_first, preferred_element_type=F32)

    st = {}
    for p in range(HEAD_PAIRS):
        state = state_ref[p]
        for j in range(nblk):
            st[p, j] = state.astype(BF16)
            state = state * sdec_ref[p] + kv[p, j] * bmask
        state_ref[p] = state

    out = {}
    for (p, j) in units:
        pm = (sc[p, j] * dmat_ref[p]).astype(BF16)
        out[p, j] = _dot(pm, split_heads(v2[p, j])) + _dot(q2[p, j], st[p, j]) * qdec_ref[p]

    tiles = [jnp.concatenate([out[p, j] for j in range(nblk)], axis=0) for p in range(HEAD_PAIRS)]
    means = [_dot(t.astype(BF16), mavg) for t in tiles]
    devs = [t - mu for t, mu in zip(tiles, means)]
    variances = [_dot((d * d).astype(BF16), mavg) for d in devs]
    for p in range(HEAD_PAIRS):
        cs = slice(p * LANES, (p + 1) * LANES)
        y = devs[p] * lax.rsqrt(variances[p] + EPS) * g_ref[:, cs].astype(F32)
        o_ref[:, cs] = y.astype(BF16)


def _retention(rq, rk, rv, rg, dmat, qdec, kdec, sdec, bmask, mavg128):
    b, s, _ = rq.shape
    tm = TM_RET
    row_spec = pl.BlockSpec((None, tm, R_WIDTH), lambda bi, si: (bi, si, 0))
    c3 = lambda shp: pl.BlockSpec(shp, lambda bi, si: (0, 0, 0))
    c2 = lambda shp: pl.BlockSpec(shp, lambda bi, si: (0, 0))
    return pl.pallas_call(
        _retention_body,
        grid=(b, s // tm),
        in_specs=[row_spec, row_spec, row_spec, row_spec,
                  c3((HEAD_PAIRS, L_RET, 2 * L_RET)), c3((HEAD_PAIRS, L_RET, LANES)),
                  c3((HEAD_PAIRS, L_RET, LANES)), c3((HEAD_PAIRS, LANES, LANES)),
                  c2((LANES, LANES)), c2((LANES, LANES))],
        out_specs=row_spec,
        out_shape=jax.ShapeDtypeStruct((b, s, R_WIDTH), BF16),
        scratch_shapes=[pltpu.VMEM((HEAD_PAIRS, LANES, LANES), F32)],
        compiler_params=pltpu.CompilerParams(
            dimension_semantics=("arbitrary", "arbitrary"), vmem_limit_bytes=VMEM_LIMIT),
        name="retention",
    )(rq, rk, rv, rg, dmat, qdec, kdec, sdec, bmask, mavg128)


def _fox_body(qt_ref, kaug_ref, vaug_ref, cft_ref, g_ref, o_ref, m_ref, acc_ref, sa_ref, sb_ref):
    qi = pl.program_id(1)
    srow_q = lax.broadcasted_iota(jnp.int32, (HEAD_DIM, 1), 0)
    q_ones = jnp.broadcast_to(jnp.where(srow_q < 3, 1.0, 0.0), (HEAD_DIM, TQ)).astype(BF16)
    qta = [jnp.concatenate([qt_ref[h * HEAD_DIM:(h + 1) * HEAD_DIM, :], q_ones], axis=0)
           for h in range(F_HEADS)]
    fq = [cft_ref[h:h + 1, :] for h in range(F_HEADS)]
    krow = lax.broadcasted_iota(jnp.int32, (TK, TQ), 0)
    qcol = lax.broadcasted_iota(jnp.int32, (TK, TQ), 1)
    causal = krow <= qcol

    for h in range(F_HEADS):
        m_ref[h] = jnp.full((8, TQ), MASK_VALUE, F32)
        acc_ref[h] = jnp.zeros((VAUG_ROWS, TQ), F32)

    def scores(kj, dst_ref, h):
        k0 = pl.multiple_of(kj * TK, TK)
        dst_ref[h] = _dot(kaug_ref[h, pl.ds(k0, TK), :], qta[h])

    def consume(kj, src_ref, h, diagonal):
        k0 = pl.multiple_of(kj * TK, TK)
        st = src_ref[h]
        if diagonal:
            st = jnp.where(causal, st, MASK_VALUE)
        m_old = m_ref[h][0:1, :]
        m_new = jnp.maximum(m_old, jnp.max(st, axis=0, keepdims=True) + fq[h])
        p = jnp.exp2(st - (m_new - fq[h]))
        alpha = jnp.exp2(m_old - m_new)
        pv = _dot(vaug_ref[h, :, pl.ds(k0, TK)], p.astype(BF16))
        acc_ref[h] = alpha * acc_ref[h] + pv
        m_ref[h] = jnp.broadcast_to(m_new, (8, TQ))

    def overlapped(next_j, dst_ref, cur_j, src_ref):
        lead = 2
        for h in range(lead):
            scores(next_j, dst_ref, h)
        for h in range(F_HEADS):
            consume(cur_j, src_ref, h, False)
            if h + lead < F_HEADS:
                scores(next_j, dst_ref, h + lead)

    for h in range(F_HEADS):
        scores(0, sa_ref, h)

    def body(i, carry):
        j0 = 2 * i
        overlapped(j0 + 1, sb_ref, j0, sa_ref)
        overlapped(j0 + 2, sa_ref, j0 + 1, sb_ref)
        return carry

    lax.fori_loop(0, lax.shift_right_logical(qi, 1), body, 0)
    qi_odd = jnp.bitwise_and(qi, 1)

    @pl.when(qi_odd == 0)
    def _():
        for h in range(F_HEADS):
            consume(qi, sa_ref, h, True)

    @pl.when(qi_odd == 1)
    def _():
        overlapped(qi, sb_ref, qi - 1, sa_ref)
        for h in range(F_HEADS):
            consume(qi, sb_ref, h, True)

    for p in range(HEAD_PAIRS):
        halves = []
        for hh in range(2):
            acc = acc_ref[2 * p + hh]
            halves.append(acc[0:HEAD_DIM, :] / acc[HEAD_DIM:HEAD_DIM + 1, :])
        out_t = jnp.concatenate(halves, axis=0)
        cs = slice(p * LANES, (p + 1) * LANES)
        o_ref[:, cs] = (out_t.T * g_ref[:, cs].astype(F32)).astype(BF16)


def _fox(fqt, kaug, vaug, fg, cf_t):
    b, _, s = fqt.shape
    assert TQ == TK
    row_spec = pl.BlockSpec((None, TQ, F_WIDTH), lambda bi, qi: (bi, qi, 0))
    return pl.pallas_call(
        _fox_body,
        grid=(b, s // TQ),
        in_specs=[pl.BlockSpec((None, F_WIDTH, TQ), lambda bi, qi: (bi, 0, qi)),
                  pl.BlockSpec((None, F_HEADS, s, LANES), lambda bi, qi: (bi, 0, 0, 0)),
                  pl.BlockSpec((None, F_HEADS, VAUG_ROWS, s), lambda bi, qi: (bi, 0, 0, 0)),
                  pl.BlockSpec((None, 8, TQ), lambda bi, qi: (bi, 0, qi)),
                  row_spec],
        out_specs=row_spec,
        out_shape=jax.ShapeDtypeStruct((b, s, F_WIDTH), BF16),
        scratch_shapes=[pltpu.VMEM((F_HEADS, 8, TQ), F32),
                        pltpu.VMEM((F_HEADS, VAUG_ROWS, TQ), F32),
                        pltpu.VMEM((F_HEADS, TK, TQ), F32),
                        pltpu.VMEM((F_HEADS, TK, TQ), F32)],
        compiler_params=pltpu.CompilerParams(
            dimension_semantics=("arbitrary", "arbitrary"), vmem_limit_bytes=VMEM_LIMIT),
        name="fox",
    )(fqt, kaug, vaug, cf_t, fg)


def _outproj_body(a_ref, r_ref, f_ref, w_ref, x_ref, pg_ref, o_ref):
    tm = x_ref.shape[0]

    def project(c):
        rows = slice(c * RC_OUT, (c + 1) * RC_OUT)
        y = jnp.concatenate([a_ref[rows, :], r_ref[rows, :], f_ref[rows, :]], axis=1)
        return _dot(y, w_ref[...])

    nchunks = tm // RC_OUT
    o_next = project(0)
    for c in range(nchunks):
        rows = slice(c * RC_OUT, (c + 1) * RC_OUT)
        o = o_next
        if c + 1 < nchunks:
            o_next = project(c + 1)
        ms = jnp.mean(o * o, axis=-1, keepdims=True)
        o_ref[rows, :] = x_ref[rows, :] + o * lax.rsqrt(ms + EPS) * pg_ref[...]


def _outproj(a, r, f, w, x, post_g):
    b, s, _ = x.shape
    tm = TM_OUT
    row_spec = lambda n: pl.BlockSpec((None, tm, n), lambda bi, si: (bi, si, 0))
    c2 = lambda shp: pl.BlockSpec(shp, lambda bi, si: (0, 0))
    return pl.pallas_call(
        _outproj_body,
        grid=(b, s // tm),
        in_specs=[row_spec(A_WIDTH), row_spec(R_WIDTH), row_spec(F_WIDTH),
                  c2((D_MODEL, D_MODEL)), row_spec(D_MODEL), c2((1, D_MODEL))],
        out_specs=row_spec(D_MODEL),
        out_shape=jax.ShapeDtypeStruct((b, s, D_MODEL), F32),
        compiler_params=pltpu.CompilerParams(
            dimension_semantics=("arbitrary", "arbitrary"), vmem_limit_bytes=VMEM_LIMIT),
        name="outproj",
    )(a, r, f, w, x, post_g)


@functools.lru_cache(maxsize=None)
def _tables(seq):
    half = HEAD_DIM // 2
    lane = np.arange(LANES)
    inv = ROPE_THETA ** (-(np.arange(half, dtype=np.float64) / half))
    ang = np.arange(seq, dtype=np.float64)[:, None] * inv[None, :]
    cos = np.cos(ang)
    sin = np.sin(ang)
    cos_t = cos[:, lane % half]
    sin_t = sin[:, lane % half] * np.where((lane % HEAD_DIM) < half, -1.0, 1.0)[None, :]

    gam = 1.0 - np.exp2(-5.0 - np.arange(R_HEADS, dtype=np.float64))
    log_gam = np.log(gam)
    pos = np.arange(L_RET)
    dist = np.abs(pos[:, None] - pos[None, :])
    allowed = (pos[None, :] // CHUNK) <= (pos[:, None] // CHUNK)
    dmat = np.where(allowed[None], np.exp(log_gam[:, None, None] * dist[None]), 0.0)
    dmat = np.concatenate([dmat[0::2], dmat[1::2]], axis=2)
    head_of_lane = lane // HEAD_DIM
    qdec = np.stack([np.exp(log_gam[2 * p + head_of_lane][None, :] * (pos + 1.0)[:, None])
                     for p in range(HEAD_PAIRS)])
    kdec = np.stack([np.exp(log_gam[2 * p + head_of_lane][None, :] * (L_RET - 1.0 - pos)[:, None])
                     for p in range(HEAD_PAIRS)])
    bmask = (head_of_lane[:, None] == head_of_lane[None, :]).astype(np.float32)
    sdec = np.stack([np.exp(log_gam[2 * p + head_of_lane] * L_RET)[:, None] * bmask
                     for p in range(HEAD_PAIRS)])
    tri = (np.arange(RC_IN)[None, :] <= np.arange(RC_IN)[:, None]).astype(np.float32)
    grp = np.arange(A_WIDTH) // HEAD_DIM
    mavg = (grp[:, None] == grp[None, :]).astype(np.float32) / HEAD_DIM
    place = np.zeros((LANES, F_HEADS * LANES), np.float32)
    for h in range(F_HEADS):
        for term in range(3):
            place[8 * term + h, h * LANES + HEAD_DIM + term] = 1.0
    f = lambda a: np.asarray(a, np.float32)
    return dict(cos=f(cos_t), sin=f(sin_t), dmat=f(dmat), qdec=f(qdec), kdec=f(kdec),
                sdec=f(sdec), bmask=f(bmask), tri=f(tri), mavg=f(mavg),
                mavg128=f(mavg[:LANES, :LANES]), place=f(place))


def _layer(x, pre_g, post_g, w_in, b_f, a_ln_g, a_ws, a_bs, w_out, t):
    w_pad = jnp.pad(w_in, ((0, 0), (0, D_IN_PAD - D_IN))).astype(BF16)
    bf_pad = jnp.pad(b_f, (0, LANES - F_HEADS)).reshape(1, LANES)
    abias = jnp.repeat(a_bs.T, HEAD_DIM, axis=1)
    (a_out, rq, rk, rv, rg, fqt, kaug, vaug, fg, cf_t) = _inproj(
        x, pre_g.reshape(1, D_MODEL), w_pad, bf_pad, a_ln_g.reshape(1, A_WIDTH), a_ws, abias,
        t["cos"], t["sin"], t["tri"], t["mavg"], t["place"])
    r_out = _retention(rq, rk, rv, rg, t["dmat"], t["qdec"], t["kdec"], t["sdec"], t["bmask"],
                       t["mavg128"])
    f_out = _fox(fqt, kaug, vaug, fg, cf_t)
    return _outproj(a_out, r_out, f_out, w_out.astype(BF16), x, post_g.reshape(1, D_MODEL))


def kernel(x, pre_gain, post_gain, w_in, b_forget, a_norm_gain, a_spatial_w, a_spatial_b, w_out):
    bf16_tables = ("tri", "mavg", "mavg128", "place")
    t = {k: jnp.asarray(v, BF16 if k in bf16_tables else F32) for k, v in _tables(x.shape[1]).items()}
    for l in range(pre_gain.shape[0]):
        x = _layer(x, pre_gain[l], post_gain[l], w_in[l], b_forget[l], a_norm_gain[l],
                   a_spatial_w[l], a_spatial_b[l], w_out[l], t)
    return x
```

```python
import functools
import math

import jax
import jax.numpy as jnp
import numpy as np
from jax import lax
from jax.experimental import pallas as pl
from jax.experimental.pallas import tpu as pltpu

F32 = jnp.float32
BF16 = jnp.bfloat16

D_MODEL = 1024
HEAD_DIM = 64
CHUNK = 64
A_WIDTH = 256
R_WIDTH = 384
F_WIDTH = 384
A_GROUPS = 4
R_HEADS = 6
F_HEADS = 6
A_BLOCK = 128
ROPE_THETA = 10000.0
EPS = 1e-6
LANES = 128
HEAD_PAIRS = R_WIDTH // LANES

A_COLS = 3 * A_WIDTH
R_COLS = 4 * R_WIDTH
F_COLS = 4 * F_WIDTH + F_HEADS
D_IN = A_COLS + R_COLS + F_COLS
D_IN_PAD = A_COLS + R_COLS + 4 * F_WIDTH + LANES

TM_IN = 1024
RC_IN = 256
TM_RET = 1024
L_RET = 128
TQ = 512
TK = 256
TM_OUT = 1024
RC_OUT = 256
VMEM_LIMIT = 56 * 1024 * 1024
VAUG_ROWS = HEAD_DIM + 16
LOG2E = 1.4426950408889634
MASK_VALUE = -1e30


def _silu(x):
    return x / (1.0 + jnp.exp(-x))


def _gelu_tanh(x):
    c = math.sqrt(2.0 / math.pi)
    return 0.5 * x * (1.0 + jnp.tanh(c * (x + 0.044715 * (x * x * x))))


def _log_sigmoid(x):
    return jnp.minimum(x, 0.0) - jnp.log1p(jnp.exp(-jnp.abs(x)))


def _dot(a, b):
    return jnp.dot(a, b, preferred_element_type=F32)


def _group_norm64(x, mavg):
    mean = _dot(x.astype(BF16), mavg)
    d = x - mean
    var = _dot((d * d).astype(BF16), mavg)
    return d * lax.rsqrt(var + EPS)


def _inproj_body(x_ref, pg_ref, w_ref, bf_ref, aln_ref, aws_ref, abias_ref, cos_ref, sin_ref,
                 tri_ref, mavg_ref, place_ref,
                 a_ref, rq_ref, rk_ref, rv_ref, rg_ref, fq_ref, kaug_ref, vaug_ref, fg_ref, cft_ref,
                 carry_ref):
    tm = x_ref.shape[0]

    @pl.when(pl.program_id(1) == 0)
    def _():
        carry_ref[...] = jnp.zeros_like(carry_ref)

    mavg = mavg_ref[...]
    tri = tri_ref[...]
    row = lax.broadcasted_iota(jnp.int32, (A_BLOCK, A_BLOCK), 0)
    col = lax.broadcasted_iota(jnp.int32, (A_BLOCK, A_BLOCK), 1)
    allowed = jnp.logical_not(jnp.logical_and(row < CHUNK, col >= CHUNK))
    wcat = jnp.concatenate(
        [jnp.where(allowed, aws_ref[gi], 0.0).astype(BF16) for gi in range(A_GROUPS)], axis=1)
    lane_grp = lax.shift_right_logical(
        lax.broadcasted_iota(jnp.int32, (A_BLOCK, A_WIDTH), 1), HEAD_DIM.bit_length() - 1)
    zero_bf = jnp.zeros((A_BLOCK, A_WIDTH), BF16)
    lane = lax.broadcasted_iota(jnp.int32, (1, LANES), 1)
    first_half = jnp.bitwise_and(lane, HEAD_DIM - 1) < (HEAD_DIM // 2)
    srow = lax.broadcasted_iota(jnp.int32, (VAUG_ROWS - HEAD_DIM, 1), 0)
    ones_rows = jnp.broadcast_to(jnp.where(srow == 0, 1.0, 0.0),
                                 (VAUG_ROWS - HEAD_DIM, RC_IN)).astype(BF16)

    def rope(z, cos_t, sin_t):
        outs = []
        for c in range(HEAD_PAIRS):
            zc = z[:, c * LANES:(c + 1) * LANES]
            rot = jnp.where(first_half, pltpu.roll(zc, LANES - HEAD_DIM // 2, 1),
                            pltpu.roll(zc, HEAD_DIM // 2, 1))
            outs.append(zc * cos_t + rot * sin_t)
        return jnp.concatenate(outs, axis=1)

    def normed(c):
        x = x_ref[c * RC_IN:(c + 1) * RC_IN, :]
        ms = jnp.mean(x * x, axis=-1, keepdims=True)
        return (x * lax.rsqrt(ms + EPS) * pg_ref[...]).astype(BF16)

    wide_cols = ((0, A_COLS), (A_COLS, A_COLS + R_COLS), (A_COLS + R_COLS, D_IN_PAD))

    def wide(h, k):
        return lax.dot_general(h, w_ref[wide_cols[k][0]:wide_cols[k][1], :],
                               (((1,), (1,)), ((), ())), preferred_element_type=F32)

    def stage1(c, z, e):
        za, zr, zf = z
        e["u"] = _gelu_tanh(za[:, 0:A_WIDTH])
        e["v"] = _gelu_tanh(za[:, A_WIDTH:2 * A_WIDTH])
        e["g"] = _silu(za[:, 2 * A_WIDTH:3 * A_WIDTH])
        e["mean"] = _dot(e["v"].astype(BF16), mavg)
        ls = _log_sigmoid(zf[:, 4 * F_WIDTH:4 * F_WIDTH + LANES] + bf_ref[...])
        ls_hi = ls.astype(BF16)
        ls_lo = (ls - ls_hi.astype(F32)).astype(BF16)
        cf = _dot(tri, ls_hi) + _dot(tri, ls_lo) + carry_ref[...]
        carry_ref[...] = cf[RC_IN - 1:RC_IN, :]
        e["cf2"] = cf * LOG2E

    def stage2(c, z, e):
        rows = slice(c * RC_IN, (c + 1) * RC_IN)
        za, zr, zf = z
        e["d"] = e["v"] - e["mean"]
        e["var"] = _dot((e["d"] * e["d"]).astype(BF16), mavg)
        cf2 = e["cf2"]
        cft_ref[:, rows] = cf2.T[:8, :]
        negf = -cf2
        hi = negf.astype(BF16).astype(F32)
        mid = (negf - hi).astype(BF16).astype(F32)
        lo = (negf - hi - mid).astype(BF16).astype(F32)
        packed = jnp.where(lane < 8, hi,
                           jnp.where(lane < 16, pltpu.roll(mid, 8, 1),
                                     jnp.where(lane < 24, pltpu.roll(lo, 16, 1), 0.0)))
        e["placed"] = _dot(packed.astype(BF16), place_ref[...])
        cos_t = cos_ref[rows, :]
        sin_t = sin_ref[rows, :]
        rq_ref[rows, :] = (rope(zr[:, 0:R_WIDTH], cos_t, sin_t) * (HEAD_DIM ** -0.5)).astype(BF16)
        rk_ref[rows, :] = rope(zr[:, R_WIDTH:2 * R_WIDTH], cos_t, sin_t).astype(BF16)
        rv_ref[rows, :] = zr[:, 2 * R_WIDTH:3 * R_WIDTH].astype(BF16)
        rg_ref[rows, :] = _silu(zr[:, 3 * R_WIDTH:4 * R_WIDTH]).astype(BF16)

    def stage3(c, z, e):
        rows = slice(c * RC_IN, (c + 1) * RC_IN)
        za, zr, zf = z
        vn = (e["d"] * lax.rsqrt(e["var"] + EPS) * aln_ref[...]).astype(BF16)
        mixed_blocks = []
        for nb in range(RC_IN // A_BLOCK):
            vb = vn[nb * A_BLOCK:(nb + 1) * A_BLOCK, :]
            vstack = jnp.concatenate(
                [jnp.where(lane_grp == gi, vb, zero_bf) for gi in range(A_GROUPS)], axis=0)
            mixed_blocks.append(_dot(wcat, vstack) + abias_ref[...])
        mixed = jnp.concatenate(mixed_blocks, axis=0)
        a_ref[rows, :] = (e["u"] * mixed * e["g"]).astype(BF16)
        fq_ref[:, rows] = (zf[:, 0:F_WIDTH] * (HEAD_DIM ** -0.5 * LOG2E)).T.astype(BF16)
        fg_ref[rows, :] = _silu(zf[:, 3 * F_WIDTH:4 * F_WIDTH]).astype(BF16)
        fv_t = zf[:, 2 * F_WIDTH:3 * F_WIDTH].T.astype(BF16)
        for hd in range(F_HEADS):
            p, hh = divmod(hd, 2)
            kk = zf[:, F_WIDTH + p * LANES:F_WIDTH + (p + 1) * LANES]
            if hh == 1:
                kk = pltpu.roll(kk, HEAD_DIM, 1)
            aug = e["placed"][:, hd * LANES:(hd + 1) * LANES]
            kaug_ref[hd, rows, :] = jnp.where(lane < HEAD_DIM, kk, aug).astype(BF16)
            vaug_ref[hd, 0:HEAD_DIM, rows] = fv_t[hd * HEAD_DIM:(hd + 1) * HEAD_DIM, :]
            vaug_ref[hd, HEAD_DIM:VAUG_ROWS, rows] = ones_rows

    stages = (stage1, stage2, stage3)
    nchunks = tm // RC_IN
    h0 = normed(0)
    z_next = tuple(wide(h0, k) for k in range(3))
    for c in range(nchunks):
        z, e = z_next, {}
        if c + 1 < nchunks:
            h_next = normed(c + 1)
            z_new = []
            for k in range(3):
                z_new.append(wide(h_next, k))
                stages[k](c, z, e)
            z_next = tuple(z_new)
        else:
            for k in range(3):
                stages[k](c, z, e)


def _inproj(x, pre_g, w_pad, bf_pad, aln, aws, abias, cos_t, sin_t, tri, mavg, place):
    b, s, _ = x.shape
    tm = TM_IN
    grid = (b, s // tm)
    row_spec = lambda n: pl.BlockSpec((None, tm, n), lambda bi, si: (bi, si, 0))
    col_spec = lambda n: pl.BlockSpec((None, n, tm), lambda bi, si: (bi, 0, si))
    const2 = lambda shp: pl.BlockSpec(shp, lambda bi, si: (0, 0))
    bf_sds = lambda n: jax.ShapeDtypeStruct((b, s, n), BF16)
    out_shape = ([bf_sds(A_WIDTH)] + [bf_sds(R_WIDTH)] * 4
                 + [jax.ShapeDtypeStruct((b, F_WIDTH, s), BF16),
                    jax.ShapeDtypeStruct((b, F_HEADS, s, LANES), BF16),
                    jax.ShapeDtypeStruct((b, F_HEADS, VAUG_ROWS, s), BF16),
                    bf_sds(F_WIDTH),
                    jax.ShapeDtypeStruct((b, 8, s), F32)])
    out_specs = ([row_spec(A_WIDTH)] + [row_spec(R_WIDTH)] * 4
                 + [col_spec(F_WIDTH),
                    pl.BlockSpec((None, F_HEADS, tm, LANES), lambda bi, si: (bi, 0, si, 0)),
                    pl.BlockSpec((None, F_HEADS, VAUG_ROWS, tm), lambda bi, si: (bi, 0, 0, si)),
                    row_spec(F_WIDTH),
                    col_spec(8)])
    in_specs = [
        row_spec(D_MODEL),
        const2((1, D_MODEL)),
        pl.BlockSpec((D_IN_PAD, D_MODEL), lambda bi, si: (0, 0), pipeline_mode=pl.Buffered(1)),
        const2((1, LANES)),
        const2((1, A_WIDTH)),
        pl.BlockSpec((A_GROUPS, A_BLOCK, A_BLOCK), lambda bi, si: (0, 0, 0)),
        const2((A_BLOCK, A_WIDTH)),
        pl.BlockSpec((tm, LANES), lambda bi, si: (si, 0)),
        pl.BlockSpec((tm, LANES), lambda bi, si: (si, 0)),
        const2((RC_IN, RC_IN)),
        const2((A_WIDTH, A_WIDTH)),
        const2((LANES, F_HEADS * LANES)),
    ]
    return pl.pallas_call(
        _inproj_body,
        grid=grid,
        in_specs=in_specs,
        out_specs=out_specs,
        out_shape=out_shape,
        scratch_shapes=[pltpu.VMEM((1, LANES), F32)],
        compiler_params=pltpu.CompilerParams(
            dimension_semantics=("arbitrary", "arbitrary"), vmem_limit_bytes=VMEM_LIMIT),
        name="inproj",
    )(x, pre_g, w_pad, bf_pad, aln, aws, abias, cos_t, sin_t, tri, mavg, place)


def _retention_body(q_ref, k_ref, v_ref, g_ref, dmat_ref, qdec_ref, kdec_ref, sdec_ref, bmask_ref,
                    mavg_ref, o_ref, state_ref):
    tm = q_ref.shape[0]

    @pl.when(pl.program_id(1) == 0)
    def _():
        state_ref[...] = jnp.zeros_like(state_ref)

    lane = lax.broadcasted_iota(jnp.int32, (1, LANES), 1)
    lower = lane < HEAD_DIM
    mavg = mavg_ref[...]
    bmask = bmask_ref[...]
    contract_last = (((1,), (1,)), ((), ()))
    contract_first = (((0,), (0,)), ((), ()))
    nblk = tm // L_RET
    units = [(p, j) for p in range(HEAD_PAIRS) for j in range(nblk)]

    def split_heads(x2):
        zero = jnp.zeros_like(x2)
        return jnp.concatenate([jnp.where(lower, x2, zero), jnp.where(lower, zero, x2)], axis=0)

    q2, v2, sc, kv = {}, {}, {}, {}
    for (p, j) in units:
        rs, cs = slice(j * L_RET, (j + 1) * L_RET), slice(p * LANES, (p + 1) * LANES)
        q2[p, j] = q_ref[rs, cs]
        k2 = k_ref[rs, cs]
        v2[p, j] = v_ref[rs, cs]
        sc[p, j] = lax.dot_general(q2[p, j], split_heads(k2), contract_last,
                                   preferred_element_type=F32)
        kd = (k2.astype(F32) * kdec_ref[p]).astype(BF16)
        kv[p, j] = lax.dot_general(kd, v2[p, j], contract_first, preferred_element_type=F32)

    st = {}
    for p in range(HEAD_PAIRS):
        state = state_ref[p]
        for j in range(nblk):
            st[p, j] = state.astype(BF16)
            state = state * sdec_ref[p] + kv[p, j] * bmask
        state_ref[p] = state

    out = {}
    for (p, j) in units:
        pm = (sc[p, j] * dmat_ref[p]).astype(BF16)
        out[p, j] = _dot(pm, split_heads(v2[p, j])) + _dot(q2[p, j], st[p, j]) * qdec_ref[p]

    tiles = [jnp.concatenate([out[p, j] for j in range(nblk)], axis=0) for p in range(HEAD_PAIRS)]
    means = [_dot(t.astype(BF16), mavg) for t in tiles]
    devs = [t - mu for t, mu in zip(tiles, means)]
    variances = [_dot((d * d).astype(BF16), mavg) for d in devs]
    for p in range(HEAD_PAIRS):
        cs = slice(p * LANES, (p + 1) * LANES)
        y = devs[p] * lax.rsqrt(variances[p] + EPS) * g_ref[:, cs].astype(F32)
        o_ref[:, cs] = y.astype(BF16)


def _retention(rq, rk, rv, rg, dmat, qdec, kdec, sdec, bmask, mavg128):
    b, s, _ = rq.shape
    tm = TM_RET
    row_spec = pl.BlockSpec((None, tm, R_WIDTH), lambda bi, si: (bi, si, 0))
    c3 = lambda shp: pl.BlockSpec(shp, lambda bi, si: (0, 0, 0))
    c2 = lambda shp: pl.BlockSpec(shp, lambda bi, si: (0, 0))
    return pl.pallas_call(
        _retention_body,
        grid=(b, s // tm),
        in_specs=[row_spec, row_spec, row_spec, row_spec,
                  c3((HEAD_PAIRS, L_RET, 2 * L_RET)), c3((HEAD_PAIRS, L_RET, LANES)),
                  c3((HEAD_PAIRS, L_RET, LANES)), c3((HEAD_PAIRS, LANES, LANES)),
                  c2((LANES, LANES)), c2((LANES, LANES))],
        out_specs=row_spec,
        out_shape=jax.ShapeDtypeStruct((b, s, R_WIDTH), BF16),
        scratch_shapes=[pltpu.VMEM((HEAD_PAIRS, LANES, LANES), F32)],
        compiler_params=pltpu.CompilerParams(
            dimension_semantics=("arbitrary", "arbitrary"), vmem_limit_bytes=VMEM_LIMIT),
        name="retention",
    )(rq, rk, rv, rg, dmat, qdec, kdec, sdec, bmask, mavg128)


def _fox_body(qt_ref, kaug_ref, vaug_ref, cft_ref, g_ref, o_ref, m_ref, acc_ref, sa_ref, sb_ref):
    qi = pl.program_id(1)
    srow_q = lax.broadcasted_iota(jnp.int32, (HEAD_DIM, 1), 0)
    q_ones = jnp.broadcast_to(jnp.where(srow_q < 3, 1.0, 0.0), (HEAD_DIM, TQ)).astype(BF16)
    qta = [jnp.concatenate([qt_ref[h * HEAD_DIM:(h + 1) * HEAD_DIM, :], q_ones], axis=0)
           for h in range(F_HEADS)]
    fq = [cft_ref[h:h + 1, :] for h in range(F_HEADS)]
    krow = lax.broadcasted_iota(jnp.int32, (TK, TQ), 0)
    qcol = lax.broadcasted_iota(jnp.int32, (TK, TQ), 1)
    causal = (krow <= qcol, krow + TK <= qcol)

    for h in range(F_HEADS):
        m_ref[h] = jnp.full((8, TQ), MASK_VALUE, F32)
        acc_ref[h] = jnp.zeros((VAUG_ROWS, TQ), F32)

    def scores(kj, dst_ref, h):
        k0 = pl.multiple_of(kj * TK, TK)
        dst_ref[h] = _dot(kaug_ref[h, pl.ds(k0, TK), :], qta[h])

    def consume(kj, src_ref, h, diagonal):
        k0 = pl.multiple_of(kj * TK, TK)
        st = src_ref[h]
        if diagonal is not None:
            st = jnp.where(causal[diagonal], st, MASK_VALUE)
        m_old = m_ref[h][0:1, :]
        m_new = jnp.maximum(m_old, jnp.max(st, axis=0, keepdims=True) + fq[h])
        p = jnp.exp2(st - (m_new - fq[h]))
        alpha = jnp.exp2(m_old - m_new)
        pv = _dot(vaug_ref[h, :, pl.ds(k0, TK)], p.astype(BF16))
        acc_ref[h] = alpha * acc_ref[h] + pv
        m_ref[h] = jnp.broadcast_to(m_new, (8, TQ))

    def overlapped(next_j, dst_ref, cur_j, src_ref, diagonal=None):
        lead = 2
        for h in range(lead):
            scores(next_j, dst_ref, h)
        for h in range(F_HEADS):
            consume(cur_j, src_ref, h, diagonal)
            if h + lead < F_HEADS:
                scores(next_j, dst_ref, h + lead)

    for h in range(F_HEADS):
        scores(0, sa_ref, h)

    def body(i, carry):
        j0 = 2 * i
        overlapped(j0 + 1, sb_ref, j0, sa_ref)
        overlapped(j0 + 2, sa_ref, j0 + 1, sb_ref)
        return carry

    lax.fori_loop(0, qi, body, 0)
    overlapped(2 * qi + 1, sb_ref, 2 * qi, sa_ref, diagonal=0)
    for h in range(F_HEADS):
        consume(2 * qi + 1, sb_ref, h, 1)

    for p in range(HEAD_PAIRS):
        halves = []
        for hh in range(2):
            acc = acc_ref[2 * p + hh]
            halves.append(acc[0:HEAD_DIM, :] / acc[HEAD_DIM:HEAD_DIM + 1, :])
        out_t = jnp.concatenate(halves, axis=0)
        cs = slice(p * LANES, (p + 1) * LANES)
        o_ref[:, cs] = (out_t.T * g_ref[:, cs].astype(F32)).astype(BF16)


def _fox(fqt, kaug, vaug, fg, cf_t):
    b, _, s = fqt.shape
    assert TQ == 2 * TK
    row_spec = pl.BlockSpec((None, TQ, F_WIDTH), lambda bi, qi: (bi, qi, 0))
    return pl.pallas_call(
        _fox_body,
        grid=(b, s // TQ),
        in_specs=[pl.BlockSpec((None, F_WIDTH, TQ), lambda bi, qi: (bi, 0, qi)),
                  pl.BlockSpec((None, F_HEADS, s, LANES), lambda bi, qi: (bi, 0, 0, 0)),
                  pl.BlockSpec((None, F_HEADS, VAUG_ROWS, s), lambda bi, qi: (bi, 0, 0, 0)),
                  pl.BlockSpec((None, 8, TQ), lambda bi, qi: (bi, 0, qi)),
                  row_spec],
        out_specs=row_spec,
        out_shape=jax.ShapeDtypeStruct((b, s, F_WIDTH), BF16),
        scratch_shapes=[pltpu.VMEM((F_HEADS, 8, TQ), F32),
                        pltpu.VMEM((F_HEADS, VAUG_ROWS, TQ), F32),
                        pltpu.VMEM((F_HEADS, TK, TQ), F32),
                        pltpu.VMEM((F_HEADS, TK, TQ), F32)],
        compiler_params=pltpu.CompilerParams(
            dimension_semantics=("arbitrary", "arbitrary"), vmem_limit_bytes=VMEM_LIMIT),
        name="fox",
    )(fqt, kaug, vaug, cf_t, fg)


def _outproj_body(a_ref, r_ref, f_ref, w_ref, x_ref, pg_ref, o_ref):
    tm = x_ref.shape[0]

    def project(c):
        rows = slice(c * RC_OUT, (c + 1) * RC_OUT)
        y = jnp.concatenate([a_ref[rows, :], r_ref[rows, :], f_ref[rows, :]], axis=1)
        return _dot(y, w_ref[...])

    nchunks = tm // RC_OUT
    o_next = project(0)
    for c in range(nchunks):
        rows = slice(c * RC_OUT, (c + 1) * RC_OUT)
        o = o_next
        if c + 1 < nchunks:
            o_next = project(c + 1)
        ms = jnp.mean(o * o, axis=-1, keepdims=True)
        o_ref[rows, :] = x_ref[rows, :] + o * lax.rsqrt(ms + EPS) * pg_ref[...]


def _outproj(a, r, f, w, x, post_g):
    b, s, _ = x.shape
    tm = TM_OUT
    row_spec = lambda n: pl.BlockSpec((None, tm, n), lambda bi, si: (bi, si, 0))
    c2 = lambda shp: pl.BlockSpec(shp, lambda bi, si: (0, 0))
    return pl.pallas_call(
        _outproj_body,
        grid=(b, s // tm),
        in_specs=[row_spec(A_WIDTH), row_spec(R_WIDTH), row_spec(F_WIDTH),
                  c2((D_MODEL, D_MODEL)), row_spec(D_MODEL), c2((1, D_MODEL))],
        out_specs=row_spec(D_MODEL),
        out_shape=jax.ShapeDtypeStruct((b, s, D_MODEL), F32),
        compiler_params=pltpu.CompilerParams(
            dimension_semantics=("arbitrary", "arbitrary"), vmem_limit_bytes=VMEM_LIMIT),
        name="outproj",
    )(a, r, f, w, x, post_g)


@functools.lru_cache(maxsize=None)
def _tables(seq):
    half = HEAD_DIM // 2
    lane = np.arange(LANES)
    inv = ROPE_THETA ** (-(np.arange(half, dtype=np.float64) / half))
    ang = np.arange(seq, dtype=np.float64)[:, None] * inv[None, :]
    cos = np.cos(ang)
    sin = np.sin(ang)
    cos_t = cos[:, lane % half]
    sin_t = sin[:, lane % half] * np.where((lane % HEAD_DIM) < half, -1.0, 1.0)[None, :]

    gam = 1.0 - np.exp2(-5.0 - np.arange(R_HEADS, dtype=np.float64))
    log_gam = np.log(gam)
    pos = np.arange(L_RET)
    dist = np.abs(pos[:, None] - pos[None, :])
    allowed = (pos[None, :] // CHUNK) <= (pos[:, None] // CHUNK)
    dmat = np.where(allowed[None], np.exp(log_gam[:, None, None] * dist[None]), 0.0)
    dmat = np.concatenate([dmat[0::2], dmat[1::2]], axis=2)
    head_of_lane = lane // HEAD_DIM
    qdec = np.stack([np.exp(log_gam[2 * p + head_of_lane][None, :] * (pos + 1.0)[:, None])
                     for p in range(HEAD_PAIRS)])
    kdec = np.stack([np.exp(log_gam[2 * p + head_of_lane][None, :] * (L_RET - 1.0 - pos)[:, None])
                     for p in range(HEAD_PAIRS)])
    bmask = (head_of_lane[:, None] == head_of_lane[None, :]).astype(np.float32)
    sdec = np.stack([np.exp(log_gam[2 * p + head_of_lane] * L_RET)[:, None] * bmask
                     for p in range(HEAD_PAIRS)])
    tri = (np.arange(RC_IN)[None, :] <= np.arange(RC_IN)[:, None]).astype(np.float32)
    grp = np.arange(A_WIDTH) // HEAD_DIM
    mavg = (grp[:, None] == grp[None, :]).astype(np.float32) / HEAD_DIM
    place = np.zeros((LANES, F_HEADS * LANES), np.float32)
    for h in range(F_HEADS):
        for term in range(3):
            place[8 * term + h, h * LANES + HEAD_DIM + term] = 1.0
    f = lambda a: np.asarray(a, np.float32)
    return dict(cos=f(cos_t), sin=f(sin_t), dmat=f(dmat), qdec=f(qdec), kdec=f(kdec),
                sdec=f(sdec), bmask=f(bmask), tri=f(tri), mavg=f(mavg),
                mavg128=f(mavg[:LANES, :LANES]), place=f(place))


def _layer(x, pre_g, post_g, w_in, b_f, a_ln_g, a_ws, a_bs, w_out, t):
    w_pad = jnp.pad(w_in.T, ((0, D_IN_PAD - D_IN), (0, 0))).astype(BF16)
    bf_pad = jnp.pad(b_f, (0, LANES - F_HEADS)).reshape(1, LANES)
    abias = jnp.repeat(a_bs.T, HEAD_DIM, axis=1)
    (a_out, rq, rk, rv, rg, fqt, kaug, vaug, fg, cf_t) = _inproj(
        x, pre_g.reshape(1, D_MODEL), w_pad, bf_pad, a_ln_g.reshape(1, A_WIDTH), a_ws, abias,
        t["cos"], t["sin"], t["tri"], t["mavg"], t["place"])
    r_out = _retention(rq, rk, rv, rg, t["dmat"], t["qdec"], t["kdec"], t["sdec"], t["bmask"],
                       t["mavg128"])
    f_out = _fox(fqt, kaug, vaug, fg, cf_t)
    return _outproj(a_out, r_out, f_out, w_out.astype(BF16), x, post_g.reshape(1, D_MODEL))


def kernel(x, pre_gain, post_gain, w_in, b_forget, a_norm_gain, a_spatial_w, a_spatial_b, w_out):
    bf16_tables = ("tri", "mavg", "mavg128", "place")
    t = {k: jnp.asarray(v, BF16 if k in bf16_tables else F32) for k, v in _tables(x.shape[1]).items()}
    for l in range(pre_gain.shape[0]):
        x = _layer(x, pre_gain[l], post_gain[l], w_in[l], b_forget[l], a_norm_gain[l],
                   a_spatial_w[l], a_spatial_b[l], w_out[l], t)
    return x
```

```python
import functools
import math

import jax
import jax.numpy as jnp
import numpy as np
from jax import lax
from jax.experimental import pallas as pl
from jax.experimental.pallas import tpu as pltpu

F32 = jnp.float32
BF16 = jnp.bfloat16

D_MODEL = 1024
HEAD_DIM = 64
CHUNK = 64
A_WIDTH = 256
R_WIDTH = 384
F_WIDTH = 384
A_GROUPS = 4
R_HEADS = 6
F_HEADS = 6
A_BLOCK = 128
ROPE_THETA = 10000.0
EPS = 1e-6
LANES = 128
HEAD_PAIRS = R_WIDTH // LANES

A_COLS = 3 * A_WIDTH
R_COLS = 4 * R_WIDTH
F_COLS = 4 * F_WIDTH + F_HEADS
D_IN = A_COLS + R_COLS + F_COLS
D_IN_PAD = A_COLS + R_COLS + 4 * F_WIDTH + LANES

TM_IN = 1024
RC_IN = 256
TM_RET = 1024
L_RET = 128
TQ = 512
TK = 256
TM_OUT = 1024
RC_OUT = 256
VMEM_LIMIT = 56 * 1024 * 1024
VAUG_ROWS = HEAD_DIM + 16
LOG2E = 1.4426950408889634
MASK_VALUE = -1e30


def _silu(x):
    return 0.5 * x * (1.0 + jnp.tanh(0.5 * x))


def _gelu_tanh(x):
    c = math.sqrt(2.0 / math.pi)
    return 0.5 * x * (1.0 + jnp.tanh(c * (x + 0.044715 * (x * x * x))))


def _log_sigmoid(x):
    return jnp.minimum(x, 0.0) - jnp.log1p(jnp.exp(-jnp.abs(x)))


def _dot(a, b):
    return jnp.dot(a, b, preferred_element_type=F32)


def _group_norm64(x, mavg):
    mean = _dot(x.astype(BF16), mavg)
    d = x - mean
    var = _dot((d * d).astype(BF16), mavg)
    return d * lax.rsqrt(var + EPS)


def _inproj_body(x_ref, pg_ref, w_ref, bf_ref, aln_ref, aws_ref, abias_ref, cos_ref, sin_ref,
                 tri_ref, mavg_ref, place_ref,
                 a_ref, rq_ref, rk_ref, rv_ref, rg_ref, fq_ref, kaug_ref, vaug_ref, fg_ref, cft_ref,
                 carry_ref):
    tm = x_ref.shape[0]

    @pl.when(pl.program_id(1) == 0)
    def _():
        carry_ref[...] = jnp.zeros_like(carry_ref)

    mavg = mavg_ref[...]
    tri = tri_ref[...]
    row = lax.broadcasted_iota(jnp.int32, (A_BLOCK, A_BLOCK), 0)
    col = lax.broadcasted_iota(jnp.int32, (A_BLOCK, A_BLOCK), 1)
    allowed = jnp.logical_not(jnp.logical_and(row < CHUNK, col >= CHUNK))
    wcat = jnp.concatenate(
        [jnp.where(allowed, aws_ref[gi], 0.0).astype(BF16) for gi in range(A_GROUPS)], axis=1)
    lane_grp = lax.shift_right_logical(
        lax.broadcasted_iota(jnp.int32, (A_BLOCK, A_WIDTH), 1), HEAD_DIM.bit_length() - 1)
    zero_bf = jnp.zeros((A_BLOCK, A_WIDTH), BF16)
    lane = lax.broadcasted_iota(jnp.int32, (1, LANES), 1)
    srow = lax.broadcasted_iota(jnp.int32, (VAUG_ROWS - HEAD_DIM, 1), 0)
    ones_rows = jnp.broadcast_to(jnp.where(srow == 0, 1.0, 0.0),
                                 (VAUG_ROWS - HEAD_DIM, RC_IN)).astype(BF16)

    def rope(z, cos_t, sin_t):
        outs = []
        for c in range(HEAD_PAIRS):
            zc = z[:, c * LANES:(c + 1) * LANES]
            outs.append(zc * cos_t + pltpu.roll(zc, HEAD_DIM, 1) * sin_t)
        return jnp.concatenate(outs, axis=1)

    def normed(c):
        x = x_ref[c * RC_IN:(c + 1) * RC_IN, :]
        ms = jnp.mean(x * x, axis=-1, keepdims=True)
        return (x * lax.rsqrt(ms + EPS) * pg_ref[...]).astype(BF16)

    wide_cols = ((0, A_COLS), (A_COLS, A_COLS + R_COLS), (A_COLS + R_COLS, D_IN_PAD))

    def wide(h, k):
        return lax.dot_general(h, w_ref[wide_cols[k][0]:wide_cols[k][1], :],
                               (((1,), (1,)), ((), ())), preferred_element_type=F32)

    def stage1(c, z, e):
        za, zr, zf = z
        e["u"] = _gelu_tanh(za[:, 0:A_WIDTH])
        e["v"] = _gelu_tanh(za[:, A_WIDTH:2 * A_WIDTH])
        e["g"] = _silu(za[:, 2 * A_WIDTH:3 * A_WIDTH])
        e["mean"] = _dot(e["v"].astype(BF16), mavg)
        ls = _log_sigmoid(zf[:, 4 * F_WIDTH:4 * F_WIDTH + LANES] + bf_ref[...])
        ls_hi = ls.astype(BF16)
        ls_lo = (ls - ls_hi.astype(F32)).astype(BF16)
        cf = _dot(tri, ls_hi) + _dot(tri, ls_lo) + carry_ref[...]
        carry_ref[...] = cf[RC_IN - 1:RC_IN, :]
        e["cf2"] = cf * LOG2E

    def stage2(c, z, e):
        rows = slice(c * RC_IN, (c + 1) * RC_IN)
        za, zr, zf = z
        e["d"] = e["v"] - e["mean"]
        e["var"] = _dot((e["d"] * e["d"]).astype(BF16), mavg)
        cf2 = e["cf2"]
        cft_ref[:, rows] = cf2.T[:8, :]
        negf = -cf2
        hi = negf.astype(BF16).astype(F32)
        mid = (negf - hi).astype(BF16).astype(F32)
        lo = (negf - hi - mid).astype(BF16).astype(F32)
        packed = jnp.where(lane < 8, hi,
                           jnp.where(lane < 16, pltpu.roll(mid, 8, 1),
                                     jnp.where(lane < 24, pltpu.roll(lo, 16, 1), 0.0)))
        e["placed"] = _dot(packed.astype(BF16), place_ref[...])
        cos_t = cos_ref[rows, :]
        sin_t = sin_ref[rows, :]
        rq_ref[rows, :] = (rope(zr[:, 0:R_WIDTH], cos_t, sin_t) * (HEAD_DIM ** -0.5)).astype(BF16)
        rk_ref[rows, :] = rope(zr[:, R_WIDTH:2 * R_WIDTH], cos_t, sin_t).astype(BF16)
        rv_ref[rows, :] = zr[:, 2 * R_WIDTH:3 * R_WIDTH].astype(BF16)
        rg_ref[rows, :] = _silu(zr[:, 3 * R_WIDTH:4 * R_WIDTH]).astype(BF16)

    def stage3(c, z, e):
        rows = slice(c * RC_IN, (c + 1) * RC_IN)
        za, zr, zf = z
        vn = (e["d"] * lax.rsqrt(e["var"] + EPS) * aln_ref[...]).astype(BF16)
        mixed_blocks = []
        for nb in range(RC_IN // A_BLOCK):
            vb = vn[nb * A_BLOCK:(nb + 1) * A_BLOCK, :]
            vstack = jnp.concatenate(
                [jnp.where(lane_grp == gi, vb, zero_bf) for gi in range(A_GROUPS)], axis=0)
            mixed_blocks.append(_dot(wcat, vstack) + abias_ref[...])
        mixed = jnp.concatenate(mixed_blocks, axis=0)
        a_ref[rows, :] = (e["u"] * mixed * e["g"]).astype(BF16)
        fq_ref[:, rows] = (zf[:, 0:F_WIDTH] * (HEAD_DIM ** -0.5 * LOG2E)).T.astype(BF16)
        fg_ref[rows, :] = _silu(zf[:, 3 * F_WIDTH:4 * F_WIDTH]).astype(BF16)
        fv_t = zf[:, 2 * F_WIDTH:3 * F_WIDTH].T.astype(BF16)
        for hd in range(F_HEADS):
            p, hh = divmod(hd, 2)
            kk = zf[:, F_WIDTH + p * LANES:F_WIDTH + (p + 1) * LANES]
            if hh == 1:
                kk = pltpu.roll(kk, HEAD_DIM, 1)
            aug = e["placed"][:, hd * LANES:(hd + 1) * LANES]
            kaug_ref[hd, rows, :] = jnp.where(lane < HEAD_DIM, kk, aug).astype(BF16)
            vaug_ref[hd, 0:HEAD_DIM, rows] = fv_t[hd * HEAD_DIM:(hd + 1) * HEAD_DIM, :]
            vaug_ref[hd, HEAD_DIM:VAUG_ROWS, rows] = ones_rows

    stages = (stage1, stage2, stage3)
    nchunks = tm // RC_IN
    h0 = normed(0)
    z_next = tuple(wide(h0, k) for k in range(3))
    for c in range(nchunks):
        z, e = z_next, {}
        if c + 1 < nchunks:
            h_next = normed(c + 1)
            z_new = []
            for k in range(3):
                z_new.append(wide(h_next, k))
                stages[k](c, z, e)
            z_next = tuple(z_new)
        else:
            for k in range(3):
                stages[k](c, z, e)


def _inproj(x, pre_g, w_pad, bf_pad, aln, aws, abias, cos_t, sin_t, tri, mavg, place):
    b, s, _ = x.shape
    tm = TM_IN
    grid = (b, s // tm)
    row_spec = lambda n: pl.BlockSpec((None, tm, n), lambda bi, si: (bi, si, 0))
    col_spec = lambda n: pl.BlockSpec((None, n, tm), lambda bi, si: (bi, 0, si))
    const2 = lambda shp: pl.BlockSpec(shp, lambda bi, si: (0, 0))
    bf_sds = lambda n: jax.ShapeDtypeStruct((b, s, n), BF16)
    out_shape = ([bf_sds(A_WIDTH)] + [bf_sds(R_WIDTH)] * 4
                 + [jax.ShapeDtypeStruct((b, F_WIDTH, s), BF16),
                    jax.ShapeDtypeStruct((b, F_HEADS, s, LANES), BF16),
                    jax.ShapeDtypeStruct((b, F_HEADS, VAUG_ROWS, s), BF16),
                    bf_sds(F_WIDTH),
                    jax.ShapeDtypeStruct((b, 8, s), F32)])
    out_specs = ([row_spec(A_WIDTH)] + [row_spec(R_WIDTH)] * 4
                 + [col_spec(F_WIDTH),
                    pl.BlockSpec((None, F_HEADS, tm, LANES), lambda bi, si: (bi, 0, si, 0)),
                    pl.BlockSpec((None, F_HEADS, VAUG_ROWS, tm), lambda bi, si: (bi, 0, 0, si)),
                    row_spec(F_WIDTH),
                    col_spec(8)])
    in_specs = [
        row_spec(D_MODEL),
        const2((1, D_MODEL)),
        pl.BlockSpec((D_IN_PAD, D_MODEL), lambda bi, si: (0, 0), pipeline_mode=pl.Buffered(1)),
        const2((1, LANES)),
        const2((1, A_WIDTH)),
        pl.BlockSpec((A_GROUPS, A_BLOCK, A_BLOCK), lambda bi, si: (0, 0, 0)),
        const2((A_BLOCK, A_WIDTH)),
        pl.BlockSpec((tm, LANES), lambda bi, si: (si, 0)),
        pl.BlockSpec((tm, LANES), lambda bi, si: (si, 0)),
        const2((RC_IN, RC_IN)),
        const2((A_WIDTH, A_WIDTH)),
        const2((LANES, F_HEADS * LANES)),
    ]
    return pl.pallas_call(
        _inproj_body,
        grid=grid,
        in_specs=in_specs,
        out_specs=out_specs,
        out_shape=out_shape,
        scratch_shapes=[pltpu.VMEM((1, LANES), F32)],
        compiler_params=pltpu.CompilerParams(
            dimension_semantics=("arbitrary", "arbitrary"), vmem_limit_bytes=VMEM_LIMIT),
        name="inproj",
    )(x, pre_g, w_pad, bf_pad, aln, aws, abias, cos_t, sin_t, tri, mavg, place)


def _retention_body(q_ref, k_ref, v_ref, g_ref, dmat_ref, qdec_ref, kdec_ref, sdec_ref, bmask_ref,
                    mavg_ref, o_ref, state_ref):
    tm = q_ref.shape[0]

    @pl.when(pl.program_id(1) == 0)
    def _():
        state_ref[...] = jnp.zeros_like(state_ref)

    lane = lax.broadcasted_iota(jnp.int32, (1, LANES), 1)
    head0_v = lane < HEAD_DIM
    head0_qk = jnp.bitwise_and(lane, HEAD_DIM // 2) == 0
    mavg = mavg_ref[...]
    bmask = bmask_ref[...]
    contract_last = (((1,), (1,)), ((), ()))
    contract_first = (((0,), (0,)), ((), ()))
    nblk = tm // L_RET
    units = [(p, j) for p in range(HEAD_PAIRS) for j in range(nblk)]

    def split_heads(x2, head0):
        zero = jnp.zeros_like(x2)
        return jnp.concatenate([jnp.where(head0, x2, zero), jnp.where(head0, zero, x2)], axis=0)

    q2, v2, sc, kv = {}, {}, {}, {}
    for (p, j) in units:
        rs, cs = slice(j * L_RET, (j + 1) * L_RET), slice(p * LANES, (p + 1) * LANES)
        q2[p, j] = q_ref[rs, cs]
        k2 = k_ref[rs, cs]
        v2[p, j] = v_ref[rs, cs]
        sc[p, j] = lax.dot_general(q2[p, j], split_heads(k2, head0_qk), contract_last,
                                   preferred_element_type=F32)
        kd = (k2.astype(F32) * kdec_ref[p]).astype(BF16)
        kv[p, j] = lax.dot_general(kd, v2[p, j], contract_first, preferred_element_type=F32)

    st = {}
    for p in range(HEAD_PAIRS):
        state = state_ref[p]
        for j in range(nblk):
            st[p, j] = state.astype(BF16)
            state = state * sdec_ref[p] + kv[p, j] * bmask
        state_ref[p] = state

    out = {}
    for (p, j) in units:
        pm = (sc[p, j] * dmat_ref[p]).astype(BF16)
        out[p, j] = (_dot(pm, split_heads(v2[p, j], head0_v))
                     + _dot(q2[p, j], st[p, j]) * qdec_ref[p])

    tiles = [jnp.concatenate([out[p, j] for j in range(nblk)], axis=0) for p in range(HEAD_PAIRS)]
    means = [_dot(t.astype(BF16), mavg) for t in tiles]
    devs = [t - mu for t, mu in zip(tiles, means)]
    variances = [_dot((d * d).astype(BF16), mavg) for d in devs]
    for p in range(HEAD_PAIRS):
        cs = slice(p * LANES, (p + 1) * LANES)
        y = devs[p] * lax.rsqrt(variances[p] + EPS) * g_ref[:, cs].astype(F32)
        o_ref[:, cs] = y.astype(BF16)


def _retention(rq, rk, rv, rg, dmat, qdec, kdec, sdec, bmask, mavg128):
    b, s, _ = rq.shape
    tm = TM_RET
    row_spec = pl.BlockSpec((None, tm, R_WIDTH), lambda bi, si: (bi, si, 0))
    c3 = lambda shp: pl.BlockSpec(shp, lambda bi, si: (0, 0, 0))
    c2 = lambda shp: pl.BlockSpec(shp, lambda bi, si: (0, 0))
    return pl.pallas_call(
        _retention_body,
        grid=(b, s // tm),
        in_specs=[row_spec, row_spec, row_spec, row_spec,
                  c3((HEAD_PAIRS, L_RET, 2 * L_RET)), c3((HEAD_PAIRS, L_RET, LANES)),
                  c3((HEAD_PAIRS, L_RET, LANES)), c3((HEAD_PAIRS, LANES, LANES)),
                  c2((LANES, LANES)), c2((LANES, LANES))],
        out_specs=row_spec,
        out_shape=jax.ShapeDtypeStruct((b, s, R_WIDTH), BF16),
        scratch_shapes=[pltpu.VMEM((HEAD_PAIRS, LANES, LANES), F32)],
        compiler_params=pltpu.CompilerParams(
            dimension_semantics=("arbitrary", "arbitrary"), vmem_limit_bytes=VMEM_LIMIT),
        name="retention",
    )(rq, rk, rv, rg, dmat, qdec, kdec, sdec, bmask, mavg128)


def _fox_body(qt_ref, kaug_ref, vaug_ref, cft_ref, g_ref, o_ref, m_ref, acc_ref, sa_ref, sb_ref):
    qi = pl.program_id(1)
    srow_q = lax.broadcasted_iota(jnp.int32, (HEAD_DIM, 1), 0)
    q_ones = jnp.broadcast_to(jnp.where(srow_q < 3, 1.0, 0.0), (HEAD_DIM, TQ)).astype(BF16)
    qta = [jnp.concatenate([qt_ref[h * HEAD_DIM:(h + 1) * HEAD_DIM, :], q_ones], axis=0)
           for h in range(F_HEADS)]
    fq = [cft_ref[h:h + 1, :] for h in range(F_HEADS)]
    krow = lax.broadcasted_iota(jnp.int32, (TK, TQ), 0)
    qcol = lax.broadcasted_iota(jnp.int32, (TK, TQ), 1)
    causal = (krow <= qcol, krow + TK <= qcol)

    for h in range(F_HEADS):
        m_ref[h] = jnp.full((8, TQ), MASK_VALUE, F32)
        acc_ref[h] = jnp.zeros((VAUG_ROWS, TQ), F32)

    def scores(kj, dst_ref, h):
        k0 = pl.multiple_of(kj * TK, TK)
        st = _dot(kaug_ref[h, pl.ds(k0, TK), :], qta[h])
        dst_ref[h, 0:TK, :] = st
        dst_ref[h, TK:TK + 8, :] = jnp.broadcast_to(jnp.max(st, axis=0, keepdims=True), (8, TQ))

    def consume(kj, src_ref, h, diagonal):
        k0 = pl.multiple_of(kj * TK, TK)
        st = src_ref[h, 0:TK, :]
        if diagonal is not None:
            st = jnp.where(causal[diagonal], st, MASK_VALUE)
            st_max = jnp.max(st, axis=0, keepdims=True)
        else:
            st_max = src_ref[h, TK:TK + 1, :]
        m_old = m_ref[h][0:1, :]
        m_new = jnp.maximum(m_old, st_max + fq[h])
        p = jnp.exp2(st - (m_new - fq[h]))
        alpha = jnp.exp2(m_old - m_new)
        pv = _dot(vaug_ref[h, :, pl.ds(k0, TK)], p.astype(BF16))
        acc_ref[h] = alpha * acc_ref[h] + pv
        m_ref[h] = jnp.broadcast_to(m_new, (8, TQ))

    def overlapped(next_j, dst_ref, cur_j, src_ref, diagonal=None):
        lead = 2
        for h in range(lead):
            scores(next_j, dst_ref, h)
        for h in range(F_HEADS):
            consume(cur_j, src_ref, h, diagonal)
            if h + lead < F_HEADS:
                scores(next_j, dst_ref, h + lead)

    for h in range(F_HEADS):
        scores(0, sa_ref, h)

    def body(i, carry):
        j0 = 2 * i
        overlapped(j0 + 1, sb_ref, j0, sa_ref)
        overlapped(j0 + 2, sa_ref, j0 + 1, sb_ref)
        return carry

    lax.fori_loop(0, qi, body, 0)
    overlapped(2 * qi + 1, sb_ref, 2 * qi, sa_ref, diagonal=0)
    for h in range(F_HEADS):
        consume(2 * qi + 1, sb_ref, h, 1)

    for p in range(HEAD_PAIRS):
        halves = []
        for hh in range(2):
            acc = acc_ref[2 * p + hh]
            halves.append(acc[0:HEAD_DIM, :] / acc[HEAD_DIM:HEAD_DIM + 1, :])
        out_t = jnp.concatenate(halves, axis=0)
        cs = slice(p * LANES, (p + 1) * LANES)
        o_ref[:, cs] = (out_t.T * g_ref[:, cs].astype(F32)).astype(BF16)


def _fox(fqt, kaug, vaug, fg, cf_t):
    b, _, s = fqt.shape
    assert TQ == 2 * TK
    row_spec = pl.BlockSpec((None, TQ, F_WIDTH), lambda bi, qi: (bi, qi, 0))
    return pl.pallas_call(
        _fox_body,
        grid=(b, s // TQ),
        in_specs=[pl.BlockSpec((None, F_WIDTH, TQ), lambda bi, qi: (bi, 0, qi)),
                  pl.BlockSpec((None, F_HEADS, s, LANES), lambda bi, qi: (bi, 0, 0, 0)),
                  pl.BlockSpec((None, F_HEADS, VAUG_ROWS, s), lambda bi, qi: (bi, 0, 0, 0)),
                  pl.BlockSpec((None, 8, TQ), lambda bi, qi: (bi, 0, qi)),
                  row_spec],
        out_specs=row_spec,
        out_shape=jax.ShapeDtypeStruct((b, s, F_WIDTH), BF16),
        scratch_shapes=[pltpu.VMEM((F_HEADS, 8, TQ), F32),
                        pltpu.VMEM((F_HEADS, VAUG_ROWS, TQ), F32),
                        pltpu.VMEM((F_HEADS, TK + 8, TQ), F32),
                        pltpu.VMEM((F_HEADS, TK + 8, TQ), F32)],
        compiler_params=pltpu.CompilerParams(
            dimension_semantics=("arbitrary", "arbitrary"), vmem_limit_bytes=VMEM_LIMIT),
        name="fox",
    )(fqt, kaug, vaug, cf_t, fg)


def _outproj_body(a_ref, r_ref, f_ref, w_ref, x_ref, pg_ref, o_ref):
    tm = x_ref.shape[0]

    def project(c):
        rows = slice(c * RC_OUT, (c + 1) * RC_OUT)
        y = jnp.concatenate([a_ref[rows, :], r_ref[rows, :], f_ref[rows, :]], axis=1)
        return _dot(y, w_ref[...])

    nchunks = tm // RC_OUT
    o_next = project(0)
    for c in range(nchunks):
        rows = slice(c * RC_OUT, (c + 1) * RC_OUT)
        o = o_next
        if c + 1 < nchunks:
            o_next = project(c + 1)
        ms = jnp.mean(o * o, axis=-1, keepdims=True)
        o_ref[rows, :] = x_ref[rows, :] + o * lax.rsqrt(ms + EPS) * pg_ref[...]


def _outproj(a, r, f, w, x, post_g):
    b, s, _ = x.shape
    tm = TM_OUT
    row_spec = lambda n: pl.BlockSpec((None, tm, n), lambda bi, si: (bi, si, 0))
    c2 = lambda shp: pl.BlockSpec(shp, lambda bi, si: (0, 0))
    return pl.pallas_call(
        _outproj_body,
        grid=(b, s // tm),
        in_specs=[row_spec(A_WIDTH), row_spec(R_WIDTH), row_spec(F_WIDTH),
                  c2((D_MODEL, D_MODEL)), row_spec(D_MODEL), c2((1, D_MODEL))],
        out_specs=row_spec(D_MODEL),
        out_shape=jax.ShapeDtypeStruct((b, s, D_MODEL), F32),
        compiler_params=pltpu.CompilerParams(
            dimension_semantics=("arbitrary", "arbitrary"), vmem_limit_bytes=VMEM_LIMIT),
        name="outproj",
    )(a, r, f, w, x, post_g)


@functools.lru_cache(maxsize=None)
def _tables(seq):
    half = HEAD_DIM // 2
    lane = np.arange(LANES)
    inv = ROPE_THETA ** (-(np.arange(half, dtype=np.float64) / half))
    ang = np.arange(seq, dtype=np.float64)[:, None] * inv[None, :]
    cos = np.cos(ang)
    sin = np.sin(ang)
    cos_t = cos[:, lane % half]
    sin_t = sin[:, lane % half] * np.where(lane < HEAD_DIM, -1.0, 1.0)[None, :]

    gam = 1.0 - np.exp2(-5.0 - np.arange(R_HEADS, dtype=np.float64))
    log_gam = np.log(gam)
    pos = np.arange(L_RET)
    dist = np.abs(pos[:, None] - pos[None, :])
    allowed = (pos[None, :] // CHUNK) <= (pos[:, None] // CHUNK)
    dmat = np.where(allowed[None], np.exp(log_gam[:, None, None] * dist[None]), 0.0)
    dmat = np.concatenate([dmat[0::2], dmat[1::2]], axis=2)
    head_of_lane = lane // HEAD_DIM
    head_of_qk_lane = (lane // half) % 2
    qdec = np.stack([np.exp(log_gam[2 * p + head_of_lane][None, :] * (pos + 1.0)[:, None])
                     for p in range(HEAD_PAIRS)])
    kdec = np.stack([np.exp(log_gam[2 * p + head_of_qk_lane][None, :] * (L_RET - 1.0 - pos)[:, None])
                     for p in range(HEAD_PAIRS)])
    bmask = (head_of_qk_lane[:, None] == head_of_lane[None, :]).astype(np.float32)
    sdec = np.stack([np.exp(log_gam[2 * p + head_of_qk_lane] * L_RET)[:, None] * bmask
                     for p in range(HEAD_PAIRS)])
    tri = (np.arange(RC_IN)[None, :] <= np.arange(RC_IN)[:, None]).astype(np.float32)
    grp = np.arange(A_WIDTH) // HEAD_DIM
    mavg = (grp[:, None] == grp[None, :]).astype(np.float32) / HEAD_DIM
    place = np.zeros((LANES, F_HEADS * LANES), np.float32)
    for h in range(F_HEADS):
        for term in range(3):
            place[8 * term + h, h * LANES + HEAD_DIM + term] = 1.0
    f = lambda a: np.asarray(a, np.float32)
    return dict(cos=f(cos_t), sin=f(sin_t), dmat=f(dmat), qdec=f(qdec), kdec=f(kdec),
                sdec=f(sdec), bmask=f(bmask), tri=f(tri), mavg=f(mavg),
                mavg128=f(mavg[:LANES, :LANES]), place=f(place))


def _layer(x, pre_g, post_g, w_in, b_f, a_ln_g, a_ws, a_bs, w_out, t):
    w_t = w_in.T
    half = HEAD_DIM // 2
    qk = w_t[A_COLS:A_COLS + 2 * R_WIDTH].reshape(2 * HEAD_PAIRS, 2, 2, half, D_MODEL)
    qk = jnp.swapaxes(qk, 1, 2).reshape(2 * R_WIDTH, D_MODEL)
    w_t = jnp.concatenate([w_t[:A_COLS], qk, w_t[A_COLS + 2 * R_WIDTH:]], axis=0)
    w_pad = jnp.pad(w_t, ((0, D_IN_PAD - D_IN), (0, 0))).astype(BF16)
    bf_pad = jnp.pad(b_f, (0, LANES - F_HEADS)).reshape(1, LANES)
    abias = jnp.repeat(a_bs.T, HEAD_DIM, axis=1)
    (a_out, rq, rk, rv, rg, fqt, kaug, vaug, fg, cf_t) = _inproj(
        x, pre_g.reshape(1, D_MODEL), w_pad, bf_pad, a_ln_g.reshape(1, A_WIDTH), a_ws, abias,
        t["cos"], t["sin"], t["tri"], t["mavg"], t["place"])
    r_out = _retention(rq, rk, rv, rg, t["dmat"], t["qdec"], t["kdec"], t["sdec"], t["bmask"],
                       t["mavg128"])
    f_out = _fox(fqt, kaug, vaug, fg, cf_t)
    return _outproj(a_out, r_out, f_out, w_out.astype(BF16), x, post_g.reshape(1, D_MODEL))


def kernel(x, pre_gain, post_gain, w_in, b_forget, a_norm_gain, a_spatial_w, a_spatial_b, w_out):
    bf16_tables = ("tri", "mavg", "mavg128", "place")
    t = {k: jnp.asarray(v, BF16 if k in bf16_tables else F32) for k, v in _tables(x.shape[1]).items()}
    for l in range(pre_gain.shape[0]):
        x = _layer(x, pre_gain[l], post_gain[l], w_in[l], b_forget[l], a_norm_gain[l],
                   a_spatial_w[l], a_spatial_b[l], w_out[l], t)
    return x
```

```python
import functools
import math

import jax
import jax.numpy as jnp
import numpy as np
from jax import lax
from jax.experimental import pallas as pl
from jax.experimental.pallas import tpu as pltpu

F32 = jnp.float32
BF16 = jnp.bfloat16

D_MODEL = 1024
HEAD_DIM = 64
CHUNK = 64
A_WIDTH = 256
R_WIDTH = 384
F_WIDTH = 384
A_GROUPS = 4
R_HEADS = 6
F_HEADS = 6
A_BLOCK = 128
ROPE_THETA = 10000.0
EPS = 1e-6
LANES = 128
HEAD_PAIRS = R_WIDTH // LANES

A_COLS = 3 * A_WIDTH
R_COLS = 4 * R_WIDTH
F_COLS = 4 * F_WIDTH + F_HEADS
D_IN = A_COLS + R_COLS + F_COLS
D_IN_MAIN = A_COLS + R_COLS + 4 * F_WIDTH

TM_IN = 1024
RC_IN = 256
TM_RET = 1024
L_RET = 128
TQ = 512
TK = 256
TM_OUT = 1024
RC_OUT = 256
VMEM_LIMIT = 56 * 1024 * 1024
VAUG_ROWS = HEAD_DIM + 16
LOG2E = 1.4426950408889634
MASK_VALUE = -1e30


def _silu(x):
    return 0.5 * x * (1.0 + jnp.tanh(0.5 * x))


def _gelu_tanh(x):
    c = math.sqrt(2.0 / math.pi)
    return 0.5 * x * (1.0 + jnp.tanh(c * (x + 0.044715 * (x * x * x))))


def _log_sigmoid(x):
    return jnp.minimum(x, 0.0) - jnp.log1p(jnp.exp(-jnp.abs(x)))


def _dot(a, b):
    return jnp.dot(a, b, preferred_element_type=F32)


def _group_norm64(x, mavg):
    mean = _dot(x.astype(BF16), mavg)
    d = x - mean
    var = _dot((d * d).astype(BF16), mavg)
    return d * lax.rsqrt(var + EPS)


def _inproj_body(x_ref, pg_ref, w_ref, wflg_ref, bf_ref, aln_ref, aws_ref, abias_ref, cos_ref, sin_ref,
                 tri_ref, mavg_ref, place_ref,
                 a_ref, rq_ref, rk_ref, rv_ref, rg_ref, fq_ref, kaug_ref, vaug_ref, fg_ref, cft_ref,
                 carry_ref):
    tm = x_ref.shape[0]

    @pl.when(pl.program_id(1) == 0)
    def _():
        carry_ref[...] = jnp.zeros_like(carry_ref)

    mavg = mavg_ref[...]
    tri = tri_ref[...]
    row = lax.broadcasted_iota(jnp.int32, (A_BLOCK, A_BLOCK), 0)
    col = lax.broadcasted_iota(jnp.int32, (A_BLOCK, A_BLOCK), 1)
    allowed = jnp.logical_not(jnp.logical_and(row < CHUNK, col >= CHUNK))
    wcat = jnp.concatenate(
        [jnp.where(allowed, aws_ref[gi], 0.0).astype(BF16) for gi in range(A_GROUPS)], axis=1)
    lane_grp = lax.shift_right_logical(
        lax.broadcasted_iota(jnp.int32, (A_BLOCK, A_WIDTH), 1), HEAD_DIM.bit_length() - 1)
    zero_bf = jnp.zeros((A_BLOCK, A_WIDTH), BF16)
    lane = lax.broadcasted_iota(jnp.int32, (1, LANES), 1)
    srow = lax.broadcasted_iota(jnp.int32, (VAUG_ROWS - HEAD_DIM, 1), 0)
    ones_rows = jnp.broadcast_to(jnp.where(srow == 0, 1.0, 0.0),
                                 (VAUG_ROWS - HEAD_DIM, RC_IN)).astype(BF16)

    def rope(z, cos_t, sin_t):
        outs = []
        for c in range(HEAD_PAIRS):
            zc = z[:, c * LANES:(c + 1) * LANES]
            outs.append(zc * cos_t + pltpu.roll(zc, HEAD_DIM, 1) * sin_t)
        return jnp.concatenate(outs, axis=1)

    def normed(c):
        x = x_ref[c * RC_IN:(c + 1) * RC_IN, :]
        ms = jnp.mean(x * x, axis=-1, keepdims=True)
        return (x * lax.rsqrt(ms + EPS) * pg_ref[...]).astype(BF16)

    wide_cols = ((0, A_COLS), (A_COLS, A_COLS + R_COLS), (A_COLS + R_COLS, D_IN_MAIN))
    contract_last = (((1,), (1,)), ((), ()))

    def wide(h, k):
        z = lax.dot_general(h, w_ref[wide_cols[k][0]:wide_cols[k][1], :], contract_last,
                            preferred_element_type=F32)
        if k == 2:
            z_flg = lax.dot_general(h, wflg_ref[...], contract_last, preferred_element_type=F32)
            z = jnp.concatenate([z, z_flg], axis=1)
        return z

    def stage1(c, z, e):
        za, zr, zf = z
        e["u"] = _gelu_tanh(za[:, 0:A_WIDTH])
        e["v"] = _gelu_tanh(za[:, A_WIDTH:2 * A_WIDTH])
        e["g"] = _silu(za[:, 2 * A_WIDTH:3 * A_WIDTH])
        e["mean"] = _dot(e["v"].astype(BF16), mavg)
        ls = _log_sigmoid(zf[:, 4 * F_WIDTH:4 * F_WIDTH + LANES] + bf_ref[...])
        ls_hi = ls.astype(BF16)
        ls_lo = (ls - ls_hi.astype(F32)).astype(BF16)
        cf = _dot(tri, ls_hi) + _dot(tri, ls_lo) + carry_ref[...]
        carry_ref[...] = cf[RC_IN - 1:RC_IN, :]
        e["cf2"] = cf * LOG2E

    def stage2(c, z, e):
        rows = slice(c * RC_IN, (c + 1) * RC_IN)
        za, zr, zf = z
        e["d"] = e["v"] - e["mean"]
        e["var"] = _dot((e["d"] * e["d"]).astype(BF16), mavg)
        cf2 = e["cf2"]
        cft_ref[:, rows] = cf2.T[:8, :]
        negf = -cf2
        hi = negf.astype(BF16).astype(F32)
        mid = (negf - hi).astype(BF16).astype(F32)
        lo = (negf - hi - mid).astype(BF16).astype(F32)
        packed = jnp.where(lane < 8, hi,
                           jnp.where(lane < 16, pltpu.roll(mid, 8, 1),
                                     jnp.where(lane < 24, pltpu.roll(lo, 16, 1), 0.0)))
        e["placed"] = _dot(packed.astype(BF16), place_ref[...])
        cos_t = cos_ref[rows, :]
        sin_t = sin_ref[rows, :]
        rq_ref[rows, :] = (rope(zr[:, 0:R_WIDTH], cos_t, sin_t) * (HEAD_DIM ** -0.5)).astype(BF16)
        rk_ref[rows, :] = rope(zr[:, R_WIDTH:2 * R_WIDTH], cos_t, sin_t).astype(BF16)
        rv_ref[rows, :] = zr[:, 2 * R_WIDTH:3 * R_WIDTH].astype(BF16)
        rg_ref[rows, :] = _silu(zr[:, 3 * R_WIDTH:4 * R_WIDTH]).astype(BF16)

    def stage3(c, z, e):
        rows = slice(c * RC_IN, (c + 1) * RC_IN)
        za, zr, zf = z
        vn = (e["d"] * lax.rsqrt(e["var"] + EPS) * aln_ref[...]).astype(BF16)
        mixed_blocks = []
        for nb in range(RC_IN // A_BLOCK):
            vb = vn[nb * A_BLOCK:(nb + 1) * A_BLOCK, :]
            vstack = jnp.concatenate(
                [jnp.where(lane_grp == gi, vb, zero_bf) for gi in range(A_GROUPS)], axis=0)
            mixed_blocks.append(_dot(wcat, vstack) + abias_ref[...])
        mixed = jnp.concatenate(mixed_blocks, axis=0)
        a_ref[rows, :] = (e["u"] * mixed * e["g"]).astype(BF16)
        fq_ref[:, rows] = (zf[:, 0:F_WIDTH] * (HEAD_DIM ** -0.5 * LOG2E)).T.astype(BF16)
        fg_ref[rows, :] = _silu(zf[:, 3 * F_WIDTH:4 * F_WIDTH]).astype(BF16)
        fv_t = zf[:, 2 * F_WIDTH:3 * F_WIDTH].T.astype(BF16)
        for hd in range(F_HEADS):
            p, hh = divmod(hd, 2)
            kk = zf[:, F_WIDTH + p * LANES:F_WIDTH + (p + 1) * LANES]
            if hh == 1:
                kk = pltpu.roll(kk, HEAD_DIM, 1)
            aug = e["placed"][:, hd * LANES:(hd + 1) * LANES]
            kaug_ref[hd, rows, :] = jnp.where(lane < HEAD_DIM, kk, aug).astype(BF16)
            vaug_ref[hd, 0:HEAD_DIM, rows] = fv_t[hd * HEAD_DIM:(hd + 1) * HEAD_DIM, :]
            vaug_ref[hd, HEAD_DIM:VAUG_ROWS, rows] = ones_rows

    stages = (stage1, stage2, stage3)
    nchunks = tm // RC_IN
    h0 = normed(0)
    z_next = tuple(wide(h0, k) for k in range(3))
    for c in range(nchunks):
        z, e = z_next, {}
        if c + 1 < nchunks:
            h_next = normed(c + 1)
            z_new = []
            for k in range(3):
                z_new.append(wide(h_next, k))
                stages[k](c, z, e)
            z_next = tuple(z_new)
        else:
            for k in range(3):
                stages[k](c, z, e)


def _inproj(x, pre_g, w_main, w_flg, bf_pad, aln, aws, abias, cos_t, sin_t, tri, mavg, place):
    b, s, _ = x.shape
    tm = TM_IN
    grid = (b, s // tm)
    row_spec = lambda n: pl.BlockSpec((None, tm, n), lambda bi, si: (bi, si, 0))
    col_spec = lambda n: pl.BlockSpec((None, n, tm), lambda bi, si: (bi, 0, si))
    const2 = lambda shp: pl.BlockSpec(shp, lambda bi, si: (0, 0))
    bf_sds = lambda n: jax.ShapeDtypeStruct((b, s, n), BF16)
    out_shape = ([bf_sds(A_WIDTH)] + [bf_sds(R_WIDTH)] * 4
                 + [jax.ShapeDtypeStruct((b, F_WIDTH, s), BF16),
                    jax.ShapeDtypeStruct((b, F_HEADS, s, LANES), BF16),
                    jax.ShapeDtypeStruct((b, F_HEADS, VAUG_ROWS, s), BF16),
                    bf_sds(F_WIDTH),
                    jax.ShapeDtypeStruct((b, 8, s), F32)])
    out_specs = ([row_spec(A_WIDTH)] + [row_spec(R_WIDTH)] * 4
                 + [col_spec(F_WIDTH),
                    pl.BlockSpec((None, F_HEADS, tm, LANES), lambda bi, si: (bi, 0, si, 0)),
                    pl.BlockSpec((None, F_HEADS, VAUG_ROWS, tm), lambda bi, si: (bi, 0, 0, si)),
                    row_spec(F_WIDTH),
                    col_spec(8)])
    in_specs = [
        row_spec(D_MODEL),
        const2((1, D_MODEL)),
        pl.BlockSpec((D_IN_MAIN, D_MODEL), lambda bi, si: (0, 0), pipeline_mode=pl.Buffered(1)),
        const2((LANES, D_MODEL)),
        const2((1, LANES)),
        const2((1, A_WIDTH)),
        pl.BlockSpec((A_GROUPS, A_BLOCK, A_BLOCK), lambda bi, si: (0, 0, 0)),
        const2((A_BLOCK, A_WIDTH)),
        pl.BlockSpec((tm, LANES), lambda bi, si: (si, 0)),
        pl.BlockSpec((tm, LANES), lambda bi, si: (si, 0)),
        const2((RC_IN, RC_IN)),
        const2((A_WIDTH, A_WIDTH)),
        const2((LANES, F_HEADS * LANES)),
    ]
    return pl.pallas_call(
        _inproj_body,
        grid=grid,
        in_specs=in_specs,
        out_specs=out_specs,
        out_shape=out_shape,
        scratch_shapes=[pltpu.VMEM((1, LANES), F32)],
        compiler_params=pltpu.CompilerParams(
            dimension_semantics=("arbitrary", "arbitrary"), vmem_limit_bytes=VMEM_LIMIT),
        name="inproj",
    )(x, pre_g, w_main, w_flg, bf_pad, aln, aws, abias, cos_t, sin_t, tri, mavg, place)


def _retention_body(q_ref, k_ref, v_ref, g_ref, dmat_ref, qdec_ref, kdec_ref, sdec_ref, bmask_ref,
                    mavg_ref, o_ref, state_ref):
    tm = q_ref.shape[0]

    @pl.when(pl.program_id(1) == 0)
    def _():
        state_ref[...] = jnp.zeros_like(state_ref)

    lane = lax.broadcasted_iota(jnp.int32, (1, LANES), 1)
    head0_v = lane < HEAD_DIM
    head0_qk = jnp.bitwise_and(lane, HEAD_DIM // 2) == 0
    mavg = mavg_ref[...]
    bmask = bmask_ref[...]
    contract_last = (((1,), (1,)), ((), ()))
    contract_first = (((0,), (0,)), ((), ()))
    nblk = tm // L_RET
    units = [(p, j) for p in range(HEAD_PAIRS) for j in range(nblk)]

    def split_heads(x2, head0):
        zero = jnp.zeros_like(x2)
        return jnp.concatenate([jnp.where(head0, x2, zero), jnp.where(head0, zero, x2)], axis=0)

    q2, v2, sc, kv = {}, {}, {}, {}
    for (p, j) in units:
        rs, cs = slice(j * L_RET, (j + 1) * L_RET), slice(p * LANES, (p + 1) * LANES)
        q2[p, j] = q_ref[rs, cs]
        k2 = k_ref[rs, cs]
        v2[p, j] = v_ref[rs, cs]
        sc[p, j] = lax.dot_general(q2[p, j], split_heads(k2, head0_qk), contract_last,
                                   preferred_element_type=F32)
        kd = (k2.astype(F32) * kdec_ref[p]).astype(BF16)
        kv[p, j] = lax.dot_general(kd, v2[p, j], contract_first, preferred_element_type=F32)

    st = {}
    for p in range(HEAD_PAIRS):
        state = state_ref[p]
        for j in range(nblk):
            st[p, j] = state.astype(BF16)
            state = state * sdec_ref[p] + kv[p, j] * bmask
        state_ref[p] = state

    out = {}
    for (p, j) in units:
        pm = (sc[p, j] * dmat_ref[p]).astype(BF16)
        out[p, j] = (_dot(pm, split_heads(v2[p, j], head0_v))
                     + _dot(q2[p, j], st[p, j]) * qdec_ref[p])

    tiles = [jnp.concatenate([out[p, j] for j in range(nblk)], axis=0) for p in range(HEAD_PAIRS)]
    means = [_dot(t.astype(BF16), mavg) for t in tiles]
    devs = [t - mu for t, mu in zip(tiles, means)]
    variances = [_dot((d * d).astype(BF16), mavg) for d in devs]
    for p in range(HEAD_PAIRS):
        cs = slice(p * LANES, (p + 1) * LANES)
        y = devs[p] * lax.rsqrt(variances[p] + EPS) * g_ref[:, cs].astype(F32)
        o_ref[:, cs] = y.astype(BF16)


def _retention(rq, rk, rv, rg, dmat, qdec, kdec, sdec, bmask, mavg128):
    b, s, _ = rq.shape
    tm = TM_RET
    row_spec = pl.BlockSpec((None, tm, R_WIDTH), lambda bi, si: (bi, si, 0))
    c3 = lambda shp: pl.BlockSpec(shp, lambda bi, si: (0, 0, 0))
    c2 = lambda shp: pl.BlockSpec(shp, lambda bi, si: (0, 0))
    return pl.pallas_call(
        _retention_body,
        grid=(b, s // tm),
        in_specs=[row_spec, row_spec, row_spec, row_spec,
                  c3((HEAD_PAIRS, L_RET, 2 * L_RET)), c3((HEAD_PAIRS, L_RET, LANES)),
                  c3((HEAD_PAIRS, L_RET, LANES)), c3((HEAD_PAIRS, LANES, LANES)),
                  c2((LANES, LANES)), c2((LANES, LANES))],
        out_specs=row_spec,
        out_shape=jax.ShapeDtypeStruct((b, s, R_WIDTH), BF16),
        scratch_shapes=[pltpu.VMEM((HEAD_PAIRS, LANES, LANES), F32)],
        compiler_params=pltpu.CompilerParams(
            dimension_semantics=("arbitrary", "arbitrary"), vmem_limit_bytes=VMEM_LIMIT),
        name="retention",
    )(rq, rk, rv, rg, dmat, qdec, kdec, sdec, bmask, mavg128)


def _fox_body(qt_ref, kaug_ref, vaug_ref, cft_ref, g_ref, o_ref, m_ref, acc_ref, sa_ref, sb_ref):
    qi = pl.program_id(1)
    srow_q = lax.broadcasted_iota(jnp.int32, (HEAD_DIM, 1), 0)
    q_ones = jnp.broadcast_to(jnp.where(srow_q < 3, 1.0, 0.0), (HEAD_DIM, TQ)).astype(BF16)
    qta = [jnp.concatenate([qt_ref[h * HEAD_DIM:(h + 1) * HEAD_DIM, :], q_ones], axis=0)
           for h in range(F_HEADS)]
    fq = [cft_ref[h:h + 1, :] for h in range(F_HEADS)]
    krow = lax.broadcasted_iota(jnp.int32, (TK, TQ), 0)
    qcol = lax.broadcasted_iota(jnp.int32, (TK, TQ), 1)
    causal = (krow <= qcol, krow + TK <= qcol)

    for h in range(F_HEADS):
        m_ref[h] = jnp.full((8, TQ), MASK_VALUE, F32)
        acc_ref[h] = jnp.zeros((VAUG_ROWS, TQ), F32)

    def scores(kj, dst_ref, h, last=False):
        k0 = pl.multiple_of(kj * TK, TK)
        kt = kaug_ref[h, pl.ds(k0, TK), :]
        if last:
            dst_ref[h, :, TK:TQ] = _dot(kt, qta[h][:, TK:TQ])
        else:
            dst_ref[h] = _dot(kt, qta[h])

    def consume(kj, src_ref, h, diagonal):
        k0 = pl.multiple_of(kj * TK, TK)
        st = src_ref[h]
        if diagonal:
            st = jnp.where(causal[0], st, MASK_VALUE)
        m_old = m_ref[h][0:1, :]
        m_new = jnp.maximum(m_old, jnp.max(st, axis=0, keepdims=True) + fq[h])
        p = jnp.exp2(st - (m_new - fq[h]))
        alpha = jnp.exp2(m_old - m_new)
        pv = _dot(vaug_ref[h, :, pl.ds(k0, TK)], p.astype(BF16))
        acc_ref[h] = alpha * acc_ref[h] + pv
        m_ref[h] = jnp.broadcast_to(m_new, (8, TQ))

    def consume_last(kj, src_ref, h):
        k0 = pl.multiple_of(kj * TK, TK)
        st = jnp.where(causal[1][:, TK:TQ], src_ref[h, :, TK:TQ], MASK_VALUE)
        m_old = m_ref[h][0:1, TK:TQ]
        fq_h = fq[h][:, TK:TQ]
        m_new = jnp.maximum(m_old, jnp.max(st, axis=0, keepdims=True) + fq_h)
        p = jnp.exp2(st - (m_new - fq_h))
        alpha = jnp.exp2(m_old - m_new)
        pv = _dot(vaug_ref[h, :, pl.ds(k0, TK)], p.astype(BF16))
        acc_ref[h, :, TK:TQ] = alpha * acc_ref[h, :, TK:TQ] + pv

    def overlapped(next_j, dst_ref, cur_j, src_ref, diagonal=False, next_last=False):
        lead = 2
        for h in range(lead):
            scores(next_j, dst_ref, h, next_last)
        for h in range(F_HEADS):
            consume(cur_j, src_ref, h, diagonal)
            if h + lead < F_HEADS:
                scores(next_j, dst_ref, h + lead, next_last)

    for h in range(F_HEADS):
        scores(0, sa_ref, h)

    def body(i, carry):
        j0 = 2 * i
        overlapped(j0 + 1, sb_ref, j0, sa_ref)
        overlapped(j0 + 2, sa_ref, j0 + 1, sb_ref)
        return carry

    lax.fori_loop(0, qi, body, 0)
    overlapped(2 * qi + 1, sb_ref, 2 * qi, sa_ref, diagonal=True, next_last=True)
    for h in range(F_HEADS):
        consume_last(2 * qi + 1, sb_ref, h)

    for p in range(HEAD_PAIRS):
        halves = []
        for hh in range(2):
            acc = acc_ref[2 * p + hh]
            halves.append(acc[0:HEAD_DIM, :] / acc[HEAD_DIM:HEAD_DIM + 1, :])
        out_t = jnp.concatenate(halves, axis=0)
        cs = slice(p * LANES, (p + 1) * LANES)
        o_ref[:, cs] = (out_t.T * g_ref[:, cs].astype(F32)).astype(BF16)


def _fox(fqt, kaug, vaug, fg, cf_t):
    b, _, s = fqt.shape
    assert TQ == 2 * TK
    row_spec = pl.BlockSpec((None, TQ, F_WIDTH), lambda bi, qi: (bi, qi, 0))
    return pl.pallas_call(
        _fox_body,
        grid=(b, s // TQ),
        in_specs=[pl.BlockSpec((None, F_WIDTH, TQ), lambda bi, qi: (bi, 0, qi)),
                  pl.BlockSpec((None, F_HEADS, s, LANES), lambda bi, qi: (bi, 0, 0, 0)),
                  pl.BlockSpec((None, F_HEADS, VAUG_ROWS, s), lambda bi, qi: (bi, 0, 0, 0)),
                  pl.BlockSpec((None, 8, TQ), lambda bi, qi: (bi, 0, qi)),
                  row_spec],
        out_specs=row_spec,
        out_shape=jax.ShapeDtypeStruct((b, s, F_WIDTH), BF16),
        scratch_shapes=[pltpu.VMEM((F_HEADS, 8, TQ), F32),
                        pltpu.VMEM((F_HEADS, VAUG_ROWS, TQ), F32),
                        pltpu.VMEM((F_HEADS, TK, TQ), F32),
                        pltpu.VMEM((F_HEADS, TK, TQ), F32)],
        compiler_params=pltpu.CompilerParams(
            dimension_semantics=("arbitrary", "arbitrary"), vmem_limit_bytes=VMEM_LIMIT),
        name="fox",
    )(fqt, kaug, vaug, cf_t, fg)


def _outproj_body(a_ref, r_ref, f_ref, w_ref, x_ref, pg_ref, o_ref):
    tm = x_ref.shape[0]

    def project(c):
        rows = slice(c * RC_OUT, (c + 1) * RC_OUT)
        y = jnp.concatenate([a_ref[rows, :], r_ref[rows, :], f_ref[rows, :]], axis=1)
        return _dot(y, w_ref[...])

    nchunks = tm // RC_OUT
    o_next = project(0)
    for c in range(nchunks):
        rows = slice(c * RC_OUT, (c + 1) * RC_OUT)
        o = o_next
        if c + 1 < nchunks:
            o_next = project(c + 1)
        ms = jnp.mean(o * o, axis=-1, keepdims=True)
        o_ref[rows, :] = x_ref[rows, :] + o * lax.rsqrt(ms + EPS) * pg_ref[...]


def _outproj(a, r, f, w, x, post_g):
    b, s, _ = x.shape
    tm = TM_OUT
    row_spec = lambda n: pl.BlockSpec((None, tm, n), lambda bi, si: (bi, si, 0))
    c2 = lambda shp: pl.BlockSpec(shp, lambda bi, si: (0, 0))
    return pl.pallas_call(
        _outproj_body,
        grid=(b, s // tm),
        in_specs=[row_spec(A_WIDTH), row_spec(R_WIDTH), row_spec(F_WIDTH),
                  c2((D_MODEL, D_MODEL)), row_spec(D_MODEL), c2((1, D_MODEL))],
        out_specs=row_spec(D_MODEL),
        out_shape=jax.ShapeDtypeStruct((b, s, D_MODEL), F32),
        compiler_params=pltpu.CompilerParams(
            dimension_semantics=("arbitrary", "arbitrary"), vmem_limit_bytes=VMEM_LIMIT),
        name="outproj",
    )(a, r, f, w, x, post_g)


@functools.lru_cache(maxsize=None)
def _tables(seq):
    half = HEAD_DIM // 2
    lane = np.arange(LANES)
    inv = ROPE_THETA ** (-(np.arange(half, dtype=np.float64) / half))
    ang = np.arange(seq, dtype=np.float64)[:, None] * inv[None, :]
    cos = np.cos(ang)
    sin = np.sin(ang)
    cos_t = cos[:, lane % half]
    sin_t = sin[:, lane % half] * np.where(lane < HEAD_DIM, -1.0, 1.0)[None, :]

    gam = 1.0 - np.exp2(-5.0 - np.arange(R_HEADS, dtype=np.float64))
    log_gam = np.log(gam)
    pos = np.arange(L_RET)
    dist = np.abs(pos[:, None] - pos[None, :])
    allowed = (pos[None, :] // CHUNK) <= (pos[:, None] // CHUNK)
    dmat = np.where(allowed[None], np.exp(log_gam[:, None, None] * dist[None]), 0.0)
    dmat = np.concatenate([dmat[0::2], dmat[1::2]], axis=2)
    head_of_lane = lane // HEAD_DIM
    head_of_qk_lane = (lane // half) % 2
    qdec = np.stack([np.exp(log_gam[2 * p + head_of_lane][None, :] * (pos + 1.0)[:, None])
                     for p in range(HEAD_PAIRS)])
    kdec = np.stack([np.exp(log_gam[2 * p + head_of_qk_lane][None, :] * (L_RET - 1.0 - pos)[:, None])
                     for p in range(HEAD_PAIRS)])
    bmask = (head_of_qk_lane[:, None] == head_of_lane[None, :]).astype(np.float32)
    sdec = np.stack([np.exp(log_gam[2 * p + head_of_qk_lane] * L_RET)[:, None] * bmask
                     for p in range(HEAD_PAIRS)])
    tri = (np.arange(RC_IN)[None, :] <= np.arange(RC_IN)[:, None]).astype(np.float32)
    grp = np.arange(A_WIDTH) // HEAD_DIM
    mavg = (grp[:, None] == grp[None, :]).astype(np.float32) / HEAD_DIM
    place = np.zeros((LANES, F_HEADS * LANES), np.float32)
    for h in range(F_HEADS):
        for term in range(3):
            place[8 * term + h, h * LANES + HEAD_DIM + term] = 1.0
    f = lambda a: np.asarray(a, np.float32)
    return dict(cos=f(cos_t), sin=f(sin_t), dmat=f(dmat), qdec=f(qdec), kdec=f(kdec),
                sdec=f(sdec), bmask=f(bmask), tri=f(tri), mavg=f(mavg),
                mavg128=f(mavg[:LANES, :LANES]), place=f(place))


def _layer(x, pre_g, post_g, w_in, b_f, a_ln_g, a_ws, a_bs, w_out, t):
    half = HEAD_DIM // 2
    pieces = [w_in[:, :A_COLS]]
    for tp in range(2 * HEAD_PAIRS):
        for hf in range(2):
            for hd in range(2):
                c0 = A_COLS + tp * LANES + hd * HEAD_DIM + hf * half
                pieces.append(w_in[:, c0:c0 + half])
    pieces.append(w_in[:, A_COLS + 2 * R_WIDTH:D_IN_MAIN])
    w_main = jnp.concatenate(pieces, axis=1).T.astype(BF16)
    w_flg = jnp.pad(w_in[:, D_IN_MAIN:].T, ((0, LANES - F_HEADS), (0, 0))).astype(BF16)
    bf_pad = jnp.pad(b_f, (0, LANES - F_HEADS)).reshape(1, LANES)
    abias = jnp.repeat(a_bs.T, HEAD_DIM, axis=1)
    (a_out, rq, rk, rv, rg, fqt, kaug, vaug, fg, cf_t) = _inproj(
        x, pre_g.reshape(1, D_MODEL), w_main, w_flg, bf_pad, a_ln_g.reshape(1, A_WIDTH), a_ws, abias,
        t["cos"], t["sin"], t["tri"], t["mavg"], t["place"])
    r_out = _retention(rq, rk, rv, rg, t["dmat"], t["qdec"], t["kdec"], t["sdec"], t["bmask"],
                       t["mavg128"])
    f_out = _fox(fqt, kaug, vaug, fg, cf_t)
    return _outproj(a_out, r_out, f_out, w_out.astype(BF16), x, post_g.reshape(1, D_MODEL))


def kernel(x, pre_gain, post_gain, w_in, b_forget, a_norm_gain, a_spatial_w, a_spatial_b, w_out):
    bf16_tables = ("tri", "mavg", "mavg128", "place")
    t = {k: jnp.asarray(v, BF16 if k in bf16_tables else F32) for k, v in _tables(x.shape[1]).items()}
    for l in range(pre_gain.shape[0]):
        x = _layer(x, pre_gain[l], post_gain[l], w_in[l], b_forget[l], a_norm_gain[l],
                   a_spatial_w[l], a_spatial_b[l], w_out[l], t)
    return x
```

```python
import functools
import math

import jax
import jax.numpy as jnp
import numpy as np
from jax import lax
from jax.experimental import pallas as pl
from jax.experimental.pallas import tpu as pltpu

F32 = jnp.float32
BF16 = jnp.bfloat16

D_MODEL = 1024
HEAD_DIM = 64
CHUNK = 64
A_WIDTH = 256
R_WIDTH = 384
F_WIDTH = 384
A_GROUPS = 4
R_HEADS = 6
F_HEADS = 6
A_BLOCK = 128
ROPE_THETA = 10000.0
EPS = 1e-6
LANES = 128
HEAD_PAIRS = R_WIDTH // LANES

A_COLS = 3 * A_WIDTH
R_COLS = 4 * R_WIDTH
F_COLS = 4 * F_WIDTH + F_HEADS
D_IN = A_COLS + R_COLS + F_COLS
D_IN_MAIN = A_COLS + R_COLS + 4 * F_WIDTH

TM_IN = 1024
RC_IN = 256
TM_RET = 2048
L_RET = 128
TQ = 512
TK = 256
TM_OUT = 2048
RC_OUT = 256
VMEM_LIMIT = 56 * 1024 * 1024
VAUG_ROWS = HEAD_DIM + 16
LOG2E = 1.4426950408889634
MASK_VALUE = -1e30


def _silu(x):
    return 0.5 * x * (1.0 + jnp.tanh(0.5 * x))


def _gelu_tanh(x):
    c = math.sqrt(2.0 / math.pi)
    return 0.5 * x * (1.0 + jnp.tanh(c * (x + 0.044715 * (x * x * x))))


def _log_sigmoid(x):
    return jnp.minimum(x, 0.0) - jnp.log1p(jnp.exp(-jnp.abs(x)))


def _dot(a, b):
    return jnp.dot(a, b, preferred_element_type=F32)


def _group_norm64(x, mavg):
    mean = _dot(x.astype(BF16), mavg)
    d = x - mean
    var = _dot((d * d).astype(BF16), mavg)
    return d * lax.rsqrt(var + EPS)


def _inproj_body(x_ref, pg_ref, w_ref, wflg_ref, bf_ref, aln_ref, aws_ref, abias_ref, cos_ref, sin_ref,
                 tri_ref, mavg_ref, place_ref,
                 a_ref, rq_ref, rk_ref, rv_ref, rg_ref, fq_ref, kaug_ref, vaug_ref, fg_ref, cft_ref,
                 carry_ref):
    tm = x_ref.shape[0]

    @pl.when(pl.program_id(1) == 0)
    def _():
        carry_ref[...] = jnp.zeros_like(carry_ref)

    mavg = mavg_ref[...]
    tri = tri_ref[...]
    row = lax.broadcasted_iota(jnp.int32, (A_BLOCK, A_BLOCK), 0)
    col = lax.broadcasted_iota(jnp.int32, (A_BLOCK, A_BLOCK), 1)
    allowed = jnp.logical_not(jnp.logical_and(row < CHUNK, col >= CHUNK))
    wcat = jnp.concatenate(
        [jnp.where(allowed, aws_ref[gi], 0.0).astype(BF16) for gi in range(A_GROUPS)], axis=1)
    lane_grp = lax.shift_right_logical(
        lax.broadcasted_iota(jnp.int32, (A_BLOCK, A_WIDTH), 1), HEAD_DIM.bit_length() - 1)
    zero_bf = jnp.zeros((A_BLOCK, A_WIDTH), BF16)
    lane = lax.broadcasted_iota(jnp.int32, (1, LANES), 1)
    srow = lax.broadcasted_iota(jnp.int32, (VAUG_ROWS - HEAD_DIM, 1), 0)
    ones_rows = jnp.broadcast_to(jnp.where(srow == 0, 1.0, 0.0),
                                 (VAUG_ROWS - HEAD_DIM, RC_IN)).astype(BF16)

    def rope(z, cos_t, sin_t):
        outs = []
        for c in range(HEAD_PAIRS):
            zc = z[:, c * LANES:(c + 1) * LANES]
            outs.append(zc * cos_t + pltpu.roll(zc, HEAD_DIM, 1) * sin_t)
        return jnp.concatenate(outs, axis=1)

    def normed(c):
        x = x_ref[c * RC_IN:(c + 1) * RC_IN, :]
        ms = jnp.mean(x * x, axis=-1, keepdims=True)
        return (x * lax.rsqrt(ms + EPS) * pg_ref[...]).astype(BF16)

    wide_cols = ((0, A_COLS), (A_COLS, A_COLS + R_COLS), (A_COLS + R_COLS, D_IN_MAIN))
    contract_last = (((1,), (1,)), ((), ()))

    def wide(h, k):
        z = lax.dot_general(h, w_ref[wide_cols[k][0]:wide_cols[k][1], :], contract_last,
                            preferred_element_type=F32)
        if k == 2:
            z_flg = lax.dot_general(h, wflg_ref[...], contract_last, preferred_element_type=F32)
            z = jnp.concatenate([z, z_flg], axis=1)
        return z

    def stage1(c, z, e):
        za, zr, zf = z
        e["u"] = _gelu_tanh(za[:, 0:A_WIDTH])
        e["v"] = _gelu_tanh(za[:, A_WIDTH:2 * A_WIDTH])
        e["g"] = _silu(za[:, 2 * A_WIDTH:3 * A_WIDTH])
        e["mean"] = _dot(e["v"].astype(BF16), mavg)
        ls = _log_sigmoid(zf[:, 4 * F_WIDTH:4 * F_WIDTH + LANES] + bf_ref[...])
        ls_hi = ls.astype(BF16)
        ls_lo = (ls - ls_hi.astype(F32)).astype(BF16)
        cf = _dot(tri, ls_hi) + _dot(tri, ls_lo) + carry_ref[...]
        carry_ref[...] = cf[RC_IN - 1:RC_IN, :]
        e["cf2"] = cf * LOG2E

    def stage2(c, z, e):
        rows = slice(c * RC_IN, (c + 1) * RC_IN)
        za, zr, zf = z
        e["d"] = e["v"] - e["mean"]
        e["var"] = _dot((e["d"] * e["d"]).astype(BF16), mavg)
        cf2 = e["cf2"]
        cft_ref[:, rows] = cf2.T[:8, :]
        negf = -cf2
        hi = negf.astype(BF16).astype(F32)
        mid = (negf - hi).astype(BF16).astype(F32)
        lo = (negf - hi - mid).astype(BF16).astype(F32)
        packed = jnp.where(lane < 8, hi,
                           jnp.where(lane < 16, pltpu.roll(mid, 8, 1),
                                     jnp.where(lane < 24, pltpu.roll(lo, 16, 1), 0.0)))
        e["placed"] = _dot(packed.astype(BF16), place_ref[...])
        cos_t = cos_ref[rows, :]
        sin_t = sin_ref[rows, :]
        rq_ref[rows, :] = (rope(zr[:, 0:R_WIDTH], cos_t, sin_t) * (HEAD_DIM ** -0.5)).astype(BF16)
        rk_ref[rows, :] = rope(zr[:, R_WIDTH:2 * R_WIDTH], cos_t, sin_t).astype(BF16)
        rv_ref[rows, :] = zr[:, 2 * R_WIDTH:3 * R_WIDTH].astype(BF16)
        rg_ref[rows, :] = _silu(zr[:, 3 * R_WIDTH:4 * R_WIDTH]).astype(BF16)

    def stage3(c, z, e):
        rows = slice(c * RC_IN, (c + 1) * RC_IN)
        za, zr, zf = z
        vn = (e["d"] * lax.rsqrt(e["var"] + EPS) * aln_ref[...]).astype(BF16)
        mixed_blocks = []
        for nb in range(RC_IN // A_BLOCK):
            vb = vn[nb * A_BLOCK:(nb + 1) * A_BLOCK, :]
            vstack = jnp.concatenate(
                [jnp.where(lane_grp == gi, vb, zero_bf) for gi in range(A_GROUPS)], axis=0)
            mixed_blocks.append(_dot(wcat, vstack) + abias_ref[...])
        mixed = jnp.concatenate(mixed_blocks, axis=0)
        a_ref[rows, :] = (e["u"] * mixed * e["g"]).astype(BF16)
        fq_ref[:, rows] = (zf[:, 0:F_WIDTH] * (HEAD_DIM ** -0.5 * LOG2E)).T.astype(BF16)
        fg_ref[rows, :] = _silu(zf[:, 3 * F_WIDTH:4 * F_WIDTH]).astype(BF16)
        fv_t = zf[:, 2 * F_WIDTH:3 * F_WIDTH].T.astype(BF16)
        for hd in range(F_HEADS):
            p, hh = divmod(hd, 2)
            kk = zf[:, F_WIDTH + p * LANES:F_WIDTH + (p + 1) * LANES]
            if hh == 1:
                kk = pltpu.roll(kk, HEAD_DIM, 1)
            aug = e["placed"][:, hd * LANES:(hd + 1) * LANES]
            kaug_ref[hd, rows, :] = jnp.where(lane < HEAD_DIM, kk, aug).astype(BF16)
            vaug_ref[hd, 0:HEAD_DIM, rows] = fv_t[hd * HEAD_DIM:(hd + 1) * HEAD_DIM, :]
            vaug_ref[hd, HEAD_DIM:VAUG_ROWS, rows] = ones_rows

    stages = (stage1, stage2, stage3)
    nchunks = tm // RC_IN
    h0 = normed(0)
    z_next = tuple(wide(h0, k) for k in range(3))
    for c in range(nchunks):
        z, e = z_next, {}
        if c + 1 < nchunks:
            h_next = normed(c + 1)
            z_new = []
            for k in range(3):
                z_new.append(wide(h_next, k))
                stages[k](c, z, e)
            z_next = tuple(z_new)
        else:
            for k in range(3):
                stages[k](c, z, e)


def _inproj(x, pre_g, w_main, w_flg, bf_pad, aln, aws, abias, cos_t, sin_t, tri, mavg, place):
    b, s, _ = x.shape
    tm = TM_IN
    grid = (b, s // tm)
    row_spec = lambda n: pl.BlockSpec((None, tm, n), lambda bi, si: (bi, si, 0))
    col_spec = lambda n: pl.BlockSpec((None, n, tm), lambda bi, si: (bi, 0, si))
    const2 = lambda shp: pl.BlockSpec(shp, lambda bi, si: (0, 0))
    bf_sds = lambda n: jax.ShapeDtypeStruct((b, s, n), BF16)
    out_shape = ([bf_sds(A_WIDTH)] + [bf_sds(R_WIDTH)] * 4
                 + [jax.ShapeDtypeStruct((b, F_WIDTH, s), BF16),
                    jax.ShapeDtypeStruct((b, F_HEADS, s, LANES), BF16),
                    jax.ShapeDtypeStruct((b, F_HEADS, VAUG_ROWS, s), BF16),
                    bf_sds(F_WIDTH),
                    jax.ShapeDtypeStruct((b, 8, s), F32)])
    out_specs = ([row_spec(A_WIDTH)] + [row_spec(R_WIDTH)] * 4
                 + [col_spec(F_WIDTH),
                    pl.BlockSpec((None, F_HEADS, tm, LANES), lambda bi, si: (bi, 0, si, 0)),
                    pl.BlockSpec((None, F_HEADS, VAUG_ROWS, tm), lambda bi, si: (bi, 0, 0, si)),
                    row_spec(F_WIDTH),
                    col_spec(8)])
    in_specs = [
        row_spec(D_MODEL),
        const2((1, D_MODEL)),
        pl.BlockSpec((D_IN_MAIN, D_MODEL), lambda bi, si: (0, 0), pipeline_mode=pl.Buffered(1)),
        const2((LANES, D_MODEL)),
        const2((1, LANES)),
        const2((1, A_WIDTH)),
        pl.BlockSpec((A_GROUPS, A_BLOCK, A_BLOCK), lambda bi, si: (0, 0, 0)),
        const2((A_BLOCK, A_WIDTH)),
        pl.BlockSpec((tm, LANES), lambda bi, si: (si, 0)),
        pl.BlockSpec((tm, LANES), lambda bi, si: (si, 0)),
        const2((RC_IN, RC_IN)),
        const2((A_WIDTH, A_WIDTH)),
        const2((LANES, F_HEADS * LANES)),
    ]
    return pl.pallas_call(
        _inproj_body,
        grid=grid,
        in_specs=in_specs,
        out_specs=out_specs,
        out_shape=out_shape,
        scratch_shapes=[pltpu.VMEM((1, LANES), F32)],
        compiler_params=pltpu.CompilerParams(
            dimension_semantics=("arbitrary", "arbitrary"), vmem_limit_bytes=VMEM_LIMIT),
        name="inproj",
    )(x, pre_g, w_main, w_flg, bf_pad, aln, aws, abias, cos_t, sin_t, tri, mavg, place)


def _retention_body(q_ref, k_ref, v_ref, g_ref, dmat_ref, qdec_ref, kdec_ref, sdec_ref, bmask_ref,
                    mavg_ref, o_ref, state_ref):
    tm = q_ref.shape[0]

    @pl.when(pl.program_id(1) == 0)
    def _():
        state_ref[...] = jnp.zeros_like(state_ref)

    lane = lax.broadcasted_iota(jnp.int32, (1, LANES), 1)
    head0_v = lane < HEAD_DIM
    head0_qk = jnp.bitwise_and(lane, HEAD_DIM // 2) == 0
    mavg = mavg_ref[...]
    bmask = bmask_ref[...]
    contract_last = (((1,), (1,)), ((), ()))
    contract_first = (((0,), (0,)), ((), ()))
    nblk = tm // L_RET
    units = [(p, j) for p in range(HEAD_PAIRS) for j in range(nblk)]

    def split_heads(x2, head0):
        zero = jnp.zeros_like(x2)
        return jnp.concatenate([jnp.where(head0, x2, zero), jnp.where(head0, zero, x2)], axis=0)

    q2, v2, sc, kv = {}, {}, {}, {}
    for (p, j) in units:
        rs, cs = slice(j * L_RET, (j + 1) * L_RET), slice(p * LANES, (p + 1) * LANES)
        q2[p, j] = q_ref[rs, cs]
        k2 = k_ref[rs, cs]
        v2[p, j] = v_ref[rs, cs]
        sc[p, j] = lax.dot_general(q2[p, j], split_heads(k2, head0_qk), contract_last,
                                   preferred_element_type=F32)
        kd = (k2.astype(F32) * kdec_ref[p]).astype(BF16)
        kv[p, j] = lax.dot_general(kd, v2[p, j], contract_first, preferred_element_type=F32)

    st = {}
    for p in range(HEAD_PAIRS):
        state = state_ref[p]
        for j in range(nblk):
            st[p, j] = state.astype(BF16)
            state = state * sdec_ref[p] + kv[p, j] * bmask
        state_ref[p] = state

    out = {}
    for (p, j) in units:
        pm = (sc[p, j] * dmat_ref[p]).astype(BF16)
        out[p, j] = (_dot(pm, split_heads(v2[p, j], head0_v))
                     + _dot(q2[p, j], st[p, j]) * qdec_ref[p])

    tiles = [jnp.concatenate([out[p, j] for j in range(nblk)], axis=0) for p in range(HEAD_PAIRS)]
    means = [_dot(t.astype(BF16), mavg) for t in tiles]
    devs = [t - mu for t, mu in zip(tiles, means)]
    variances = [_dot((d * d).astype(BF16), mavg) for d in devs]
    for p in range(HEAD_PAIRS):
        cs = slice(p * LANES, (p + 1) * LANES)
        y = devs[p] * lax.rsqrt(variances[p] + EPS) * g_ref[:, cs].astype(F32)
        o_ref[:, cs] = y.astype(BF16)


def _retention(rq, rk, rv, rg, dmat, qdec, kdec, sdec, bmask, mavg128):
    b, s, _ = rq.shape
    tm = TM_RET
    row_spec = pl.BlockSpec((None, tm, R_WIDTH), lambda bi, si: (bi, si, 0))
    c3 = lambda shp: pl.BlockSpec(shp, lambda bi, si: (0, 0, 0))
    c2 = lambda shp: pl.BlockSpec(shp, lambda bi, si: (0, 0))
    return pl.pallas_call(
        _retention_body,
        grid=(b, s // tm),
        in_specs=[row_spec, row_spec, row_spec, row_spec,
                  c3((HEAD_PAIRS, L_RET, 2 * L_RET)), c3((HEAD_PAIRS, L_RET, LANES)),
                  c3((HEAD_PAIRS, L_RET, LANES)), c3((HEAD_PAIRS, LANES, LANES)),
                  c2((LANES, LANES)), c2((LANES, LANES))],
        out_specs=row_spec,
        out_shape=jax.ShapeDtypeStruct((b, s, R_WIDTH), BF16),
        scratch_shapes=[pltpu.VMEM((HEAD_PAIRS, LANES, LANES), F32)],
        compiler_params=pltpu.CompilerParams(
            dimension_semantics=("arbitrary", "arbitrary"), vmem_limit_bytes=VMEM_LIMIT),
        name="retention",
    )(rq, rk, rv, rg, dmat, qdec, kdec, sdec, bmask, mavg128)


def _fox_body(qt_ref, kaug_ref, vaug_ref, cft_ref, g_ref, o_ref, m_ref, acc_ref, sa_ref, sb_ref):
    qi = pl.program_id(1)
    srow_q = lax.broadcasted_iota(jnp.int32, (HEAD_DIM, 1), 0)
    q_ones = jnp.broadcast_to(jnp.where(srow_q < 3, 1.0, 0.0), (HEAD_DIM, TQ)).astype(BF16)
    qta = [jnp.concatenate([qt_ref[h * HEAD_DIM:(h + 1) * HEAD_DIM, :], q_ones], axis=0)
           for h in range(F_HEADS)]
    fq = [cft_ref[h:h + 1, :] for h in range(F_HEADS)]
    krow = lax.broadcasted_iota(jnp.int32, (TK, TQ), 0)
    qcol = lax.broadcasted_iota(jnp.int32, (TK, TQ), 1)
    causal = (krow <= qcol, krow + TK <= qcol)

    for h in range(F_HEADS):
        m_ref[h] = jnp.full((8, TQ), MASK_VALUE, F32)
        acc_ref[h] = jnp.zeros((VAUG_ROWS, TQ), F32)

    def scores(kj, dst_ref, h, last=False):
        k0 = pl.multiple_of(kj * TK, TK)
        kt = kaug_ref[h, pl.ds(k0, TK), :]
        if last:
            dst_ref[h, :, TK:TQ] = _dot(kt, qta[h][:, TK:TQ])
        else:
            dst_ref[h] = _dot(kt, qta[h])

    def consume(kj, src_ref, h, diagonal):
        k0 = pl.multiple_of(kj * TK, TK)
        st = src_ref[h]
        if diagonal:
            st = jnp.where(causal[0], st, MASK_VALUE)
        m_old = m_ref[h][0:1, :]
        m_new = jnp.maximum(m_old, jnp.max(st, axis=0, keepdims=True) + fq[h])
        p = jnp.exp2(st - (m_new - fq[h]))
        alpha = jnp.exp2(m_old - m_new)
        pv = _dot(vaug_ref[h, :, pl.ds(k0, TK)], p.astype(BF16))
        acc_ref[h] = alpha * acc_ref[h] + pv
        m_ref[h] = jnp.broadcast_to(m_new, (8, TQ))

    def consume_last(kj, src_ref, h):
        k0 = pl.multiple_of(kj * TK, TK)
        st = jnp.where(causal[1][:, TK:TQ], src_ref[h, :, TK:TQ], MASK_VALUE)
        m_old = m_ref[h][0:1, TK:TQ]
        fq_h = fq[h][:, TK:TQ]
        m_new = jnp.maximum(m_old, jnp.max(st, axis=0, keepdims=True) + fq_h)
        p = jnp.exp2(st - (m_new - fq_h))
        alpha = jnp.exp2(m_old - m_new)
        pv = _dot(vaug_ref[h, :, pl.ds(k0, TK)], p.astype(BF16))
        acc_ref[h, :, TK:TQ] = alpha * acc_ref[h, :, TK:TQ] + pv

    def overlapped(next_j, dst_ref, cur_j, src_ref, diagonal=False, next_last=False):
        lead = 2
        for h in range(lead):
            scores(next_j, dst_ref, h, next_last)
        for h in range(F_HEADS):
            consume(cur_j, src_ref, h, diagonal)
            if h + lead < F_HEADS:
                scores(next_j, dst_ref, h + lead, next_last)

    for h in range(F_HEADS):
        scores(0, sa_ref, h)

    def body(i, carry):
        j0 = 2 * i
        overlapped(j0 + 1, sb_ref, j0, sa_ref)
        overlapped(j0 + 2, sa_ref, j0 + 1, sb_ref)
        return carry

    lax.fori_loop(0, qi, body, 0)
    overlapped(2 * qi + 1, sb_ref, 2 * qi, sa_ref, diagonal=True, next_last=True)
    for h in range(F_HEADS):
        consume_last(2 * qi + 1, sb_ref, h)

    for p in range(HEAD_PAIRS):
        halves = []
        for hh in range(2):
            acc = acc_ref[2 * p + hh]
            halves.append(acc[0:HEAD_DIM, :] / acc[HEAD_DIM:HEAD_DIM + 1, :])
        out_t = jnp.concatenate(halves, axis=0)
        cs = slice(p * LANES, (p + 1) * LANES)
        o_ref[:, cs] = (out_t.T * g_ref[:, cs].astype(F32)).astype(BF16)


def _fox(fqt, kaug, vaug, fg, cf_t):
    b, _, s = fqt.shape
    assert TQ == 2 * TK
    row_spec = pl.BlockSpec((None, TQ, F_WIDTH), lambda bi, qi: (bi, qi, 0))
    return pl.pallas_call(
        _fox_body,
        grid=(b, s // TQ),
        in_specs=[pl.BlockSpec((None, F_WIDTH, TQ), lambda bi, qi: (bi, 0, qi)),
                  pl.BlockSpec((None, F_HEADS, s, LANES), lambda bi, qi: (bi, 0, 0, 0)),
                  pl.BlockSpec((None, F_HEADS, VAUG_ROWS, s), lambda bi, qi: (bi, 0, 0, 0)),
                  pl.BlockSpec((None, 8, TQ), lambda bi, qi: (bi, 0, qi)),
                  row_spec],
        out_specs=row_spec,
        out_shape=jax.ShapeDtypeStruct((b, s, F_WIDTH), BF16),
        scratch_shapes=[pltpu.VMEM((F_HEADS, 8, TQ), F32),
                        pltpu.VMEM((F_HEADS, VAUG_ROWS, TQ), F32),
                        pltpu.VMEM((F_HEADS, TK, TQ), F32),
                        pltpu.VMEM((F_HEADS, TK, TQ), F32)],
        compiler_params=pltpu.CompilerParams(
            dimension_semantics=("arbitrary", "arbitrary"), vmem_limit_bytes=VMEM_LIMIT),
        name="fox",
    )(fqt, kaug, vaug, cf_t, fg)


def _outproj_body(a_ref, r_ref, f_ref, w_ref, x_ref, pg_ref, o_ref):
    tm = x_ref.shape[0]

    def project(c):
        rows = slice(c * RC_OUT, (c + 1) * RC_OUT)
        y = jnp.concatenate([a_ref[rows, :], r_ref[rows, :], f_ref[rows, :]], axis=1)
        return _dot(y, w_ref[...])

    nchunks = tm // RC_OUT
    o_next = project(0)
    for c in range(nchunks):
        rows = slice(c * RC_OUT, (c + 1) * RC_OUT)
        o = o_next
        if c + 1 < nchunks:
            o_next = project(c + 1)
        ms = jnp.mean(o * o, axis=-1, keepdims=True)
        o_ref[rows, :] = x_ref[rows, :] + o * lax.rsqrt(ms + EPS) * pg_ref[...]


def _outproj(a, r, f, w, x, post_g):
    b, s, _ = x.shape
    tm = TM_OUT
    row_spec = lambda n: pl.BlockSpec((None, tm, n), lambda bi, si: (bi, si, 0))
    c2 = lambda shp: pl.BlockSpec(shp, lambda bi, si: (0, 0))
    return pl.pallas_call(
        _outproj_body,
        grid=(b, s // tm),
        in_specs=[row_spec(A_WIDTH), row_spec(R_WIDTH), row_spec(F_WIDTH),
                  c2((D_MODEL, D_MODEL)), row_spec(D_MODEL), c2((1, D_MODEL))],
        out_specs=row_spec(D_MODEL),
        out_shape=jax.ShapeDtypeStruct((b, s, D_MODEL), F32),
        compiler_params=pltpu.CompilerParams(
            dimension_semantics=("arbitrary", "arbitrary"), vmem_limit_bytes=VMEM_LIMIT),
        name="outproj",
    )(a, r, f, w, x, post_g)


@functools.lru_cache(maxsize=None)
def _tables(seq):
    half = HEAD_DIM // 2
    lane = np.arange(LANES)
    inv = ROPE_THETA ** (-(np.arange(half, dtype=np.float64) / half))
    ang = np.arange(seq, dtype=np.float64)[:, None] * inv[None, :]
    cos = np.cos(ang)
    sin = np.sin(ang)
    cos_t = cos[:, lane % half]
    sin_t = sin[:, lane % half] * np.where(lane < HEAD_DIM, -1.0, 1.0)[None, :]

    gam = 1.0 - np.exp2(-5.0 - np.arange(R_HEADS, dtype=np.float64))
    log_gam = np.log(gam)
    pos = np.arange(L_RET)
    dist = np.abs(pos[:, None] - pos[None, :])
    allowed = (pos[None, :] // CHUNK) <= (pos[:, None] // CHUNK)
    dmat = np.where(allowed[None], np.exp(log_gam[:, None, None] * dist[None]), 0.0)
    dmat = np.concatenate([dmat[0::2], dmat[1::2]], axis=2)
    head_of_lane = lane // HEAD_DIM
    head_of_qk_lane = (lane // half) % 2
    qdec = np.stack([np.exp(log_gam[2 * p + head_of_lane][None, :] * (pos + 1.0)[:, None])
                     for p in range(HEAD_PAIRS)])
    kdec = np.stack([np.exp(log_gam[2 * p + head_of_qk_lane][None, :] * (L_RET - 1.0 - pos)[:, None])
                     for p in range(HEAD_PAIRS)])
    bmask = (head_of_qk_lane[:, None] == head_of_lane[None, :]).astype(np.float32)
    sdec = np.stack([np.exp(log_gam[2 * p + head_of_qk_lane] * L_RET)[:, None] * bmask
                     for p in range(HEAD_PAIRS)])
    tri = (np.arange(RC_IN)[None, :] <= np.arange(RC_IN)[:, None]).astype(np.float32)
    grp = np.arange(A_WIDTH) // HEAD_DIM
    mavg = (grp[:, None] == grp[None, :]).astype(np.float32) / HEAD_DIM
    place = np.zeros((LANES, F_HEADS * LANES), np.float32)
    for h in range(F_HEADS):
        for term in range(3):
            place[8 * term + h, h * LANES + HEAD_DIM + term] = 1.0
    f = lambda a: np.asarray(a, np.float32)
    return dict(cos=f(cos_t), sin=f(sin_t), dmat=f(dmat), qdec=f(qdec), kdec=f(kdec),
                sdec=f(sdec), bmask=f(bmask), tri=f(tri), mavg=f(mavg),
                mavg128=f(mavg[:LANES, :LANES]), place=f(place))


def _layer(x, pre_g, post_g, w_in, b_f, a_ln_g, a_ws, a_bs, w_out, t):
    half = HEAD_DIM // 2
    w_t = w_in.T.astype(BF16)
    qk = w_t[A_COLS:A_COLS + 2 * R_WIDTH].reshape(2 * HEAD_PAIRS, 2, 2, half, D_MODEL)
    qk = jnp.swapaxes(qk, 1, 2).reshape(2 * R_WIDTH, D_MODEL)
    w_main = jnp.concatenate([w_t[:A_COLS], qk, w_t[A_COLS + 2 * R_WIDTH:D_IN_MAIN]], axis=0)
    w_flg = jnp.pad(w_t[D_IN_MAIN:], ((0, LANES - F_HEADS), (0, 0)))
    bf_pad = jnp.pad(b_f, (0, LANES - F_HEADS)).reshape(1, LANES)
    abias = jnp.repeat(a_bs.T, HEAD_DIM, axis=1)
    (a_out, rq, rk, rv, rg, fqt, kaug, vaug, fg, cf_t) = _inproj(
        x, pre_g.reshape(1, D_MODEL), w_main, w_flg, bf_pad, a_ln_g.reshape(1, A_WIDTH), a_ws, abias,
        t["cos"], t["sin"], t["tri"], t["mavg"], t["place"])
    r_out = _retention(rq, rk, rv, rg, t["dmat"], t["qdec"], t["kdec"], t["sdec"], t["bmask"],
                       t["mavg128"])
    f_out = _fox(fqt, kaug, vaug, fg, cf_t)
    return _outproj(a_out, r_out, f_out, w_out.astype(BF16), x, post_g.reshape(1, D_MODEL))


def kernel(x, pre_gain, post_gain, w_in, b_forget, a_norm_gain, a_spatial_w, a_spatial_b, w_out):
    bf16_tables = ("tri", "mavg", "mavg128", "place")
    t = {k: jnp.asarray(v, BF16 if k in bf16_tables else F32) for k, v in _tables(x.shape[1]).items()}
    for l in range(pre_gain.shape[0]):
        x = _layer(x, pre_gain[l], post_gain[l], w_in[l], b_forget[l], a_norm_gain[l],
                   a_spatial_w[l], a_spatial_b[l], w_out[l], t)
    return x
```

```python
import functools
import math

import jax
import jax.numpy as jnp
import numpy as np
from jax import lax
from jax.experimental import pallas as pl
from jax.experimental.pallas import tpu as pltpu

F32 = jnp.float32
BF16 = jnp.bfloat16

D_MODEL = 1024
HEAD_DIM = 64
CHUNK = 64
A_WIDTH = 256
R_WIDTH = 384
F_WIDTH = 384
A_GROUPS = 4
R_HEADS = 6
F_HEADS = 6
A_BLOCK = 128
ROPE_THETA = 10000.0
EPS = 1e-6
LANES = 128
HEAD_PAIRS = R_WIDTH // LANES

A_COLS = 3 * A_WIDTH
R_COLS = 4 * R_WIDTH
F_COLS = 4 * F_WIDTH + F_HEADS
D_IN = A_COLS + R_COLS + F_COLS
D_IN_MAIN = A_COLS + R_COLS + 4 * F_WIDTH

TM_IN = 1024
RC_IN = 256
TM_RET = 2048
L_RET = 128
TQ = 512
TK = 256
TM_OUT = 2048
RC_OUT = 256
VMEM_LIMIT = 56 * 1024 * 1024
VAUG_ROWS = HEAD_DIM + 16
LOG2E = 1.4426950408889634
MASK_VALUE = -1e30


def _silu(x):
    return 0.5 * x * (1.0 + jnp.tanh(0.5 * x))


def _gelu_tanh(x):
    c = math.sqrt(2.0 / math.pi)
    return 0.5 * x * (1.0 + jnp.tanh(c * (x + 0.044715 * (x * x * x))))


def _log_sigmoid(x):
    return jnp.minimum(x, 0.0) - jnp.log1p(jnp.exp(-jnp.abs(x)))


def _dot(a, b):
    return jnp.dot(a, b, preferred_element_type=F32)


def _group_norm64(x, mavg):
    mean = _dot(x.astype(BF16), mavg)
    d = x - mean
    var = _dot((d * d).astype(BF16), mavg)
    return d * lax.rsqrt(var + EPS)


def _inproj_body(x_ref, pg_ref, w_ref, wflg_ref, bf_ref, aln_ref, aws_ref, abias_ref, cos_ref, sin_ref,
                 tri_ref, mavg_ref, place_ref,
                 a_ref, rq_ref, rk_ref, rv_ref, rg_ref, fq_ref, kaug_ref, vaug_ref, fg_ref, cft_ref,
                 carry_ref):
    tm = x_ref.shape[0]

    @pl.when(pl.program_id(1) == 0)
    def _():
        carry_ref[...] = jnp.zeros_like(carry_ref)

    mavg = mavg_ref[...]
    tri = tri_ref[...]
    row = lax.broadcasted_iota(jnp.int32, (A_BLOCK, A_BLOCK), 0)
    col = lax.broadcasted_iota(jnp.int32, (A_BLOCK, A_BLOCK), 1)
    allowed = jnp.logical_not(jnp.logical_and(row < CHUNK, col >= CHUNK))
    wcat = jnp.concatenate(
        [jnp.where(allowed, aws_ref[gi], 0.0).astype(BF16) for gi in range(A_GROUPS)], axis=1)
    lane_grp = lax.shift_right_logical(
        lax.broadcasted_iota(jnp.int32, (A_BLOCK, A_WIDTH), 1), HEAD_DIM.bit_length() - 1)
    zero_bf = jnp.zeros((A_BLOCK, A_WIDTH), BF16)
    lane = lax.broadcasted_iota(jnp.int32, (1, LANES), 1)
    srow = lax.broadcasted_iota(jnp.int32, (VAUG_ROWS - HEAD_DIM, 1), 0)
    ones_rows = jnp.broadcast_to(jnp.where(srow == 0, 1.0, 0.0),
                                 (VAUG_ROWS - HEAD_DIM, RC_IN)).astype(BF16)

    def rope(z, cos_t, sin_t):
        outs = []
        for c in range(HEAD_PAIRS):
            zc = z[:, c * LANES:(c + 1) * LANES]
            outs.append(zc * cos_t + pltpu.roll(zc, HEAD_DIM, 1) * sin_t)
        return jnp.concatenate(outs, axis=1)

    def normed(c):
        x = x_ref[c * RC_IN:(c + 1) * RC_IN, :]
        ms = jnp.mean(x * x, axis=-1, keepdims=True)
        return (x * lax.rsqrt(ms + EPS) * pg_ref[...]).astype(BF16)

    wide_cols = ((0, A_COLS), (A_COLS, A_COLS + R_COLS), (A_COLS + R_COLS, D_IN_MAIN))
    contract_last = (((1,), (1,)), ((), ()))

    def wide(h, k):
        z = lax.dot_general(h, w_ref[wide_cols[k][0]:wide_cols[k][1], :], contract_last,
                            preferred_element_type=F32)
        if k == 2:
            z_flg = lax.dot_general(h, wflg_ref[...], contract_last, preferred_element_type=F32)
            z = jnp.concatenate([z, z_flg], axis=1)
        return z

    def rows_of(c):
        return slice(c * RC_IN, (c + 1) * RC_IN)

    def a1(c, za, e):
        e["u"] = _gelu_tanh(za[:, 0:A_WIDTH])
        e["v"] = _gelu_tanh(za[:, A_WIDTH:2 * A_WIDTH])
        e["g"] = _silu(za[:, 2 * A_WIDTH:3 * A_WIDTH])
        e["mean"] = _dot(e["v"].astype(BF16), mavg)

    def a2(c, za, e):
        e["d"] = e["v"] - e["mean"]
        e["var"] = _dot((e["d"] * e["d"]).astype(BF16), mavg)

    def a3(c, za, e):
        vn = (e["d"] * lax.rsqrt(e["var"] + EPS) * aln_ref[...]).astype(BF16)
        mixed_blocks = []
        for nb in range(RC_IN // A_BLOCK):
            vb = vn[nb * A_BLOCK:(nb + 1) * A_BLOCK, :]
            vstack = jnp.concatenate(
                [jnp.where(lane_grp == gi, vb, zero_bf) for gi in range(A_GROUPS)], axis=0)
            mixed_blocks.append(_dot(wcat, vstack) + abias_ref[...])
        e["mixed"] = jnp.concatenate(mixed_blocks, axis=0)

    def a4(c, za, e):
        a_ref[rows_of(c), :] = (e["u"] * e["mixed"] * e["g"]).astype(BF16)

    def r1(c, zr, e):
        rows = rows_of(c)
        cos_t = cos_ref[rows, :]
        sin_t = sin_ref[rows, :]
        rq_ref[rows, :] = (rope(zr[:, 0:R_WIDTH], cos_t, sin_t) * (HEAD_DIM ** -0.5)).astype(BF16)
        rk_ref[rows, :] = rope(zr[:, R_WIDTH:2 * R_WIDTH], cos_t, sin_t).astype(BF16)
        rv_ref[rows, :] = zr[:, 2 * R_WIDTH:3 * R_WIDTH].astype(BF16)
        rg_ref[rows, :] = _silu(zr[:, 3 * R_WIDTH:4 * R_WIDTH]).astype(BF16)

    def f1(c, zf, e):
        ls = _log_sigmoid(zf[:, 4 * F_WIDTH:4 * F_WIDTH + LANES] + bf_ref[...])
        ls_hi = ls.astype(BF16)
        ls_lo = (ls - ls_hi.astype(F32)).astype(BF16)
        cf = _dot(tri, ls_hi) + _dot(tri, ls_lo) + carry_ref[...]
        carry_ref[...] = cf[RC_IN - 1:RC_IN, :]
        e["cf2"] = cf * LOG2E

    def f2(c, zf, e):
        cf2 = e["cf2"]
        cft_ref[:, rows_of(c)] = cf2.T[:8, :]
        negf = -cf2
        hi = negf.astype(BF16).astype(F32)
        mid = (negf - hi).astype(BF16).astype(F32)
        lo = (negf - hi - mid).astype(BF16).astype(F32)
        packed = jnp.where(lane < 8, hi,
                           jnp.where(lane < 16, pltpu.roll(mid, 8, 1),
                                     jnp.where(lane < 24, pltpu.roll(lo, 16, 1), 0.0)))
        e["placed"] = _dot(packed.astype(BF16), place_ref[...])

    def f3(c, zf, e):
        rows = rows_of(c)
        fq_ref[:, rows] = (zf[:, 0:F_WIDTH] * (HEAD_DIM ** -0.5 * LOG2E)).T.astype(BF16)
        fg_ref[rows, :] = _silu(zf[:, 3 * F_WIDTH:4 * F_WIDTH]).astype(BF16)
        fv_t = zf[:, 2 * F_WIDTH:3 * F_WIDTH].T.astype(BF16)
        for hd in range(F_HEADS):
            p, hh = divmod(hd, 2)
            kk = zf[:, F_WIDTH + p * LANES:F_WIDTH + (p + 1) * LANES]
            if hh == 1:
                kk = pltpu.roll(kk, HEAD_DIM, 1)
            aug = e["placed"][:, hd * LANES:(hd + 1) * LANES]
            kaug_ref[hd, rows, :] = jnp.where(lane < HEAD_DIM, kk, aug).astype(BF16)
            vaug_ref[hd, 0:HEAD_DIM, rows] = fv_t[hd * HEAD_DIM:(hd + 1) * HEAD_DIM, :]
            vaug_ref[hd, HEAD_DIM:VAUG_ROWS, rows] = ones_rows

    nchunks = tm // RC_IN
    hs = [normed(c) for c in range(nchunks)]
    groups = ((0, (a1, a2, a3, a4)), (2, (f1, f2, f3)), (1, (r1,)))
    issued = []

    def run_slot(t):
        for s, (c, z, stages, e) in enumerate(issued):
            j = t - 1 - s
            if 0 <= j < len(stages):
                stages[j](c, z, e)

    for k, stages in groups:
        for c in range(nchunks):
            issued.append((c, wide(hs[c], k), stages, {}))
            run_slot(len(issued) - 1)
    depth = max(len(stages) for _, stages in groups)
    for t in range(len(issued), len(issued) + depth):
        run_slot(t)


def _inproj(x, pre_g, w_main, w_flg, bf_pad, aln, aws, abias, cos_t, sin_t, tri, mavg, place):
    b, s, _ = x.shape
    tm = TM_IN
    grid = (b, s // tm)
    row_spec = lambda n: pl.BlockSpec((None, tm, n), lambda bi, si: (bi, si, 0))
    col_spec = lambda n: pl.BlockSpec((None, n, tm), lambda bi, si: (bi, 0, si))
    const2 = lambda shp: pl.BlockSpec(shp, lambda bi, si: (0, 0))
    bf_sds = lambda n: jax.ShapeDtypeStruct((b, s, n), BF16)
    out_shape = ([bf_sds(A_WIDTH)] + [bf_sds(R_WIDTH)] * 4
                 + [jax.ShapeDtypeStruct((b, F_WIDTH, s), BF16),
                    jax.ShapeDtypeStruct((b, F_HEADS, s, LANES), BF16),
                    jax.ShapeDtypeStruct((b, F_HEADS, VAUG_ROWS, s), BF16),
                    bf_sds(F_WIDTH),
                    jax.ShapeDtypeStruct((b, 8, s), F32)])
    out_specs = ([row_spec(A_WIDTH)] + [row_spec(R_WIDTH)] * 4
                 + [col_spec(F_WIDTH),
                    pl.BlockSpec((None, F_HEADS, tm, LANES), lambda bi, si: (bi, 0, si, 0)),
                    pl.BlockSpec((None, F_HEADS, VAUG_ROWS, tm), lambda bi, si: (bi, 0, 0, si)),
                    row_spec(F_WIDTH),
                    col_spec(8)])
    in_specs = [
        row_spec(D_MODEL),
        const2((1, D_MODEL)),
        pl.BlockSpec((D_IN_MAIN, D_MODEL), lambda bi, si: (0, 0), pipeline_mode=pl.Buffered(1)),
        const2((LANES, D_MODEL)),
        const2((1, LANES)),
        const2((1, A_WIDTH)),
        pl.BlockSpec((A_GROUPS, A_BLOCK, A_BLOCK), lambda bi, si: (0, 0, 0)),
        const2((A_BLOCK, A_WIDTH)),
        pl.BlockSpec((tm, LANES), lambda bi, si: (si, 0)),
        pl.BlockSpec((tm, LANES), lambda bi, si: (si, 0)),
        const2((RC_IN, RC_IN)),
        const2((A_WIDTH, A_WIDTH)),
        const2((LANES, F_HEADS * LANES)),
    ]
    return pl.pallas_call(
        _inproj_body,
        grid=grid,
        in_specs=in_specs,
        out_specs=out_specs,
        out_shape=out_shape,
        scratch_shapes=[pltpu.VMEM((1, LANES), F32)],
        compiler_params=pltpu.CompilerParams(
            dimension_semantics=("arbitrary", "arbitrary"), vmem_limit_bytes=VMEM_LIMIT),
        name="inproj",
    )(x, pre_g, w_main, w_flg, bf_pad, aln, aws, abias, cos_t, sin_t, tri, mavg, place)


def _retention_body(q_ref, k_ref, v_ref, g_ref, dmat_ref, qdec_ref, kdec_ref, sdec_ref, bmask_ref,
                    mavg_ref, o_ref, state_ref):
    tm = q_ref.shape[0]

    @pl.when(pl.program_id(1) == 0)
    def _():
        state_ref[...] = jnp.zeros_like(state_ref)

    lane = lax.broadcasted_iota(jnp.int32, (1, LANES), 1)
    head0_v = lane < HEAD_DIM
    head0_qk = jnp.bitwise_and(lane, HEAD_DIM // 2) == 0
    mavg = mavg_ref[...]
    bmask = bmask_ref[...]
    contract_last = (((1,), (1,)), ((), ()))
    contract_first = (((0,), (0,)), ((), ()))
    nblk = tm // L_RET
    units = [(p, j) for p in range(HEAD_PAIRS) for j in range(nblk)]

    def split_heads(x2, head0):
        zero = jnp.zeros_like(x2)
        return jnp.concatenate([jnp.where(head0, x2, zero), jnp.where(head0, zero, x2)], axis=0)

    q2, v2, sc, kv = {}, {}, {}, {}
    for (p, j) in units:
        rs, cs = slice(j * L_RET, (j + 1) * L_RET), slice(p * LANES, (p + 1) * LANES)
        q2[p, j] = q_ref[rs, cs]
        k2 = k_ref[rs, cs]
        v2[p, j] = v_ref[rs, cs]
        sc[p, j] = lax.dot_general(q2[p, j], split_heads(k2, head0_qk), contract_last,
                                   preferred_element_type=F32)
        kd = (k2.astype(F32) * kdec_ref[p]).astype(BF16)
        kv[p, j] = lax.dot_general(kd, v2[p, j], contract_first, preferred_element_type=F32)

    st = {}
    for p in range(HEAD_PAIRS):
        state = state_ref[p]
        for j in range(nblk):
            st[p, j] = state.astype(BF16)
            state = state * sdec_ref[p] + kv[p, j] * bmask
        state_ref[p] = state

    out = {}
    for (p, j) in units:
        pm = (sc[p, j] * dmat_ref[p]).astype(BF16)
        out[p, j] = (_dot(pm, split_heads(v2[p, j], head0_v))
                     + _dot(q2[p, j], st[p, j]) * qdec_ref[p])

    tiles = [jnp.concatenate([out[p, j] for j in range(nblk)], axis=0) for p in range(HEAD_PAIRS)]
    means = [_dot(t.astype(BF16), mavg) for t in tiles]
    devs = [t - mu for t, mu in zip(tiles, means)]
    variances = [_dot((d * d).astype(BF16), mavg) for d in devs]
    for p in range(HEAD_PAIRS):
        cs = slice(p * LANES, (p + 1) * LANES)
        y = devs[p] * lax.rsqrt(variances[p] + EPS) * g_ref[:, cs].astype(F32)
        o_ref[:, cs] = y.astype(BF16)


def _retention(rq, rk, rv, rg, dmat, qdec, kdec, sdec, bmask, mavg128):
    b, s, _ = rq.shape
    tm = TM_RET
    row_spec = pl.BlockSpec((None, tm, R_WIDTH), lambda bi, si: (bi, si, 0))
    c3 = lambda shp: pl.BlockSpec(shp, lambda bi, si: (0, 0, 0))
    c2 = lambda shp: pl.BlockSpec(shp, lambda bi, si: (0, 0))
    return pl.pallas_call(
        _retention_body,
        grid=(b, s // tm),
        in_specs=[row_spec, row_spec, row_spec, row_spec,
                  c3((HEAD_PAIRS, L_RET, 2 * L_RET)), c3((HEAD_PAIRS, L_RET, LANES)),
                  c3((HEAD_PAIRS, L_RET, LANES)), c3((HEAD_PAIRS, LANES, LANES)),
                  c2((LANES, LANES)), c2((LANES, LANES))],
        out_specs=row_spec,
        out_shape=jax.ShapeDtypeStruct((b, s, R_WIDTH), BF16),
        scratch_shapes=[pltpu.VMEM((HEAD_PAIRS, LANES, LANES), F32)],
        compiler_params=pltpu.CompilerParams(
            dimension_semantics=("arbitrary", "arbitrary"), vmem_limit_bytes=VMEM_LIMIT),
        name="retention",
    )(rq, rk, rv, rg, dmat, qdec, kdec, sdec, bmask, mavg128)


def _fox_body(qt_ref, kaug_ref, vaug_ref, cft_ref, g_ref, o_ref, m_ref, acc_ref, sa_ref, sb_ref):
    qi = pl.program_id(1)
    srow_q = lax.broadcasted_iota(jnp.int32, (HEAD_DIM, 1), 0)
    q_ones = jnp.broadcast_to(jnp.where(srow_q < 3, 1.0, 0.0), (HEAD_DIM, TQ)).astype(BF16)
    qta = [jnp.concatenate([qt_ref[h * HEAD_DIM:(h + 1) * HEAD_DIM, :], q_ones], axis=0)
           for h in range(F_HEADS)]
    fq = [cft_ref[h:h + 1, :] for h in range(F_HEADS)]
    krow = lax.broadcasted_iota(jnp.int32, (TK, TQ), 0)
    qcol = lax.broadcasted_iota(jnp.int32, (TK, TQ), 1)
    causal = (krow <= qcol, krow + TK <= qcol)

    for h in range(F_HEADS):
        m_ref[h] = jnp.full((8, TQ), MASK_VALUE, F32)
        acc_ref[h] = jnp.zeros((VAUG_ROWS, TQ), F32)

    def scores(kj, dst_ref, h, last=False):
        k0 = pl.multiple_of(kj * TK, TK)
        kt = kaug_ref[h, pl.ds(k0, TK), :]
        if last:
            dst_ref[h, :, TK:TQ] = _dot(kt, qta[h][:, TK:TQ])
        else:
            dst_ref[h] = _dot(kt, qta[h])

    def consume(kj, src_ref, h, diagonal):
        k0 = pl.multiple_of(kj * TK, TK)
        st = src_ref[h]
        if diagonal:
            st = jnp.where(causal[0], st, MASK_VALUE)
        m_old = m_ref[h][0:1, :]
        m_new = jnp.maximum(m_old, jnp.max(st, axis=0, keepdims=True) + fq[h])
        p = jnp.exp2(st - (m_new - fq[h]))
        alpha = jnp.exp2(m_old - m_new)
        pv = _dot(vaug_ref[h, :, pl.ds(k0, TK)], p.astype(BF16))
        acc_ref[h] = alpha * acc_ref[h] + pv
        m_ref[h] = jnp.broadcast_to(m_new, (8, TQ))

    def consume_last(kj, src_ref, h):
        k0 = pl.multiple_of(kj * TK, TK)
        st = jnp.where(causal[1][:, TK:TQ], src_ref[h, :, TK:TQ], MASK_VALUE)
        m_old = m_ref[h][0:1, TK:TQ]
        fq_h = fq[h][:, TK:TQ]
        m_new = jnp.maximum(m_old, jnp.max(st, axis=0, keepdims=True) + fq_h)
        p = jnp.exp2(st - (m_new - fq_h))
        alpha = jnp.exp2(m_old - m_new)
        pv = _dot(vaug_ref[h, :, pl.ds(k0, TK)], p.astype(BF16))
        acc_ref[h, :, TK:TQ] = alpha * acc_ref[h, :, TK:TQ] + pv

    def overlapped(next_j, dst_ref, cur_j, src_ref, diagonal=False, next_last=False):
        lead = 2
        for h in range(lead):
            scores(next_j, dst_ref, h, next_last)
        for h in range(F_HEADS):
            consume(cur_j, src_ref, h, diagonal)
            if h + lead < F_HEADS:
                scores(next_j, dst_ref, h + lead, next_last)

    for h in range(F_HEADS):
        scores(0, sa_ref, h)

    def body(i, carry):
        j0 = 2 * i
        overlapped(j0 + 1, sb_ref, j0, sa_ref)
        overlapped(j0 + 2, sa_ref, j0 + 1, sb_ref)
        return carry

    lax.fori_loop(0, qi, body, 0)
    overlapped(2 * qi + 1, sb_ref, 2 * qi, sa_ref, diagonal=True, next_last=True)
    for h in range(F_HEADS):
        consume_last(2 * qi + 1, sb_ref, h)

    for p in range(HEAD_PAIRS):
        halves = []
        for hh in range(2):
            acc = acc_ref[2 * p + hh]
            halves.append(acc[0:HEAD_DIM, :] / acc[HEAD_DIM:HEAD_DIM + 1, :])
        out_t = jnp.concatenate(halves, axis=0)
        cs = slice(p * LANES, (p + 1) * LANES)
        o_ref[:, cs] = (out_t.T * g_ref[:, cs].astype(F32)).astype(BF16)


def _fox(fqt, kaug, vaug, fg, cf_t):
    b, _, s = fqt.shape
    assert TQ == 2 * TK
    row_spec = pl.BlockSpec((None, TQ, F_WIDTH), lambda bi, qi: (bi, qi, 0))
    return pl.pallas_call(
        _fox_body,
        grid=(b, s // TQ),
        in_specs=[pl.BlockSpec((None, F_WIDTH, TQ), lambda bi, qi: (bi, 0, qi)),
                  pl.BlockSpec((None, F_HEADS, s, LANES), lambda bi, qi: (bi, 0, 0, 0)),
                  pl.BlockSpec((None, F_HEADS, VAUG_ROWS, s), lambda bi, qi: (bi, 0, 0, 0)),
                  pl.BlockSpec((None, 8, TQ), lambda bi, qi: (bi, 0, qi)),
                  row_spec],
        out_specs=row_spec,
        out_shape=jax.ShapeDtypeStruct((b, s, F_WIDTH), BF16),
        scratch_shapes=[pltpu.VMEM((F_HEADS, 8, TQ), F32),
                        pltpu.VMEM((F_HEADS, VAUG_ROWS, TQ), F32),
                        pltpu.VMEM((F_HEADS, TK, TQ), F32),
                        pltpu.VMEM((F_HEADS, TK, TQ), F32)],
        compiler_params=pltpu.CompilerParams(
            dimension_semantics=("arbitrary", "arbitrary"), vmem_limit_bytes=VMEM_LIMIT),
        name="fox",
    )(fqt, kaug, vaug, cf_t, fg)


def _outproj_body(a_ref, r_ref, f_ref, w_ref, x_ref, pg_ref, o_ref):
    tm = x_ref.shape[0]

    def project(c):
        rows = slice(c * RC_OUT, (c + 1) * RC_OUT)
        y = jnp.concatenate([a_ref[rows, :], r_ref[rows, :], f_ref[rows, :]], axis=1)
        return _dot(y, w_ref[...])

    nchunks = tm // RC_OUT
    o_next = project(0)
    for c in range(nchunks):
        rows = slice(c * RC_OUT, (c + 1) * RC_OUT)
        o = o_next
        if c + 1 < nchunks:
            o_next = project(c + 1)
        ms = jnp.mean(o * o, axis=-1, keepdims=True)
        o_ref[rows, :] = x_ref[rows, :] + o * lax.rsqrt(ms + EPS) * pg_ref[...]


def _outproj(a, r, f, w, x, post_g):
    b, s, _ = x.shape
    tm = TM_OUT
    row_spec = lambda n: pl.BlockSpec((None, tm, n), lambda bi, si: (bi, si, 0))
    c2 = lambda shp: pl.BlockSpec(shp, lambda bi, si: (0, 0))
    return pl.pallas_call(
        _outproj_body,
        grid=(b, s // tm),
        in_specs=[row_spec(A_WIDTH), row_spec(R_WIDTH), row_spec(F_WIDTH),
                  c2((D_MODEL, D_MODEL)), row_spec(D_MODEL), c2((1, D_MODEL))],
        out_specs=row_spec(D_MODEL),
        out_shape=jax.ShapeDtypeStruct((b, s, D_MODEL), F32),
        compiler_params=pltpu.CompilerParams(
            dimension_semantics=("arbitrary", "arbitrary"), vmem_limit_bytes=VMEM_LIMIT),
        name="outproj",
    )(a, r, f, w, x, post_g)


@functools.lru_cache(maxsize=None)
def _tables(seq):
    half = HEAD_DIM // 2
    lane = np.arange(LANES)
    inv = ROPE_THETA ** (-(np.arange(half, dtype=np.float64) / half))
    ang = np.arange(seq, dtype=np.float64)[:, None] * inv[None, :]
    cos = np.cos(ang)
    sin = np.sin(ang)
    cos_t = cos[:, lane % half]
    sin_t = sin[:, lane % half] * np.where(lane < HEAD_DIM, -1.0, 1.0)[None, :]

    gam = 1.0 - np.exp2(-5.0 - np.arange(R_HEADS, dtype=np.float64))
    log_gam = np.log(gam)
    pos = np.arange(L_RET)
    dist = np.abs(pos[:, None] - pos[None, :])
    allowed = (pos[None, :] // CHUNK) <= (pos[:, None] // CHUNK)
    dmat = np.where(allowed[None], np.exp(log_gam[:, None, None] * dist[None]), 0.0)
    dmat = np.concatenate([dmat[0::2], dmat[1::2]], axis=2)
    head_of_lane = lane // HEAD_DIM
    head_of_qk_lane = (lane // half) % 2
    qdec = np.stack([np.exp(log_gam[2 * p + head_of_lane][None, :] * (pos + 1.0)[:, None])
                     for p in range(HEAD_PAIRS)])
    kdec = np.stack([np.exp(log_gam[2 * p + head_of_qk_lane][None, :] * (L_RET - 1.0 - pos)[:, None])
                     for p in range(HEAD_PAIRS)])
    bmask = (head_of_qk_lane[:, None] == head_of_lane[None, :]).astype(np.float32)
    sdec = np.stack([np.exp(log_gam[2 * p + head_of_qk_lane] * L_RET)[:, None] * bmask
                     for p in range(HEAD_PAIRS)])
    tri = (np.arange(RC_IN)[None, :] <= np.arange(RC_IN)[:, None]).astype(np.float32)
    grp = np.arange(A_WIDTH) // HEAD_DIM
    mavg = (grp[:, None] == grp[None, :]).astype(np.float32) / HEAD_DIM
    place = np.zeros((LANES, F_HEADS * LANES), np.float32)
    for h in range(F_HEADS):
        for term in range(3):
            place[8 * term + h, h * LANES + HEAD_DIM + term] = 1.0
    f = lambda a: np.asarray(a, np.float32)
    return dict(cos=f(cos_t), sin=f(sin_t), dmat=f(dmat), qdec=f(qdec), kdec=f(kdec),
                sdec=f(sdec), bmask=f(bmask), tri=f(tri), mavg=f(mavg),
                mavg128=f(mavg[:LANES, :LANES]), place=f(place))


def _layer(x, pre_g, post_g, w_in, b_f, a_ln_g, a_ws, a_bs, w_out, t):
    half = HEAD_DIM // 2
    w_t = w_in.T.astype(BF16)
    qk = w_t[A_COLS:A_COLS + 2 * R_WIDTH].reshape(2 * HEAD_PAIRS, 2, 2, half, D_MODEL)
    qk = jnp.swapaxes(qk, 1, 2).reshape(2 * R_WIDTH, D_MODEL)
    w_main = jnp.concatenate([w_t[:A_COLS], qk, w_t[A_COLS + 2 * R_WIDTH:D_IN_MAIN]], axis=0)
    w_flg = jnp.pad(w_t[D_IN_MAIN:], ((0, LANES - F_HEADS), (0, 0)))
    bf_pad = jnp.pad(b_f, (0, LANES - F_HEADS)).reshape(1, LANES)
    abias = jnp.repeat(a_bs.T, HEAD_DIM, axis=1)
    (a_out, rq, rk, rv, rg, fqt, kaug, vaug, fg, cf_t) = _inproj(
        x, pre_g.reshape(1, D_MODEL), w_main, w_flg, bf_pad, a_ln_g.reshape(1, A_WIDTH), a_ws, abias,
        t["cos"], t["sin"], t["tri"], t["mavg"], t["place"])
    r_out = _retention(rq, rk, rv, rg, t["dmat"], t["qdec"], t["kdec"], t["sdec"], t["bmask"],
                       t["mavg128"])
    f_out = _fox(fqt, kaug, vaug, fg, cf_t)
    return _outproj(a_out, r_out, f_out, w_out.astype(BF16), x, post_g.reshape(1, D_MODEL))


def kernel(x, pre_gain, post_gain, w_in, b_forget, a_norm_gain, a_spatial_w, a_spatial_b, w_out):
    bf16_tables = ("tri", "mavg", "mavg128", "place")
    t = {k: jnp.asarray(v, BF16 if k in bf16_tables else F32) for k, v in _tables(x.shape[1]).items()}
    for l in range(pre_gain.shape[0]):
        x = _layer(x, pre_gain[l], post_gain[l], w_in[l], b_forget[l], a_norm_gain[l],
                   a_spatial_w[l], a_spatial_b[l], w_out[l], t)
    return x
```

```python
import functools
import math

import jax
import jax.numpy as jnp
import numpy as np
from jax import lax
from jax.experimental import pallas as pl
from jax.experimental.pallas import tpu as pltpu

F32 = jnp.float32
BF16 = jnp.bfloat16

D_MODEL = 1024
HEAD_DIM = 64
CHUNK = 64
A_WIDTH = 256
R_WIDTH = 384
F_WIDTH = 384
A_GROUPS = 4
R_HEADS = 6
F_HEADS = 6
A_BLOCK = 128
ROPE_THETA = 10000.0
EPS = 1e-6
LANES = 128
HEAD_PAIRS = R_WIDTH // LANES

A_COLS = 3 * A_WIDTH
R_COLS = 4 * R_WIDTH
F_COLS = 4 * F_WIDTH + F_HEADS
D_IN = A_COLS + R_COLS + F_COLS
D_IN_MAIN = A_COLS + R_COLS + 4 * F_WIDTH

TM_IN = 1024
RC_IN = 256
TM_RET = 2048
L_RET = 128
TQ = 512
TK = 256
TM_OUT = 2048
RC_OUT = 256
VMEM_LIMIT = 56 * 1024 * 1024
VAUG_ROWS = HEAD_DIM + 16
LOG2E = 1.4426950408889634
MASK_VALUE = -1e30


def _silu(x):
    return 0.5 * x * (1.0 + jnp.tanh(0.5 * x))


def _gelu_tanh(x):
    c = math.sqrt(2.0 / math.pi)
    return 0.5 * x * (1.0 + jnp.tanh(c * (x + 0.044715 * (x * x * x))))


def _log_sigmoid(x):
    return jnp.minimum(x, 0.0) - jnp.log1p(jnp.exp(-jnp.abs(x)))


def _dot(a, b):
    return jnp.dot(a, b, preferred_element_type=F32)


def _group_norm64(x, mavg):
    mean = _dot(x.astype(BF16), mavg)
    d = x - mean
    var = _dot((d * d).astype(BF16), mavg)
    return d * lax.rsqrt(var + EPS)


def _inproj_body(x_ref, pg_ref, w_ref, wflg_ref, bf_ref, aln_ref, aws_ref, abias_ref, cos_ref, sin_ref,
                 tri_ref, mavg_ref, place_ref,
                 a_ref, rq_ref, rk_ref, rv_ref, rg_ref, fq_ref, kaug_ref, vaug_ref, fg_ref, cft_ref,
                 carry_ref):
    tm = x_ref.shape[0]

    @pl.when(pl.program_id(1) == 0)
    def _():
        carry_ref[...] = jnp.zeros_like(carry_ref)

    mavg = mavg_ref[...]
    tri = tri_ref[...]
    row = lax.broadcasted_iota(jnp.int32, (A_BLOCK, A_BLOCK), 0)
    col = lax.broadcasted_iota(jnp.int32, (A_BLOCK, A_BLOCK), 1)
    allowed = jnp.logical_not(jnp.logical_and(row < CHUNK, col >= CHUNK))
    wcat = jnp.concatenate(
        [jnp.where(allowed, aws_ref[gi], 0.0).astype(BF16) for gi in range(A_GROUPS)], axis=1)
    lane_grp = lax.shift_right_logical(
        lax.broadcasted_iota(jnp.int32, (A_BLOCK, A_WIDTH), 1), HEAD_DIM.bit_length() - 1)
    zero_bf = jnp.zeros((A_BLOCK, A_WIDTH), BF16)
    lane = lax.broadcasted_iota(jnp.int32, (1, LANES), 1)
    srow = lax.broadcasted_iota(jnp.int32, (VAUG_ROWS - HEAD_DIM, 1), 0)
    ones_rows = jnp.broadcast_to(jnp.where(srow == 0, 1.0, 0.0),
                                 (VAUG_ROWS - HEAD_DIM, RC_IN)).astype(BF16)

    def rope(z, cos_t, sin_t):
        outs = []
        for c in range(HEAD_PAIRS):
            zc = z[:, c * LANES:(c + 1) * LANES]
            outs.append(zc * cos_t + pltpu.roll(zc, HEAD_DIM, 1) * sin_t)
        return jnp.concatenate(outs, axis=1)

    def normed(c):
        x = x_ref[c * RC_IN:(c + 1) * RC_IN, :]
        ms = jnp.mean(x * x, axis=-1, keepdims=True)
        return (x * lax.rsqrt(ms + EPS) * pg_ref[...]).astype(BF16)

    wide_cols = ((0, A_COLS), (A_COLS, A_COLS + R_COLS), (A_COLS + R_COLS, D_IN_MAIN))
    contract_last = (((1,), (1,)), ((), ()))

    def wide(h, k):
        z = lax.dot_general(h, w_ref[wide_cols[k][0]:wide_cols[k][1], :], contract_last,
                            preferred_element_type=F32)
        if k == 2:
            z_flg = lax.dot_general(h, wflg_ref[...], contract_last, preferred_element_type=F32)
            z = jnp.concatenate([z, z_flg], axis=1)
        return z

    def rows_of(c):
        return slice(c * RC_IN, (c + 1) * RC_IN)

    def a1(c, za, e):
        e["u"] = _gelu_tanh(za[:, 0:A_WIDTH])
        e["v"] = _gelu_tanh(za[:, A_WIDTH:2 * A_WIDTH])
        e["g"] = _silu(za[:, 2 * A_WIDTH:3 * A_WIDTH])
        e["mean"] = _dot(e["v"].astype(BF16), mavg)

    def a2(c, za, e):
        e["d"] = e["v"] - e["mean"]
        e["var"] = _dot((e["d"] * e["d"]).astype(BF16), mavg)

    def a3(c, za, e):
        vn = (e["d"] * lax.rsqrt(e["var"] + EPS) * aln_ref[...]).astype(BF16)
        mixed_blocks = []
        for nb in range(RC_IN // A_BLOCK):
            vb = vn[nb * A_BLOCK:(nb + 1) * A_BLOCK, :]
            vstack = jnp.concatenate(
                [jnp.where(lane_grp == gi, vb, zero_bf) for gi in range(A_GROUPS)], axis=0)
            mixed_blocks.append(_dot(wcat, vstack) + abias_ref[...])
        e["mixed"] = jnp.concatenate(mixed_blocks, axis=0)

    def a4(c, za, e):
        a_ref[rows_of(c), :] = (e["u"] * e["mixed"] * e["g"]).astype(BF16)

    def r1(c, zr, e):
        rows = rows_of(c)
        cos_t = cos_ref[rows, :]
        sin_t = sin_ref[rows, :]
        rq_ref[rows, :] = (rope(zr[:, 0:R_WIDTH], cos_t, sin_t) * (HEAD_DIM ** -0.5)).astype(BF16)
        rk_ref[rows, :] = rope(zr[:, R_WIDTH:2 * R_WIDTH], cos_t, sin_t).astype(BF16)
        rv_ref[rows, :] = zr[:, 2 * R_WIDTH:3 * R_WIDTH].astype(BF16)
        rg_ref[rows, :] = _silu(zr[:, 3 * R_WIDTH:4 * R_WIDTH]).astype(BF16)

    def f1(c, zf, e):
        ls = _log_sigmoid(zf[:, 4 * F_WIDTH:4 * F_WIDTH + LANES] + bf_ref[...])
        ls_hi = ls.astype(BF16)
        ls_lo = (ls - ls_hi.astype(F32)).astype(BF16)
        cf = _dot(tri, ls_hi) + _dot(tri, ls_lo) + carry_ref[...]
        carry_ref[...] = cf[RC_IN - 1:RC_IN, :]
        e["cf2"] = cf * LOG2E

    def f2(c, zf, e):
        cf2 = e["cf2"]
        cft_ref[:, rows_of(c)] = cf2.T[:8, :]
        negf = -cf2
        hi = negf.astype(BF16).astype(F32)
        mid = (negf - hi).astype(BF16).astype(F32)
        lo = (negf - hi - mid).astype(BF16).astype(F32)
        packed = jnp.where(lane < 8, hi,
                           jnp.where(lane < 16, pltpu.roll(mid, 8, 1),
                                     jnp.where(lane < 24, pltpu.roll(lo, 16, 1), 0.0)))
        e["placed"] = _dot(packed.astype(BF16), place_ref[...])

    def f3(c, zf, e):
        rows = rows_of(c)
        fq_ref[:, rows] = (zf[:, 0:F_WIDTH] * (HEAD_DIM ** -0.5 * LOG2E)).T.astype(BF16)
        fg_ref[rows, :] = _silu(zf[:, 3 * F_WIDTH:4 * F_WIDTH]).astype(BF16)
        fv_t = zf[:, 2 * F_WIDTH:3 * F_WIDTH].T.astype(BF16)
        for hd in range(F_HEADS):
            p, hh = divmod(hd, 2)
            kk = zf[:, F_WIDTH + p * LANES:F_WIDTH + (p + 1) * LANES]
            if hh == 1:
                kk = pltpu.roll(kk, HEAD_DIM, 1)
            aug = e["placed"][:, hd * LANES:(hd + 1) * LANES]
            kaug_ref[hd, rows, :] = jnp.where(lane < HEAD_DIM, kk, aug).astype(BF16)
            vaug_ref[hd, 0:HEAD_DIM, rows] = fv_t[hd * HEAD_DIM:(hd + 1) * HEAD_DIM, :]
            vaug_ref[hd, HEAD_DIM:VAUG_ROWS, rows] = ones_rows

    nchunks = tm // RC_IN
    hs = [normed(c) for c in range(nchunks)]
    groups = ((0, (a1, a2, a3, a4)), (2, (f1, f2, f3)), (1, (r1,)))
    issued = []

    def run_slot(t):
        for s, (c, z, stages, e) in enumerate(issued):
            j = t - 1 - s
            if 0 <= j < len(stages):
                stages[j](c, z, e)

    for k, stages in groups:
        for c in range(nchunks):
            issued.append((c, wide(hs[c], k), stages, {}))
            run_slot(len(issued) - 1)
    depth = max(len(stages) for _, stages in groups)
    for t in range(len(issued), len(issued) + depth):
        run_slot(t)


def _inproj(x, pre_g, w_main, w_flg, bf_pad, aln, aws, abias, cos_t, sin_t, tri, mavg, place):
    b, s, _ = x.shape
    tm = TM_IN
    grid = (b, s // tm)
    row_spec = lambda n: pl.BlockSpec((None, tm, n), lambda bi, si: (bi, si, 0))
    col_spec = lambda n: pl.BlockSpec((None, n, tm), lambda bi, si: (bi, 0, si))
    const2 = lambda shp: pl.BlockSpec(shp, lambda bi, si: (0, 0))
    bf_sds = lambda n: jax.ShapeDtypeStruct((b, s, n), BF16)
    out_shape = ([bf_sds(A_WIDTH)] + [bf_sds(R_WIDTH)] * 4
                 + [jax.ShapeDtypeStruct((b, F_WIDTH, s), BF16),
                    jax.ShapeDtypeStruct((b, F_HEADS, s, LANES), BF16),
                    jax.ShapeDtypeStruct((b, F_HEADS, VAUG_ROWS, s), BF16),
                    bf_sds(F_WIDTH),
                    jax.ShapeDtypeStruct((b, 8, s), F32)])
    out_specs = ([row_spec(A_WIDTH)] + [row_spec(R_WIDTH)] * 4
                 + [col_spec(F_WIDTH),
                    pl.BlockSpec((None, F_HEADS, tm, LANES), lambda bi, si: (bi, 0, si, 0)),
                    pl.BlockSpec((None, F_HEADS, VAUG_ROWS, tm), lambda bi, si: (bi, 0, 0, si)),
                    row_spec(F_WIDTH),
                    col_spec(8)])
    in_specs = [
        row_spec(D_MODEL),
        const2((1, D_MODEL)),
        pl.BlockSpec((D_IN_MAIN, D_MODEL), lambda bi, si: (0, 0), pipeline_mode=pl.Buffered(1)),
        const2((LANES, D_MODEL)),
        const2((1, LANES)),
        const2((1, A_WIDTH)),
        pl.BlockSpec((A_GROUPS, A_BLOCK, A_BLOCK), lambda bi, si: (0, 0, 0)),
        const2((A_BLOCK, A_WIDTH)),
        pl.BlockSpec((tm, LANES), lambda bi, si: (si, 0)),
        pl.BlockSpec((tm, LANES), lambda bi, si: (si, 0)),
        const2((RC_IN, RC_IN)),
        const2((A_WIDTH, A_WIDTH)),
        const2((LANES, F_HEADS * LANES)),
    ]
    return pl.pallas_call(
        _inproj_body,
        grid=grid,
        in_specs=in_specs,
        out_specs=out_specs,
        out_shape=out_shape,
        scratch_shapes=[pltpu.VMEM((1, LANES), F32)],
        compiler_params=pltpu.CompilerParams(
            dimension_semantics=("arbitrary", "arbitrary"), vmem_limit_bytes=VMEM_LIMIT),
        name="inproj",
    )(x, pre_g, w_main, w_flg, bf_pad, aln, aws, abias, cos_t, sin_t, tri, mavg, place)


def _retention_body(q_ref, k_ref, v_ref, g_ref, dmat_ref, qdec_ref, kdec_ref, sdec_ref, bmask_ref,
                    mavg_ref, o_ref, state_ref):
    tm = q_ref.shape[0]

    @pl.when(pl.program_id(1) == 0)
    def _():
        state_ref[...] = jnp.zeros_like(state_ref)

    lane = lax.broadcasted_iota(jnp.int32, (1, LANES), 1)
    head0_v = lane < HEAD_DIM
    head0_qk = jnp.bitwise_and(lane, HEAD_DIM // 2) == 0
    mavg = mavg_ref[...]
    bmask = bmask_ref[...]
    contract_last = (((1,), (1,)), ((), ()))
    contract_first = (((0,), (0,)), ((), ()))
    nblk = tm // L_RET
    units = [(p, j) for p in range(HEAD_PAIRS) for j in range(nblk)]

    def split_heads(x2, head0):
        zero = jnp.zeros_like(x2)
        return jnp.concatenate([jnp.where(head0, x2, zero), jnp.where(head0, zero, x2)], axis=0)

    q2, v2, sc, kv = {}, {}, {}, {}
    for (p, j) in units:
        rs, cs = slice(j * L_RET, (j + 1) * L_RET), slice(p * LANES, (p + 1) * LANES)
        q2[p, j] = q_ref[rs, cs]
        k2 = k_ref[rs, cs]
        v2[p, j] = v_ref[rs, cs]
        sc[p, j] = lax.dot_general(q2[p, j], split_heads(k2, head0_qk), contract_last,
                                   preferred_element_type=F32)
        kd = (k2.astype(F32) * kdec_ref[p]).astype(BF16)
        kv[p, j] = lax.dot_general(kd, v2[p, j], contract_first, preferred_element_type=F32)

    st = {}
    for p in range(HEAD_PAIRS):
        state = state_ref[p]
        for j in range(nblk):
            st[p, j] = state.astype(BF16)
            state = state * sdec_ref[p] + kv[p, j] * bmask
        state_ref[p] = state

    out = {}
    for (p, j) in units:
        pm = (sc[p, j] * dmat_ref[p]).astype(BF16)
        out[p, j] = (_dot(pm, split_heads(v2[p, j], head0_v))
                     + _dot(q2[p, j], st[p, j]) * qdec_ref[p])

    tiles = [jnp.concatenate([out[p, j] for j in range(nblk)], axis=0) for p in range(HEAD_PAIRS)]
    means = [_dot(t.astype(BF16), mavg) for t in tiles]
    devs = [t - mu for t, mu in zip(tiles, means)]
    variances = [_dot((d * d).astype(BF16), mavg) for d in devs]
    for p in range(HEAD_PAIRS):
        cs = slice(p * LANES, (p + 1) * LANES)
        y = devs[p] * lax.rsqrt(variances[p] + EPS) * g_ref[:, cs].astype(F32)
        o_ref[:, cs] = y.astype(BF16)


def _retention(rq, rk, rv, rg, dmat, qdec, kdec, sdec, bmask, mavg128):
    b, s, _ = rq.shape
    tm = TM_RET
    row_spec = pl.BlockSpec((None, tm, R_WIDTH), lambda bi, si: (bi, si, 0))
    c3 = lambda shp: pl.BlockSpec(shp, lambda bi, si: (0, 0, 0))
    c2 = lambda shp: pl.BlockSpec(shp, lambda bi, si: (0, 0))
    return pl.pallas_call(
        _retention_body,
        grid=(b, s // tm),
        in_specs=[row_spec, row_spec, row_spec, row_spec,
                  c3((HEAD_PAIRS, L_RET, 2 * L_RET)), c3((HEAD_PAIRS, L_RET, LANES)),
                  c3((HEAD_PAIRS, L_RET, LANES)), c3((HEAD_PAIRS, LANES, LANES)),
                  c2((LANES, LANES)), c2((LANES, LANES))],
        out_specs=row_spec,
        out_shape=jax.ShapeDtypeStruct((b, s, R_WIDTH), BF16),
        scratch_shapes=[pltpu.VMEM((HEAD_PAIRS, LANES, LANES), F32)],
        compiler_params=pltpu.CompilerParams(
            dimension_semantics=("arbitrary", "arbitrary"), vmem_limit_bytes=VMEM_LIMIT),
        name="retention",
    )(rq, rk, rv, rg, dmat, qdec, kdec, sdec, bmask, mavg128)


def _fox_body(qt_ref, kaug_ref, vaug_ref, cft_ref, g_ref, o_ref, m_ref, acc_ref, s_ref):
    qi = pl.program_id(1)
    srow_q = lax.broadcasted_iota(jnp.int32, (HEAD_DIM, 1), 0)
    q_ones = jnp.broadcast_to(jnp.where(srow_q < 3, 1.0, 0.0), (HEAD_DIM, TQ)).astype(BF16)
    qta = [jnp.concatenate([qt_ref[h * HEAD_DIM:(h + 1) * HEAD_DIM, :], q_ones], axis=0)
           for h in range(F_HEADS)]
    fq = [cft_ref[h:h + 1, :] for h in range(F_HEADS)]
    krow = lax.broadcasted_iota(jnp.int32, (TK, TQ), 0)
    qcol = lax.broadcasted_iota(jnp.int32, (TK, TQ), 1)
    causal = (krow <= qcol, krow + TK <= qcol)

    for h in range(F_HEADS):
        m_ref[h] = jnp.full((8, TQ), MASK_VALUE, F32)
        acc_ref[h] = jnp.zeros((VAUG_ROWS, TQ), F32)

    def scores(kj, dst_ref, h, last=False):
        k0 = pl.multiple_of(kj * TK, TK)
        kt = kaug_ref[h, pl.ds(k0, TK), :]
        if last:
            dst_ref[h, :, TK:TQ] = _dot(kt, qta[h][:, TK:TQ])
        else:
            dst_ref[h] = _dot(kt, qta[h])

    def consume(kj, src_ref, h, diagonal):
        k0 = pl.multiple_of(kj * TK, TK)
        st = src_ref[h]
        if diagonal:
            st = jnp.where(causal[0], st, MASK_VALUE)
        m_old = m_ref[h][0:1, :]
        m_new = jnp.maximum(m_old, jnp.max(st, axis=0, keepdims=True) + fq[h])
        p = jnp.exp2(st - (m_new - fq[h]))
        alpha = jnp.exp2(m_old - m_new)
        pv = _dot(vaug_ref[h, :, pl.ds(k0, TK)], p.astype(BF16))
        acc_ref[h] = alpha * acc_ref[h] + pv
        m_ref[h] = jnp.broadcast_to(m_new, (8, TQ))

    def consume_last(kj, src_ref, h):
        k0 = pl.multiple_of(kj * TK, TK)
        st = jnp.where(causal[1][:, TK:TQ], src_ref[h, :, TK:TQ], MASK_VALUE)
        m_old = m_ref[h][0:1, TK:TQ]
        fq_h = fq[h][:, TK:TQ]
        m_new = jnp.maximum(m_old, jnp.max(st, axis=0, keepdims=True) + fq_h)
        p = jnp.exp2(st - (m_new - fq_h))
        alpha = jnp.exp2(m_old - m_new)
        pv = _dot(vaug_ref[h, :, pl.ds(k0, TK)], p.astype(BF16))
        acc_ref[h, :, TK:TQ] = alpha * acc_ref[h, :, TK:TQ] + pv

    lead = 2

    def tile_step(kj, diagonal=False, last=False, next_tile=None):
        for h in range(F_HEADS):
            ahead = h + lead
            if ahead < F_HEADS:
                scores(kj, s_ref, ahead, last)
            elif next_tile is not None:
                scores(kj + 1, s_ref, ahead - F_HEADS, next_tile == "last")
            if last:
                consume_last(kj, s_ref, h)
            else:
                consume(kj, s_ref, h, diagonal)

    for h in range(lead):
        scores(0, s_ref, h)

    def body(i, carry):
        tile_step(2 * i, next_tile="full")
        tile_step(2 * i + 1, next_tile="full")
        return carry

    lax.fori_loop(0, qi, body, 0)
    tile_step(2 * qi, diagonal=True, next_tile="last")
    tile_step(2 * qi + 1, last=True)

    for p in range(HEAD_PAIRS):
        halves = []
        for hh in range(2):
            acc = acc_ref[2 * p + hh]
            halves.append(acc[0:HEAD_DIM, :] / acc[HEAD_DIM:HEAD_DIM + 1, :])
        out_t = jnp.concatenate(halves, axis=0)
        cs = slice(p * LANES, (p + 1) * LANES)
        o_ref[:, cs] = (out_t.T * g_ref[:, cs].astype(F32)).astype(BF16)


def _fox(fqt, kaug, vaug, fg, cf_t):
    b, _, s = fqt.shape
    assert TQ == 2 * TK
    row_spec = pl.BlockSpec((None, TQ, F_WIDTH), lambda bi, qi: (bi, qi, 0))
    return pl.pallas_call(
        _fox_body,
        grid=(b, s // TQ),
        in_specs=[pl.BlockSpec((None, F_WIDTH, TQ), lambda bi, qi: (bi, 0, qi)),
                  pl.BlockSpec((None, F_HEADS, s, LANES), lambda bi, qi: (bi, 0, 0, 0)),
                  pl.BlockSpec((None, F_HEADS, VAUG_ROWS, s), lambda bi, qi: (bi, 0, 0, 0)),
                  pl.BlockSpec((None, 8, TQ), lambda bi, qi: (bi, 0, qi)),
                  row_spec],
        out_specs=row_spec,
        out_shape=jax.ShapeDtypeStruct((b, s, F_WIDTH), BF16),
        scratch_shapes=[pltpu.VMEM((F_HEADS, 8, TQ), F32),
                        pltpu.VMEM((F_HEADS, VAUG_ROWS, TQ), F32),
                        pltpu.VMEM((F_HEADS, TK, TQ), F32)],
        compiler_params=pltpu.CompilerParams(
            dimension_semantics=("arbitrary", "arbitrary"), vmem_limit_bytes=VMEM_LIMIT),
        name="fox",
    )(fqt, kaug, vaug, cf_t, fg)


def _outproj_body(a_ref, r_ref, f_ref, w_ref, x_ref, pg_ref, o_ref):
    tm = x_ref.shape[0]

    def project(c):
        rows = slice(c * RC_OUT, (c + 1) * RC_OUT)
        y = jnp.concatenate([a_ref[rows, :], r_ref[rows, :], f_ref[rows, :]], axis=1)
        return _dot(y, w_ref[...])

    nchunks = tm // RC_OUT
    o_next = project(0)
    for c in range(nchunks):
        rows = slice(c * RC_OUT, (c + 1) * RC_OUT)
        o = o_next
        if c + 1 < nchunks:
            o_next = project(c + 1)
        ms = jnp.mean(o * o, axis=-1, keepdims=True)
        o_ref[rows, :] = x_ref[rows, :] + o * lax.rsqrt(ms + EPS) * pg_ref[...]


def _outproj(a, r, f, w, x, post_g):
    b, s, _ = x.shape
    tm = TM_OUT
    row_spec = lambda n: pl.BlockSpec((None, tm, n), lambda bi, si: (bi, si, 0))
    c2 = lambda shp: pl.BlockSpec(shp, lambda bi, si: (0, 0))
    return pl.pallas_call(
        _outproj_body,
        grid=(b, s // tm),
        in_specs=[row_spec(A_WIDTH), row_spec(R_WIDTH), row_spec(F_WIDTH),
                  c2((D_MODEL, D_MODEL)), row_spec(D_MODEL), c2((1, D_MODEL))],
        out_specs=row_spec(D_MODEL),
        out_shape=jax.ShapeDtypeStruct((b, s, D_MODEL), F32),
        compiler_params=pltpu.CompilerParams(
            dimension_semantics=("arbitrary", "arbitrary"), vmem_limit_bytes=VMEM_LIMIT),
        name="outproj",
    )(a, r, f, w, x, post_g)


@functools.lru_cache(maxsize=None)
def _tables(seq):
    half = HEAD_DIM // 2
    lane = np.arange(LANES)
    inv = ROPE_THETA ** (-(np.arange(half, dtype=np.float64) / half))
    ang = np.arange(seq, dtype=np.float64)[:, None] * inv[None, :]
    cos = np.cos(ang)
    sin = np.sin(ang)
    cos_t = cos[:, lane % half]
    sin_t = sin[:, lane % half] * np.where(lane < HEAD_DIM, -1.0, 1.0)[None, :]

    gam = 1.0 - np.exp2(-5.0 - np.arange(R_HEADS, dtype=np.float64))
    log_gam = np.log(gam)
    pos = np.arange(L_RET)
    dist = np.abs(pos[:, None] - pos[None, :])
    allowed = (pos[None, :] // CHUNK) <= (pos[:, None] // CHUNK)
    dmat = np.where(allowed[None], np.exp(log_gam[:, None, None] * dist[None]), 0.0)
    dmat = np.concatenate([dmat[0::2], dmat[1::2]], axis=2)
    head_of_lane = lane // HEAD_DIM
    head_of_qk_lane = (lane // half) % 2
    qdec = np.stack([np.exp(log_gam[2 * p + head_of_lane][None, :] * (pos + 1.0)[:, None])
                     for p in range(HEAD_PAIRS)])
    kdec = np.stack([np.exp(log_gam[2 * p + head_of_qk_lane][None, :] * (L_RET - 1.0 - pos)[:, None])
                     for p in range(HEAD_PAIRS)])
    bmask = (head_of_qk_lane[:, None] == head_of_lane[None, :]).astype(np.float32)
    sdec = np.stack([np.exp(log_gam[2 * p + head_of_qk_lane] * L_RET)[:, None] * bmask
                     for p in range(HEAD_PAIRS)])
    tri = (np.arange(RC_IN)[None, :] <= np.arange(RC_IN)[:, None]).astype(np.float32)
    grp = np.arange(A_WIDTH) // HEAD_DIM
    mavg = (grp[:, None] == grp[None, :]).astype(np.float32) / HEAD_DIM
    place = np.zeros((LANES, F_HEADS * LANES), np.float32)
    for h in range(F_HEADS):
        for term in range(3):
            place[8 * term + h, h * LANES + HEAD_DIM + term] = 1.0
    f = lambda a: np.asarray(a, np.float32)
    return dict(cos=f(cos_t), sin=f(sin_t), dmat=f(dmat), qdec=f(qdec), kdec=f(kdec),
                sdec=f(sdec), bmask=f(bmask), tri=f(tri), mavg=f(mavg),
                mavg128=f(mavg[:LANES, :LANES]), place=f(place))


def _layer(x, pre_g, post_g, w_in, b_f, a_ln_g, a_ws, a_bs, w_out, t):
    half = HEAD_DIM // 2
    w_t = w_in.T.astype(BF16)
    qk = w_t[A_COLS:A_COLS + 2 * R_WIDTH].reshape(2 * HEAD_PAIRS, 2, 2, half, D_MODEL)
    qk = jnp.swapaxes(qk, 1, 2).reshape(2 * R_WIDTH, D_MODEL)
    w_main = jnp.concatenate([w_t[:A_COLS], qk, w_t[A_COLS + 2 * R_WIDTH:D_IN_MAIN]], axis=0)
    w_flg = jnp.pad(w_t[D_IN_MAIN:], ((0, LANES - F_HEADS), (0, 0)))
    bf_pad = jnp.pad(b_f, (0, LANES - F_HEADS)).reshape(1, LANES)
    abias = jnp.repeat(a_bs.T, HEAD_DIM, axis=1)
    (a_out, rq, rk, rv, rg, fqt, kaug, vaug, fg, cf_t) = _inproj(
        x, pre_g.reshape(1, D_MODEL), w_main, w_flg, bf_pad, a_ln_g.reshape(1, A_WIDTH), a_ws, abias,
        t["cos"], t["sin"], t["tri"], t["mavg"], t["place"])
    r_out = _retention(rq, rk, rv, rg, t["dmat"], t["qdec"], t["kdec"], t["sdec"], t["bmask"],
                       t["mavg128"])
    f_out = _fox(fqt, kaug, vaug, fg, cf_t)
    return _outproj(a_out, r_out, f_out, w_out.astype(BF16), x, post_g.reshape(1, D_MODEL))


def kernel(x, pre_gain, post_gain, w_in, b_forget, a_norm_gain, a_spatial_w, a_spatial_b, w_out):
    bf16_tables = ("tri", "mavg", "mavg128", "place")
    t = {k: jnp.asarray(v, BF16 if k in bf16_tables else F32) for k, v in _tables(x.shape[1]).items()}
    for l in range(pre_gain.shape[0]):
        x = _layer(x, pre_gain[l], post_gain[l], w_in[l], b_forget[l], a_norm_gain[l],
                   a_spatial_w[l], a_spatial_b[l], w_out[l], t)
    return x
```

```python
import functools
import math

import jax
import jax.numpy as jnp
import numpy as np
from jax import lax
from jax.experimental import pallas as pl
from jax.experimental.pallas import tpu as pltpu

F32 = jnp.float32
BF16 = jnp.bfloat16

D_MODEL = 1024
HEAD_DIM = 64
CHUNK = 64
A_WIDTH = 256
R_WIDTH = 384
F_WIDTH = 384
A_GROUPS = 4
R_HEADS = 6
F_HEADS = 6
A_BLOCK = 128
ROPE_THETA = 10000.0
EPS = 1e-6
LANES = 128
HEAD_PAIRS = R_WIDTH // LANES

A_COLS = 3 * A_WIDTH
R_COLS = 4 * R_WIDTH
F_COLS = 4 * F_WIDTH + F_HEADS
D_IN = A_COLS + R_COLS + F_COLS
D_IN_MAIN = A_COLS + R_COLS + 4 * F_WIDTH

TM_IN = 1024
RC_IN = 512
TRI_ROWS = 256
TM_RET = 2048
L_RET = 128
TQ = 512
TK = 256
TM_OUT = 2048
RC_OUT = 256
VMEM_LIMIT = 56 * 1024 * 1024
VAUG_ROWS = HEAD_DIM + 16
LOG2E = 1.4426950408889634
MASK_VALUE = -1e30


def _silu(x):
    return 0.5 * x * (1.0 + jnp.tanh(0.5 * x))


def _gelu_tanh(x):
    c = math.sqrt(2.0 / math.pi)
    return 0.5 * x * (1.0 + jnp.tanh(c * (x + 0.044715 * (x * x * x))))


def _log_sigmoid(x):
    return jnp.minimum(x, 0.0) - jnp.log1p(jnp.exp(-jnp.abs(x)))


def _dot(a, b):
    return jnp.dot(a, b, preferred_element_type=F32)


def _group_norm64(x, mavg):
    mean = _dot(x.astype(BF16), mavg)
    d = x - mean
    var = _dot((d * d).astype(BF16), mavg)
    return d * lax.rsqrt(var + EPS)


def _inproj_body(x_ref, pg_ref, w_ref, wflg_ref, bf_ref, aln_ref, aws_ref, abias_ref, cos_ref, sin_ref,
                 tri_ref, mavg_ref, place_ref,
                 a_ref, rq_ref, rk_ref, rv_ref, rg_ref, fq_ref, kaug_ref, vaug_ref, fg_ref, cft_ref,
                 carry_ref):
    tm = x_ref.shape[0]

    @pl.when(pl.program_id(1) == 0)
    def _():
        carry_ref[...] = jnp.zeros_like(carry_ref)

    mavg = mavg_ref[...]
    tri = tri_ref[...]
    row = lax.broadcasted_iota(jnp.int32, (A_BLOCK, A_BLOCK), 0)
    col = lax.broadcasted_iota(jnp.int32, (A_BLOCK, A_BLOCK), 1)
    allowed = jnp.logical_not(jnp.logical_and(row < CHUNK, col >= CHUNK))
    wcat = jnp.concatenate(
        [jnp.where(allowed, aws_ref[gi], 0.0).astype(BF16) for gi in range(A_GROUPS)], axis=1)
    lane_grp = lax.shift_right_logical(
        lax.broadcasted_iota(jnp.int32, (A_BLOCK, A_WIDTH), 1), HEAD_DIM.bit_length() - 1)
    zero_bf = jnp.zeros((A_BLOCK, A_WIDTH), BF16)
    lane = lax.broadcasted_iota(jnp.int32, (1, LANES), 1)
    srow = lax.broadcasted_iota(jnp.int32, (VAUG_ROWS - HEAD_DIM, 1), 0)
    ones_rows = jnp.broadcast_to(jnp.where(srow == 0, 1.0, 0.0),
                                 (VAUG_ROWS - HEAD_DIM, RC_IN)).astype(BF16)

    def rope(z, cos_t, sin_t):
        outs = []
        for c in range(HEAD_PAIRS):
            zc = z[:, c * LANES:(c + 1) * LANES]
            outs.append(zc * cos_t + pltpu.roll(zc, HEAD_DIM, 1) * sin_t)
        return jnp.concatenate(outs, axis=1)

    def normed(c):
        x = x_ref[c * RC_IN:(c + 1) * RC_IN, :]
        ms = jnp.mean(x * x, axis=-1, keepdims=True)
        return (x * lax.rsqrt(ms + EPS) * pg_ref[...]).astype(BF16)

    wide_cols = ((0, A_COLS), (A_COLS, A_COLS + R_COLS), (A_COLS + R_COLS, D_IN_MAIN))
    contract_last = (((1,), (1,)), ((), ()))

    def wide(h, k):
        z = lax.dot_general(h, w_ref[wide_cols[k][0]:wide_cols[k][1], :], contract_last,
                            preferred_element_type=F32)
        if k == 2:
            z_flg = lax.dot_general(h, wflg_ref[...], contract_last, preferred_element_type=F32)
            z = jnp.concatenate([z, z_flg], axis=1)
        return z

    def rows_of(c):
        return slice(c * RC_IN, (c + 1) * RC_IN)

    def a1(c, za, e):
        e["u"] = _gelu_tanh(za[:, 0:A_WIDTH])
        e["v"] = _gelu_tanh(za[:, A_WIDTH:2 * A_WIDTH])
        e["g"] = _silu(za[:, 2 * A_WIDTH:3 * A_WIDTH])
        e["mean"] = _dot(e["v"].astype(BF16), mavg)

    def a2(c, za, e):
        e["d"] = e["v"] - e["mean"]
        e["var"] = _dot((e["d"] * e["d"]).astype(BF16), mavg)

    def a3(c, za, e):
        vn = (e["d"] * lax.rsqrt(e["var"] + EPS) * aln_ref[...]).astype(BF16)
        mixed_blocks = []
        for nb in range(RC_IN // A_BLOCK):
            vb = vn[nb * A_BLOCK:(nb + 1) * A_BLOCK, :]
            vstack = jnp.concatenate(
                [jnp.where(lane_grp == gi, vb, zero_bf) for gi in range(A_GROUPS)], axis=0)
            mixed_blocks.append(_dot(wcat, vstack) + abias_ref[...])
        e["mixed"] = jnp.concatenate(mixed_blocks, axis=0)

    def a4(c, za, e):
        a_ref[rows_of(c), :] = (e["u"] * e["mixed"] * e["g"]).astype(BF16)

    def r1(c, zr, e):
        rows = rows_of(c)
        cos_t = cos_ref[rows, :]
        sin_t = sin_ref[rows, :]
        rq_ref[rows, :] = (rope(zr[:, 0:R_WIDTH], cos_t, sin_t) * (HEAD_DIM ** -0.5)).astype(BF16)
        rk_ref[rows, :] = rope(zr[:, R_WIDTH:2 * R_WIDTH], cos_t, sin_t).astype(BF16)
        rv_ref[rows, :] = zr[:, 2 * R_WIDTH:3 * R_WIDTH].astype(BF16)
        rg_ref[rows, :] = _silu(zr[:, 3 * R_WIDTH:4 * R_WIDTH]).astype(BF16)

    def f1(c, zf, e):
        ls = _log_sigmoid(zf[:, 4 * F_WIDTH:4 * F_WIDTH + LANES] + bf_ref[...])
        ls_hi = ls.astype(BF16)
        ls_lo = (ls - ls_hi.astype(F32)).astype(BF16)
        carry = carry_ref[...]
        blocks = []
        for blk in range(RC_IN // TRI_ROWS):
            rs = slice(blk * TRI_ROWS, (blk + 1) * TRI_ROWS)
            cf_blk = _dot(tri, ls_hi[rs]) + _dot(tri, ls_lo[rs]) + carry
            carry = cf_blk[TRI_ROWS - 1:TRI_ROWS, :]
            blocks.append(cf_blk)
        carry_ref[...] = carry
        e["cf2"] = jnp.concatenate(blocks, axis=0) * LOG2E

    def f2(c, zf, e):
        cf2 = e["cf2"]
        cft_ref[:, rows_of(c)] = cf2.T[:8, :]
        negf = -cf2
        hi = negf.astype(BF16).astype(F32)
        mid = (negf - hi).astype(BF16).astype(F32)
        lo = (negf - hi - mid).astype(BF16).astype(F32)
        packed = jnp.where(lane < 8, hi,
                           jnp.where(lane < 16, pltpu.roll(mid, 8, 1),
                                     jnp.where(lane < 24, pltpu.roll(lo, 16, 1), 0.0)))
        e["placed"] = _dot(packed.astype(BF16), place_ref[...])

    def f3(c, zf, e):
        rows = rows_of(c)
        fq_ref[:, rows] = (zf[:, 0:F_WIDTH] * (HEAD_DIM ** -0.5 * LOG2E)).T.astype(BF16)
        fg_ref[rows, :] = _silu(zf[:, 3 * F_WIDTH:4 * F_WIDTH]).astype(BF16)
        fv_t = zf[:, 2 * F_WIDTH:3 * F_WIDTH].T.astype(BF16)
        for hd in range(F_HEADS):
            p, hh = divmod(hd, 2)
            kk = zf[:, F_WIDTH + p * LANES:F_WIDTH + (p + 1) * LANES]
            if hh == 1:
                kk = pltpu.roll(kk, HEAD_DIM, 1)
            aug = e["placed"][:, hd * LANES:(hd + 1) * LANES]
            kaug_ref[hd, rows, :] = jnp.where(lane < HEAD_DIM, kk, aug).astype(BF16)
            vaug_ref[hd, 0:HEAD_DIM, rows] = fv_t[hd * HEAD_DIM:(hd + 1) * HEAD_DIM, :]
            vaug_ref[hd, HEAD_DIM:VAUG_ROWS, rows] = ones_rows

    nchunks = tm // RC_IN
    hs = [normed(c) for c in range(nchunks)]
    groups = ((0, (a1, a2, a3, a4)), (2, (f1, f2, f3)), (1, (r1,)))
    issued = []

    def run_slot(t):
        for s, (c, z, stages, e) in enumerate(issued):
            j = t - 1 - s
            if 0 <= j < len(stages):
                stages[j](c, z, e)

    for k, stages in groups:
        for c in range(nchunks):
            issued.append((c, wide(hs[c], k), stages, {}))
            run_slot(len(issued) - 1)
    depth = max(len(stages) for _, stages in groups)
    for t in range(len(issued), len(issued) + depth):
        run_slot(t)


def _inproj(x, pre_g, w_main, w_flg, bf_pad, aln, aws, abias, cos_t, sin_t, tri, mavg, place):
    b, s, _ = x.shape
    tm = TM_IN
    grid = (b, s // tm)
    row_spec = lambda n: pl.BlockSpec((None, tm, n), lambda bi, si: (bi, si, 0))
    col_spec = lambda n: pl.BlockSpec((None, n, tm), lambda bi, si: (bi, 0, si))
    const2 = lambda shp: pl.BlockSpec(shp, lambda bi, si: (0, 0))
    bf_sds = lambda n: jax.ShapeDtypeStruct((b, s, n), BF16)
    out_shape = ([bf_sds(A_WIDTH)] + [bf_sds(R_WIDTH)] * 4
                 + [jax.ShapeDtypeStruct((b, F_WIDTH, s), BF16),
                    jax.ShapeDtypeStruct((b, F_HEADS, s, LANES), BF16),
                    jax.ShapeDtypeStruct((b, F_HEADS, VAUG_ROWS, s), BF16),
                    bf_sds(F_WIDTH),
                    jax.ShapeDtypeStruct((b, 8, s), F32)])
    out_specs = ([row_spec(A_WIDTH)] + [row_spec(R_WIDTH)] * 4
                 + [col_spec(F_WIDTH),
                    pl.BlockSpec((None, F_HEADS, tm, LANES), lambda bi, si: (bi, 0, si, 0)),
                    pl.BlockSpec((None, F_HEADS, VAUG_ROWS, tm), lambda bi, si: (bi, 0, 0, si)),
                    row_spec(F_WIDTH),
                    col_spec(8)])
    in_specs = [
        row_spec(D_MODEL),
        const2((1, D_MODEL)),
        pl.BlockSpec((D_IN_MAIN, D_MODEL), lambda bi, si: (0, 0), pipeline_mode=pl.Buffered(1)),
        const2((LANES, D_MODEL)),
        const2((1, LANES)),
        const2((1, A_WIDTH)),
        pl.BlockSpec((A_GROUPS, A_BLOCK, A_BLOCK), lambda bi, si: (0, 0, 0)),
        const2((A_BLOCK, A_WIDTH)),
        pl.BlockSpec((tm, LANES), lambda bi, si: (si, 0)),
        pl.BlockSpec((tm, LANES), lambda bi, si: (si, 0)),
        const2((TRI_ROWS, TRI_ROWS)),
        const2((A_WIDTH, A_WIDTH)),
        const2((LANES, F_HEADS * LANES)),
    ]
    return pl.pallas_call(
        _inproj_body,
        grid=grid,
        in_specs=in_specs,
        out_specs=out_specs,
        out_shape=out_shape,
        scratch_shapes=[pltpu.VMEM((1, LANES), F32)],
        compiler_params=pltpu.CompilerParams(
            dimension_semantics=("arbitrary", "arbitrary"), vmem_limit_bytes=VMEM_LIMIT),
        name="inproj",
    )(x, pre_g, w_main, w_flg, bf_pad, aln, aws, abias, cos_t, sin_t, tri, mavg, place)


def _retention_body(q_ref, k_ref, v_ref, g_ref, dmat_ref, qdec_ref, kdec_ref, sdec_ref, bmask_ref,
                    mavg_ref, o_ref, state_ref):
    tm = q_ref.shape[0]

    @pl.when(pl.program_id(1) == 0)
    def _():
        state_ref[...] = jnp.zeros_like(state_ref)

    lane = lax.broadcasted_iota(jnp.int32, (1, LANES), 1)
    head0_v = lane < HEAD_DIM
    head0_qk = jnp.bitwise_and(lane, HEAD_DIM // 2) == 0
    mavg = mavg_ref[...]
    bmask = bmask_ref[...]
    contract_last = (((1,), (1,)), ((), ()))
    contract_first = (((0,), (0,)), ((), ()))
    nblk = tm // L_RET
    units = [(p, j) for p in range(HEAD_PAIRS) for j in range(nblk)]

    def split_heads(x2, head0):
        zero = jnp.zeros_like(x2)
        return jnp.concatenate([jnp.where(head0, x2, zero), jnp.where(head0, zero, x2)], axis=0)

    q2, v2, sc, kv = {}, {}, {}, {}
    for (p, j) in units:
        rs, cs = slice(j * L_RET, (j + 1) * L_RET), slice(p * LANES, (p + 1) * LANES)
        q2[p, j] = q_ref[rs, cs]
        k2 = k_ref[rs, cs]
        v2[p, j] = v_ref[rs, cs]
        sc[p, j] = lax.dot_general(q2[p, j], split_heads(k2, head0_qk), contract_last,
                                   preferred_element_type=F32)
        kd = (k2.astype(F32) * kdec_ref[p]).astype(BF16)
        kv[p, j] = lax.dot_general(kd, v2[p, j], contract_first, preferred_element_type=F32)

    st = {}
    for p in range(HEAD_PAIRS):
        state = state_ref[p]
        for j in range(nblk):
            st[p, j] = state.astype(BF16)
            state = state * sdec_ref[p] + kv[p, j] * bmask
        state_ref[p] = state

    out = {}
    for (p, j) in units:
        pm = (sc[p, j] * dmat_ref[p]).astype(BF16)
        out[p, j] = (_dot(pm, split_heads(v2[p, j], head0_v))
                     + _dot(q2[p, j], st[p, j]) * qdec_ref[p])

    tiles = [jnp.concatenate([out[p, j] for j in range(nblk)], axis=0) for p in range(HEAD_PAIRS)]
    groups = ((jnp.concatenate(tiles[0:2], axis=1), mavg, slice(0, 2 * LANES)),
              (tiles[2], mavg[:LANES, :LANES], slice(2 * LANES, 3 * LANES)))
    means = [_dot(t.astype(BF16), m) for t, m, _ in groups]
    devs = [t - mu for (t, _, _), mu in zip(groups, means)]
    variances = [_dot((d * d).astype(BF16), m) for d, (_, m, _) in zip(devs, groups)]
    for d, var, (_, _, cs) in zip(devs, variances, groups):
        y = d * lax.rsqrt(var + EPS) * g_ref[:, cs].astype(F32)
        o_ref[:, cs] = y.astype(BF16)


def _retention(rq, rk, rv, rg, dmat, qdec, kdec, sdec, bmask, mavg):
    b, s, _ = rq.shape
    tm = TM_RET
    row_spec = pl.BlockSpec((None, tm, R_WIDTH), lambda bi, si: (bi, si, 0))
    c3 = lambda shp: pl.BlockSpec(shp, lambda bi, si: (0, 0, 0))
    c2 = lambda shp: pl.BlockSpec(shp, lambda bi, si: (0, 0))
    return pl.pallas_call(
        _retention_body,
        grid=(b, s // tm),
        in_specs=[row_spec, row_spec, row_spec, row_spec,
                  c3((HEAD_PAIRS, L_RET, 2 * L_RET)), c3((HEAD_PAIRS, L_RET, LANES)),
                  c3((HEAD_PAIRS, L_RET, LANES)), c3((HEAD_PAIRS, LANES, LANES)),
                  c2((LANES, LANES)), c2((2 * LANES, 2 * LANES))],
        out_specs=row_spec,
        out_shape=jax.ShapeDtypeStruct((b, s, R_WIDTH), BF16),
        scratch_shapes=[pltpu.VMEM((HEAD_PAIRS, LANES, LANES), F32)],
        compiler_params=pltpu.CompilerParams(
            dimension_semantics=("arbitrary", "arbitrary"), vmem_limit_bytes=VMEM_LIMIT),
        name="retention",
    )(rq, rk, rv, rg, dmat, qdec, kdec, sdec, bmask, mavg)


def _fox_body(qt_ref, kaug_ref, vaug_ref, cft_ref, g_ref, o_ref, m_ref, acc_ref, sa_ref, sb_ref):
    qi = pl.program_id(1)
    srow_q = lax.broadcasted_iota(jnp.int32, (HEAD_DIM, 1), 0)
    q_ones = jnp.broadcast_to(jnp.where(srow_q < 3, 1.0, 0.0), (HEAD_DIM, TQ)).astype(BF16)
    qta = [jnp.concatenate([qt_ref[h * HEAD_DIM:(h + 1) * HEAD_DIM, :], q_ones], axis=0)
           for h in range(F_HEADS)]
    fq = [cft_ref[h:h + 1, :] for h in range(F_HEADS)]
    krow = lax.broadcasted_iota(jnp.int32, (TK, TQ), 0)
    qcol = lax.broadcasted_iota(jnp.int32, (TK, TQ), 1)
    causal = (krow <= qcol, krow + TK <= qcol)

    for h in range(F_HEADS):
        m_ref[h] = jnp.full((8, TQ), MASK_VALUE, F32)
        acc_ref[h] = jnp.zeros((VAUG_ROWS, TQ), F32)

    def scores(kj, dst_ref, h, last=False):
        k0 = pl.multiple_of(kj * TK, TK)
        kt = kaug_ref[h, pl.ds(k0, TK), :]
        if last:
            dst_ref[h, :, TK:TQ] = _dot(kt, qta[h][:, TK:TQ])
        else:
            dst_ref[h] = _dot(kt, qta[h])

    def consume(kj, src_ref, h, diagonal):
        k0 = pl.multiple_of(kj * TK, TK)
        st = src_ref[h]
        if diagonal:
            st = jnp.where(causal[0], st, MASK_VALUE)
        m_old = m_ref[h][0:1, :]
        m_new = jnp.maximum(m_old, jnp.max(st, axis=0, keepdims=True) + fq[h])
        p = jnp.exp2(st - (m_new - fq[h]))
        alpha = jnp.exp2(m_old - m_new)
        pv = _dot(vaug_ref[h, :, pl.ds(k0, TK)], p.astype(BF16))
        acc_ref[h] = alpha * acc_ref[h] + pv
        m_ref[h] = jnp.broadcast_to(m_new, (8, TQ))

    def consume_last(kj, src_ref, h):
        k0 = pl.multiple_of(kj * TK, TK)
        st = jnp.where(causal[1][:, TK:TQ], src_ref[h, :, TK:TQ], MASK_VALUE)
        m_old = m_ref[h][0:1, TK:TQ]
        fq_h = fq[h][:, TK:TQ]
        m_new = jnp.maximum(m_old, jnp.max(st, axis=0, keepdims=True) + fq_h)
        p = jnp.exp2(st - (m_new - fq_h))
        alpha = jnp.exp2(m_old - m_new)
        pv = _dot(vaug_ref[h, :, pl.ds(k0, TK)], p.astype(BF16))
        acc_ref[h, :, TK:TQ] = alpha * acc_ref[h, :, TK:TQ] + pv

    def overlapped(next_j, dst_ref, cur_j, src_ref, diagonal=False, next_last=False):
        lead = 2
        for h in range(lead):
            scores(next_j, dst_ref, h, next_last)
        for h in range(F_HEADS):
            consume(cur_j, src_ref, h, diagonal)
            if h + lead < F_HEADS:
                scores(next_j, dst_ref, h + lead, next_last)

    for h in range(F_HEADS):
        scores(0, sa_ref, h)

    def body(i, carry):
        j0 = 2 * i
        overlapped(j0 + 1, sb_ref, j0, sa_ref)
        overlapped(j0 + 2, sa_ref, j0 + 1, sb_ref)
        return carry

    lax.fori_loop(0, qi, body, 0)
    overlapped(2 * qi + 1, sb_ref, 2 * qi, sa_ref, diagonal=True, next_last=True)
    for h in range(F_HEADS):
        consume_last(2 * qi + 1, sb_ref, h)

    for p in range(HEAD_PAIRS):
        halves = []
        for hh in range(2):
            acc = acc_ref[2 * p + hh]
            halves.append(acc[0:HEAD_DIM, :] / acc[HEAD_DIM:HEAD_DIM + 1, :])
        out_t = jnp.concatenate(halves, axis=0)
        cs = slice(p * LANES, (p + 1) * LANES)
        o_ref[:, cs] = (out_t.T * g_ref[:, cs].astype(F32)).astype(BF16)


def _fox(fqt, kaug, vaug, fg, cf_t):
    b, _, s = fqt.shape
    assert TQ == 2 * TK
    row_spec = pl.BlockSpec((None, TQ, F_WIDTH), lambda bi, qi: (bi, qi, 0))
    return pl.pallas_call(
        _fox_body,
        grid=(b, s // TQ),
        in_specs=[pl.BlockSpec((None, F_WIDTH, TQ), lambda bi, qi: (bi, 0, qi)),
                  pl.BlockSpec((None, F_HEADS, s, LANES), lambda bi, qi: (bi, 0, 0, 0)),
                  pl.BlockSpec((None, F_HEADS, VAUG_ROWS, s), lambda bi, qi: (bi, 0, 0, 0)),
                  pl.BlockSpec((None, 8, TQ), lambda bi, qi: (bi, 0, qi)),
                  row_spec],
        out_specs=row_spec,
        out_shape=jax.ShapeDtypeStruct((b, s, F_WIDTH), BF16),
        scratch_shapes=[pltpu.VMEM((F_HEADS, 8, TQ), F32),
                        pltpu.VMEM((F_HEADS, VAUG_ROWS, TQ), F32),
                        pltpu.VMEM((F_HEADS, TK, TQ), F32),
                        pltpu.VMEM((F_HEADS, TK, TQ), F32)],
        compiler_params=pltpu.CompilerParams(
            dimension_semantics=("arbitrary", "arbitrary"), vmem_limit_bytes=VMEM_LIMIT),
        name="fox",
    )(fqt, kaug, vaug, cf_t, fg)


def _outproj_body(a_ref, r_ref, f_ref, w_ref, x_ref, pg_ref, o_ref):
    tm = x_ref.shape[0]

    def project(c):
        rows = slice(c * RC_OUT, (c + 1) * RC_OUT)
        y = jnp.concatenate([a_ref[rows, :], r_ref[rows, :], f_ref[rows, :]], axis=1)
        return _dot(y, w_ref[...])

    nchunks = tm // RC_OUT
    o_next = project(0)
    for c in range(nchunks):
        rows = slice(c * RC_OUT, (c + 1) * RC_OUT)
        o = o_next
        if c + 1 < nchunks:
            o_next = project(c + 1)
        ms = jnp.mean(o * o, axis=-1, keepdims=True)
        o_ref[rows, :] = x_ref[rows, :] + o * lax.rsqrt(ms + EPS) * pg_ref[...]


def _outproj(a, r, f, w, x, post_g):
    b, s, _ = x.shape
    tm = TM_OUT
    row_spec = lambda n: pl.BlockSpec((None, tm, n), lambda bi, si: (bi, si, 0))
    c2 = lambda shp: pl.BlockSpec(shp, lambda bi, si: (0, 0))
    return pl.pallas_call(
        _outproj_body,
        grid=(b, s // tm),
        in_specs=[row_spec(A_WIDTH), row_spec(R_WIDTH), row_spec(F_WIDTH),
                  c2((D_MODEL, D_MODEL)), row_spec(D_MODEL), c2((1, D_MODEL))],
        out_specs=row_spec(D_MODEL),
        out_shape=jax.ShapeDtypeStruct((b, s, D_MODEL), F32),
        compiler_params=pltpu.CompilerParams(
            dimension_semantics=("arbitrary", "arbitrary"), vmem_limit_bytes=VMEM_LIMIT),
        name="outproj",
    )(a, r, f, w, x, post_g)


@functools.lru_cache(maxsize=None)
def _tables(seq):
    half = HEAD_DIM // 2
    lane = np.arange(LANES)
    inv = ROPE_THETA ** (-(np.arange(half, dtype=np.float64) / half))
    ang = np.arange(seq, dtype=np.float64)[:, None] * inv[None, :]
    cos = np.cos(ang)
    sin = np.sin(ang)
    cos_t = cos[:, lane % half]
    sin_t = sin[:, lane % half] * np.where(lane < HEAD_DIM, -1.0, 1.0)[None, :]

    gam = 1.0 - np.exp2(-5.0 - np.arange(R_HEADS, dtype=np.float64))
    log_gam = np.log(gam)
    pos = np.arange(L_RET)
    dist = np.abs(pos[:, None] - pos[None, :])
    allowed = (pos[None, :] // CHUNK) <= (pos[:, None] // CHUNK)
    dmat = np.where(allowed[None], np.exp(log_gam[:, None, None] * dist[None]), 0.0)
    dmat = np.concatenate([dmat[0::2], dmat[1::2]], axis=2)
    head_of_lane = lane // HEAD_DIM
    head_of_qk_lane = (lane // half) % 2
    qdec = np.stack([np.exp(log_gam[2 * p + head_of_lane][None, :] * (pos + 1.0)[:, None])
                     for p in range(HEAD_PAIRS)])
    kdec = np.stack([np.exp(log_gam[2 * p + head_of_qk_lane][None, :] * (L_RET - 1.0 - pos)[:, None])
                     for p in range(HEAD_PAIRS)])
    bmask = (head_of_qk_lane[:, None] == head_of_lane[None, :]).astype(np.float32)
    sdec = np.stack([np.exp(log_gam[2 * p + head_of_qk_lane] * L_RET)[:, None] * bmask
                     for p in range(HEAD_PAIRS)])
    tri = (np.arange(TRI_ROWS)[None, :] <= np.arange(TRI_ROWS)[:, None]).astype(np.float32)
    grp = np.arange(A_WIDTH) // HEAD_DIM
    mavg = (grp[:, None] == grp[None, :]).astype(np.float32) / HEAD_DIM
    place = np.zeros((LANES, F_HEADS * LANES), np.float32)
    for h in range(F_HEADS):
        for term in range(3):
            place[8 * term + h, h * LANES + HEAD_DIM + term] = 1.0
    f = lambda a: np.asarray(a, np.float32)
    return dict(cos=f(cos_t), sin=f(sin_t), dmat=f(dmat), qdec=f(qdec), kdec=f(kdec),
                sdec=f(sdec), bmask=f(bmask), tri=f(tri), mavg=f(mavg),
                place=f(place))


def _layer(x, pre_g, post_g, w_in, b_f, a_ln_g, a_ws, a_bs, w_out, t):
    half = HEAD_DIM // 2
    w_t = w_in.T.astype(BF16)
    qk = w_t[A_COLS:A_COLS + 2 * R_WIDTH].reshape(2 * HEAD_PAIRS, 2, 2, half, D_MODEL)
    qk = jnp.swapaxes(qk, 1, 2).reshape(2 * R_WIDTH, D_MODEL)
    w_main = jnp.concatenate([w_t[:A_COLS], qk, w_t[A_COLS + 2 * R_WIDTH:D_IN_MAIN]], axis=0)
    w_flg = jnp.pad(w_t[D_IN_MAIN:], ((0, LANES - F_HEADS), (0, 0)))
    bf_pad = jnp.pad(b_f, (0, LANES - F_HEADS)).reshape(1, LANES)
    abias = jnp.repeat(a_bs.T, HEAD_DIM, axis=1)
    (a_out, rq, rk, rv, rg, fqt, kaug, vaug, fg, cf_t) = _inproj(
        x, pre_g.reshape(1, D_MODEL), w_main, w_flg, bf_pad, a_ln_g.reshape(1, A_WIDTH), a_ws, abias,
        t["cos"], t["sin"], t["tri"], t["mavg"], t["place"])
    r_out = _retention(rq, rk, rv, rg, t["dmat"], t["qdec"], t["kdec"], t["sdec"], t["bmask"],
                       t["mavg"])
    f_out = _fox(fqt, kaug, vaug, fg, cf_t)
    return _outproj(a_out, r_out, f_out, w_out.astype(BF16), x, post_g.reshape(1, D_MODEL))


def kernel(x, pre_gain, post_gain, w_in, b_forget, a_norm_gain, a_spatial_w, a_spatial_b, w_out):
    bf16_tables = ("tri", "mavg", "place")
    t = {k: jnp.asarray(v, BF16 if k in bf16_tables else F32) for k, v in _tables(x.shape[1]).items()}
    for l in range(pre_gain.shape[0]):
        x = _layer(x, pre_gain[l], post_gain[l], w_in[l], b_forget[l], a_norm_gain[l],
                   a_spatial_w[l], a_spatial_b[l], w_out[l], t)
    return x
```

```python
import functools
import math

import jax
import jax.numpy as jnp
import numpy as np
from jax import lax
from jax.experimental import pallas as pl
from jax.experimental.pallas import tpu as pltpu

F32 = jnp.float32
BF16 = jnp.bfloat16

D_MODEL = 1024
HEAD_DIM = 64
CHUNK = 64
A_WIDTH = 256
R_WIDTH = 384
F_WIDTH = 384
A_GROUPS = 4
R_HEADS = 6
F_HEADS = 6
A_BLOCK = 128
ROPE_THETA = 10000.0
EPS = 1e-6
LANES = 128
HEAD_PAIRS = R_WIDTH // LANES

A_COLS = 3 * A_WIDTH
R_COLS = 4 * R_WIDTH
F_COLS = 4 * F_WIDTH + F_HEADS
D_IN = A_COLS + R_COLS + F_COLS
D_IN_MAIN = A_COLS + R_COLS + 4 * F_WIDTH

TM_IN = 1024
RC_IN = 512
TRI_ROWS = 256
TM_RET = 2048
L_RET = 128
TQ = 512
TK = 256
TM_OUT = 2048
RC_OUT = 512
VMEM_LIMIT = 56 * 1024 * 1024
VAUG_ROWS = HEAD_DIM + 16
LOG2E = 1.4426950408889634
MASK_VALUE = -1e30


def _silu(x):
    return 0.5 * x * (1.0 + jnp.tanh(0.5 * x))


def _gelu_tanh(x):
    c = math.sqrt(2.0 / math.pi)
    return 0.5 * x * (1.0 + jnp.tanh(c * (x + 0.044715 * (x * x * x))))


def _log_sigmoid(x):
    return jnp.minimum(x, 0.0) - jnp.log1p(jnp.exp(-jnp.abs(x)))


def _dot(a, b):
    return jnp.dot(a, b, preferred_element_type=F32)


def _group_norm64(x, mavg):
    mean = _dot(x.astype(BF16), mavg)
    d = x - mean
    var = _dot((d * d).astype(BF16), mavg)
    return d * lax.rsqrt(var + EPS)


def _inproj_body(x_ref, pg_ref, w_ref, wflg_ref, bf_ref, aln_ref, aws_ref, abias_ref, cos_ref, sin_ref,
                 tri_ref, mavg_ref, place_ref,
                 a_ref, rq_ref, rk_ref, rv_ref, rg_ref, fq_ref, kaug_ref, vaug_ref, fg_ref, cft_ref,
                 carry_ref):
    tm = x_ref.shape[0]

    @pl.when(pl.program_id(1) == 0)
    def _():
        carry_ref[...] = jnp.zeros_like(carry_ref)

    mavg = mavg_ref[...]
    tri = tri_ref[...]
    row = lax.broadcasted_iota(jnp.int32, (A_BLOCK, A_BLOCK), 0)
    col = lax.broadcasted_iota(jnp.int32, (A_BLOCK, A_BLOCK), 1)
    allowed = jnp.logical_not(jnp.logical_and(row < CHUNK, col >= CHUNK))
    wcat = jnp.concatenate(
        [jnp.where(allowed, aws_ref[gi], 0.0).astype(BF16) for gi in range(A_GROUPS)], axis=1)
    lane_grp = lax.shift_right_logical(
        lax.broadcasted_iota(jnp.int32, (A_BLOCK, A_WIDTH), 1), HEAD_DIM.bit_length() - 1)
    zero_bf = jnp.zeros((A_BLOCK, A_WIDTH), BF16)
    lane = lax.broadcasted_iota(jnp.int32, (1, LANES), 1)
    srow = lax.broadcasted_iota(jnp.int32, (VAUG_ROWS - HEAD_DIM, 1), 0)
    ones_rows = jnp.broadcast_to(jnp.where(srow == 0, 1.0, 0.0),
                                 (VAUG_ROWS - HEAD_DIM, RC_IN)).astype(BF16)

    def rope(z, cos_t, sin_t):
        outs = []
        for c in range(HEAD_PAIRS):
            zc = z[:, c * LANES:(c + 1) * LANES]
            outs.append(zc * cos_t + pltpu.roll(zc, HEAD_DIM, 1) * sin_t)
        return jnp.concatenate(outs, axis=1)

    def normed(c):
        x = x_ref[c * RC_IN:(c + 1) * RC_IN, :]
        ms = jnp.mean(x * x, axis=-1, keepdims=True)
        return (x * lax.rsqrt(ms + EPS) * pg_ref[...]).astype(BF16)

    wide_cols = ((0, A_COLS), (A_COLS, A_COLS + R_COLS), (A_COLS + R_COLS, D_IN_MAIN))
    contract_last = (((1,), (1,)), ((), ()))

    def wide(h, k):
        z = lax.dot_general(h, w_ref[wide_cols[k][0]:wide_cols[k][1], :], contract_last,
                            preferred_element_type=F32)
        if k == 2:
            z_flg = lax.dot_general(h, wflg_ref[...], contract_last, preferred_element_type=F32)
            z = jnp.concatenate([z, z_flg], axis=1)
        return z

    def rows_of(c):
        return slice(c * RC_IN, (c + 1) * RC_IN)

    def a1(c, za, e):
        e["u"] = _gelu_tanh(za[:, 0:A_WIDTH])
        e["v"] = _gelu_tanh(za[:, A_WIDTH:2 * A_WIDTH])
        e["g"] = _silu(za[:, 2 * A_WIDTH:3 * A_WIDTH])
        e["mean"] = _dot(e["v"].astype(BF16), mavg)

    def a2(c, za, e):
        e["d"] = e["v"] - e["mean"]
        e["var"] = _dot((e["d"] * e["d"]).astype(BF16), mavg)

    def a3(c, za, e):
        vn = (e["d"] * lax.rsqrt(e["var"] + EPS) * aln_ref[...]).astype(BF16)
        mixed_blocks = []
        for nb in range(RC_IN // A_BLOCK):
            vb = vn[nb * A_BLOCK:(nb + 1) * A_BLOCK, :]
            vstack = jnp.concatenate(
                [jnp.where(lane_grp == gi, vb, zero_bf) for gi in range(A_GROUPS)], axis=0)
            mixed_blocks.append(_dot(wcat, vstack) + abias_ref[...])
        e["mixed"] = jnp.concatenate(mixed_blocks, axis=0)

    def a4(c, za, e):
        a_ref[rows_of(c), :] = (e["u"] * e["mixed"] * e["g"]).astype(BF16)

    def r1(c, zr, e):
        rows = rows_of(c)
        cos_t = cos_ref[rows, :]
        sin_t = sin_ref[rows, :]
        rq_ref[rows, :] = (rope(zr[:, 0:R_WIDTH], cos_t, sin_t) * (HEAD_DIM ** -0.5)).astype(BF16)
        rk_ref[rows, :] = rope(zr[:, R_WIDTH:2 * R_WIDTH], cos_t, sin_t).astype(BF16)
        rv_ref[rows, :] = zr[:, 2 * R_WIDTH:3 * R_WIDTH].astype(BF16)
        rg_ref[rows, :] = _silu(zr[:, 3 * R_WIDTH:4 * R_WIDTH]).astype(BF16)

    def f1(c, zf, e):
        ls = _log_sigmoid(zf[:, 4 * F_WIDTH:4 * F_WIDTH + LANES] + bf_ref[...])
        ls_hi = ls.astype(BF16)
        ls_lo = (ls - ls_hi.astype(F32)).astype(BF16)
        carry = carry_ref[...]
        blocks = []
        for blk in range(RC_IN // TRI_ROWS):
            rs = slice(blk * TRI_ROWS, (blk + 1) * TRI_ROWS)
            cf_blk = _dot(tri, ls_hi[rs]) + _dot(tri, ls_lo[rs]) + carry
            carry = cf_blk[TRI_ROWS - 1:TRI_ROWS, :]
            blocks.append(cf_blk)
        carry_ref[...] = carry
        e["cf2"] = jnp.concatenate(blocks, axis=0) * LOG2E

    def f2(c, zf, e):
        cf2 = e["cf2"]
        cft_ref[:, rows_of(c)] = cf2.T[:8, :]
        negf = -cf2
        hi = negf.astype(BF16).astype(F32)
        mid = (negf - hi).astype(BF16).astype(F32)
        lo = (negf - hi - mid).astype(BF16).astype(F32)
        packed = jnp.where(lane < 8, hi,
                           jnp.where(lane < 16, pltpu.roll(mid, 8, 1),
                                     jnp.where(lane < 24, pltpu.roll(lo, 16, 1), 0.0)))
        e["placed"] = _dot(packed.astype(BF16), place_ref[...])

    def f3(c, zf, e):
        rows = rows_of(c)
        fq_ref[:, rows] = (zf[:, 0:F_WIDTH] * (HEAD_DIM ** -0.5 * LOG2E)).T.astype(BF16)
        fg_ref[rows, :] = _silu(zf[:, 3 * F_WIDTH:4 * F_WIDTH]).astype(BF16)
        fv_t = zf[:, 2 * F_WIDTH:3 * F_WIDTH].T.astype(BF16)
        for hd in range(F_HEADS):
            p, hh = divmod(hd, 2)
            kk = zf[:, F_WIDTH + p * LANES:F_WIDTH + (p + 1) * LANES]
            if hh == 1:
                kk = pltpu.roll(kk, HEAD_DIM, 1)
            aug = e["placed"][:, hd * LANES:(hd + 1) * LANES]
            kaug_ref[hd, rows, :] = jnp.where(lane < HEAD_DIM, kk, aug).astype(BF16)
            vaug_ref[hd, 0:HEAD_DIM, rows] = fv_t[hd * HEAD_DIM:(hd + 1) * HEAD_DIM, :]
            vaug_ref[hd, HEAD_DIM:VAUG_ROWS, rows] = ones_rows

    nchunks = tm // RC_IN
    hs = [normed(c) for c in range(nchunks)]
    groups = ((0, (a1, a2, a3, a4)), (2, (f1, f2, f3)), (1, (r1,)))
    issued = []

    def run_slot(t):
        for s, (c, z, stages, e) in enumerate(issued):
            j = t - 1 - s
            if 0 <= j < len(stages):
                stages[j](c, z, e)

    for k, stages in groups:
        for c in range(nchunks):
            issued.append((c, wide(hs[c], k), stages, {}))
            run_slot(len(issued) - 1)
    depth = max(len(stages) for _, stages in groups)
    for t in range(len(issued), len(issued) + depth):
        run_slot(t)


def _inproj(x, pre_g, w_main, w_flg, bf_pad, aln, aws, abias, cos_t, sin_t, tri, mavg, place):
    b, s, _ = x.shape
    tm = TM_IN
    grid = (b, s // tm)
    row_spec = lambda n: pl.BlockSpec((None, tm, n), lambda bi, si: (bi, si, 0))
    col_spec = lambda n: pl.BlockSpec((None, n, tm), lambda bi, si: (bi, 0, si))
    const2 = lambda shp: pl.BlockSpec(shp, lambda bi, si: (0, 0))
    bf_sds = lambda n: jax.ShapeDtypeStruct((b, s, n), BF16)
    out_shape = ([bf_sds(A_WIDTH)] + [bf_sds(R_WIDTH)] * 4
                 + [jax.ShapeDtypeStruct((b, F_WIDTH, s), BF16),
                    jax.ShapeDtypeStruct((b, F_HEADS, s, LANES), BF16),
                    jax.ShapeDtypeStruct((b, F_HEADS, VAUG_ROWS, s), BF16),
                    bf_sds(F_WIDTH),
                    jax.ShapeDtypeStruct((b, 8, s), F32)])
    out_specs = ([row_spec(A_WIDTH)] + [row_spec(R_WIDTH)] * 4
                 + [col_spec(F_WIDTH),
                    pl.BlockSpec((None, F_HEADS, tm, LANES), lambda bi, si: (bi, 0, si, 0)),
                    pl.BlockSpec((None, F_HEADS, VAUG_ROWS, tm), lambda bi, si: (bi, 0, 0, si)),
                    row_spec(F_WIDTH),
                    col_spec(8)])
    in_specs = [
        row_spec(D_MODEL),
        const2((1, D_MODEL)),
        pl.BlockSpec((D_IN_MAIN, D_MODEL), lambda bi, si: (0, 0), pipeline_mode=pl.Buffered(1)),
        const2((LANES, D_MODEL)),
        const2((1, LANES)),
        const2((1, A_WIDTH)),
        pl.BlockSpec((A_GROUPS, A_BLOCK, A_BLOCK), lambda bi, si: (0, 0, 0)),
        const2((A_BLOCK, A_WIDTH)),
        pl.BlockSpec((tm, LANES), lambda bi, si: (si, 0)),
        pl.BlockSpec((tm, LANES), lambda bi, si: (si, 0)),
        const2((TRI_ROWS, TRI_ROWS)),
        const2((A_WIDTH, A_WIDTH)),
        const2((LANES, F_HEADS * LANES)),
    ]
    return pl.pallas_call(
        _inproj_body,
        grid=grid,
        in_specs=in_specs,
        out_specs=out_specs,
        out_shape=out_shape,
        scratch_shapes=[pltpu.VMEM((1, LANES), F32)],
        compiler_params=pltpu.CompilerParams(
            dimension_semantics=("arbitrary", "arbitrary"), vmem_limit_bytes=VMEM_LIMIT),
        name="inproj",
    )(x, pre_g, w_main, w_flg, bf_pad, aln, aws, abias, cos_t, sin_t, tri, mavg, place)


def _retention_body(q_ref, k_ref, v_ref, g_ref, dmat_ref, qdec_ref, kdec_ref, sdec_ref, bmask_ref,
                    mavg_ref, o_ref, state_ref):
    tm = q_ref.shape[0]

    @pl.when(pl.program_id(1) == 0)
    def _():
        state_ref[...] = jnp.zeros_like(state_ref)

    lane = lax.broadcasted_iota(jnp.int32, (1, LANES), 1)
    head0_v = lane < HEAD_DIM
    head0_qk = jnp.bitwise_and(lane, HEAD_DIM // 2) == 0
    mavg = mavg_ref[...]
    bmask = bmask_ref[...]
    contract_last = (((1,), (1,)), ((), ()))
    contract_first = (((0,), (0,)), ((), ()))
    nblk = tm // L_RET
    units = [(p, j) for p in range(HEAD_PAIRS) for j in range(nblk)]

    def split_heads(x2, head0):
        zero = jnp.zeros_like(x2)
        return jnp.concatenate([jnp.where(head0, x2, zero), jnp.where(head0, zero, x2)], axis=0)

    q2, v2, sc, kv = {}, {}, {}, {}
    for (p, j) in units:
        rs, cs = slice(j * L_RET, (j + 1) * L_RET), slice(p * LANES, (p + 1) * LANES)
        q2[p, j] = q_ref[rs, cs]
        k2 = k_ref[rs, cs]
        v2[p, j] = v_ref[rs, cs]
        sc[p, j] = lax.dot_general(q2[p, j], split_heads(k2, head0_qk), contract_last,
                                   preferred_element_type=F32)
        kd = (k2.astype(F32) * kdec_ref[p]).astype(BF16)
        kv[p, j] = lax.dot_general(kd, v2[p, j], contract_first, preferred_element_type=F32)

    st = {}
    for p in range(HEAD_PAIRS):
        state = state_ref[p]
        for j in range(nblk):
            st[p, j] = state.astype(BF16)
            state = state * sdec_ref[p] + kv[p, j] * bmask
        state_ref[p] = state

    out = {}
    for (p, j) in units:
        pm = (sc[p, j] * dmat_ref[p]).astype(BF16)
        out[p, j] = (_dot(pm, split_heads(v2[p, j], head0_v))
                     + _dot(q2[p, j], st[p, j]) * qdec_ref[p])

    tiles = [jnp.concatenate([out[p, j] for j in range(nblk)], axis=0) for p in range(HEAD_PAIRS)]
    groups = ((jnp.concatenate(tiles[0:2], axis=1), mavg, slice(0, 2 * LANES)),
              (tiles[2], mavg[:LANES, :LANES], slice(2 * LANES, 3 * LANES)))
    means = [_dot(t.astype(BF16), m) for t, m, _ in groups]
    devs = [t - mu for (t, _, _), mu in zip(groups, means)]
    variances = [_dot((d * d).astype(BF16), m) for d, (_, m, _) in zip(devs, groups)]
    for d, var, (_, _, cs) in zip(devs, variances, groups):
        y = d * lax.rsqrt(var + EPS) * g_ref[:, cs].astype(F32)
        o_ref[:, cs] = y.astype(BF16)


def _retention(rq, rk, rv, rg, dmat, qdec, kdec, sdec, bmask, mavg):
    b, s, _ = rq.shape
    tm = TM_RET
    row_spec = pl.BlockSpec((None, tm, R_WIDTH), lambda bi, si: (bi, si, 0))
    c3 = lambda shp: pl.BlockSpec(shp, lambda bi, si: (0, 0, 0))
    c2 = lambda shp: pl.BlockSpec(shp, lambda bi, si: (0, 0))
    return pl.pallas_call(
        _retention_body,
        grid=(b, s // tm),
        in_specs=[row_spec, row_spec, row_spec, row_spec,
                  c3((HEAD_PAIRS, L_RET, 2 * L_RET)), c3((HEAD_PAIRS, L_RET, LANES)),
                  c3((HEAD_PAIRS, L_RET, LANES)), c3((HEAD_PAIRS, LANES, LANES)),
                  c2((LANES, LANES)), c2((2 * LANES, 2 * LANES))],
        out_specs=row_spec,
        out_shape=jax.ShapeDtypeStruct((b, s, R_WIDTH), BF16),
        scratch_shapes=[pltpu.VMEM((HEAD_PAIRS, LANES, LANES), F32)],
        compiler_params=pltpu.CompilerParams(
            dimension_semantics=("arbitrary", "arbitrary"), vmem_limit_bytes=VMEM_LIMIT),
        name="retention",
    )(rq, rk, rv, rg, dmat, qdec, kdec, sdec, bmask, mavg)


def _fox_body(qt_ref, qtn_ref, kaug_ref, vaug_ref, cft_ref, g_ref, o_ref, m_ref, acc_ref, sa_ref,
              sb_ref):
    qi = pl.program_id(1)
    srow_q = lax.broadcasted_iota(jnp.int32, (HEAD_DIM, 1), 0)
    q_ones = jnp.broadcast_to(jnp.where(srow_q < 3, 1.0, 0.0), (HEAD_DIM, TQ)).astype(BF16)
    qta = [jnp.concatenate([qt_ref[h * HEAD_DIM:(h + 1) * HEAD_DIM, :], q_ones], axis=0)
           for h in range(F_HEADS)]
    fq = [cft_ref[h:h + 1, :] for h in range(F_HEADS)]
    krow = lax.broadcasted_iota(jnp.int32, (TK, TQ), 0)
    qcol = lax.broadcasted_iota(jnp.int32, (TK, TQ), 1)
    causal = (krow <= qcol, krow + TK <= qcol)

    for h in range(F_HEADS):
        m_ref[h] = jnp.full((8, TQ), MASK_VALUE, F32)
        acc_ref[h] = jnp.zeros((VAUG_ROWS, TQ), F32)

    def scores(kj, dst_ref, h, last=False):
        k0 = pl.multiple_of(kj * TK, TK)
        kt = kaug_ref[h, pl.ds(k0, TK), :]
        if last:
            dst_ref[h, :, TK:TQ] = _dot(kt, qta[h][:, TK:TQ])
        else:
            dst_ref[h] = _dot(kt, qta[h])

    def consume(kj, src_ref, h, diagonal):
        k0 = pl.multiple_of(kj * TK, TK)
        st = src_ref[h]
        if diagonal:
            st = jnp.where(causal[0], st, MASK_VALUE)
        m_old = m_ref[h][0:1, :]
        m_new = jnp.maximum(m_old, jnp.max(st, axis=0, keepdims=True) + fq[h])
        p = jnp.exp2(st - (m_new - fq[h]))
        alpha = jnp.exp2(m_old - m_new)
        pv = _dot(vaug_ref[h, :, pl.ds(k0, TK)], p.astype(BF16))
        acc_ref[h] = alpha * acc_ref[h] + pv
        m_ref[h] = jnp.broadcast_to(m_new, (8, TQ))

    def consume_last(kj, src_ref, h):
        k0 = pl.multiple_of(kj * TK, TK)
        st = jnp.where(causal[1][:, TK:TQ], src_ref[h, :, TK:TQ], MASK_VALUE)
        m_old = m_ref[h][0:1, TK:TQ]
        fq_h = fq[h][:, TK:TQ]
        m_new = jnp.maximum(m_old, jnp.max(st, axis=0, keepdims=True) + fq_h)
        p = jnp.exp2(st - (m_new - fq_h))
        alpha = jnp.exp2(m_old - m_new)
        pv = _dot(vaug_ref[h, :, pl.ds(k0, TK)], p.astype(BF16))
        acc_ref[h, :, TK:TQ] = alpha * acc_ref[h, :, TK:TQ] + pv

    def overlapped(next_j, dst_ref, cur_j, src_ref, diagonal=False, next_last=False):
        lead = 2
        for h in range(lead):
            scores(next_j, dst_ref, h, next_last)
        for h in range(F_HEADS):
            consume(cur_j, src_ref, h, diagonal)
            if h + lead < F_HEADS:
                scores(next_j, dst_ref, h + lead, next_last)

    @pl.when(qi == 0)
    def _():
        for h in range(F_HEADS):
            scores(0, sa_ref, h)

    def body(i, carry):
        j0 = 2 * i
        overlapped(j0 + 1, sb_ref, j0, sa_ref)
        overlapped(j0 + 2, sa_ref, j0 + 1, sb_ref)
        return carry

    lax.fori_loop(0, qi, body, 0)
    overlapped(2 * qi + 1, sb_ref, 2 * qi, sa_ref, diagonal=True, next_last=True)
    for h in range(F_HEADS):
        consume_last(2 * qi + 1, sb_ref, h)
        qta_next = jnp.concatenate([qtn_ref[h * HEAD_DIM:(h + 1) * HEAD_DIM, :], q_ones], axis=0)
        sa_ref[h] = _dot(kaug_ref[h, 0:TK, :], qta_next)

    for p in range(HEAD_PAIRS):
        halves = []
        for hh in range(2):
            acc = acc_ref[2 * p + hh]
            halves.append(acc[0:HEAD_DIM, :] / acc[HEAD_DIM:HEAD_DIM + 1, :])
        out_t = jnp.concatenate(halves, axis=0)
        cs = slice(p * LANES, (p + 1) * LANES)
        o_ref[:, cs] = (out_t.T * g_ref[:, cs].astype(F32)).astype(BF16)


def _fox(fqt, kaug, vaug, fg, cf_t):
    b, _, s = fqt.shape
    assert TQ == 2 * TK
    row_spec = pl.BlockSpec((None, TQ, F_WIDTH), lambda bi, qi: (bi, qi, 0))
    return pl.pallas_call(
        _fox_body,
        grid=(b, s // TQ),
        in_specs=[pl.BlockSpec((None, F_WIDTH, TQ), lambda bi, qi: (bi, 0, qi)),
                  pl.BlockSpec((None, F_WIDTH, TQ),
                               lambda bi, qi: (bi, 0, jnp.minimum(qi + 1, s // TQ - 1))),
                  pl.BlockSpec((None, F_HEADS, s, LANES), lambda bi, qi: (bi, 0, 0, 0)),
                  pl.BlockSpec((None, F_HEADS, VAUG_ROWS, s), lambda bi, qi: (bi, 0, 0, 0)),
                  pl.BlockSpec((None, 8, TQ), lambda bi, qi: (bi, 0, qi)),
                  row_spec],
        out_specs=row_spec,
        out_shape=jax.ShapeDtypeStruct((b, s, F_WIDTH), BF16),
        scratch_shapes=[pltpu.VMEM((F_HEADS, 8, TQ), F32),
                        pltpu.VMEM((F_HEADS, VAUG_ROWS, TQ), F32),
                        pltpu.VMEM((F_HEADS, TK, TQ), F32),
                        pltpu.VMEM((F_HEADS, TK, TQ), F32)],
        compiler_params=pltpu.CompilerParams(
            dimension_semantics=("arbitrary", "arbitrary"), vmem_limit_bytes=VMEM_LIMIT),
        name="fox",
    )(fqt, fqt, kaug, vaug, cf_t, fg)


def _outproj_body(a_ref, r_ref, f_ref, w_ref, x_ref, pg_ref, o_ref):
    tm = x_ref.shape[0]

    def project(c):
        rows = slice(c * RC_OUT, (c + 1) * RC_OUT)
        y = jnp.concatenate([a_ref[rows, :], r_ref[rows, :], f_ref[rows, :]], axis=1)
        return _dot(y, w_ref[...])

    nchunks = tm // RC_OUT
    o_next = project(0)
    for c in range(nchunks):
        rows = slice(c * RC_OUT, (c + 1) * RC_OUT)
        o = o_next
        if c + 1 < nchunks:
            o_next = project(c + 1)
        ms = jnp.mean(o * o, axis=-1, keepdims=True)
        o_ref[rows, :] = x_ref[rows, :] + o * lax.rsqrt(ms + EPS) * pg_ref[...]


def _outproj(a, r, f, w, x, post_g):
    b, s, _ = x.shape
    tm = TM_OUT
    row_spec = lambda n: pl.BlockSpec((None, tm, n), lambda bi, si: (bi, si, 0))
    c2 = lambda shp: pl.BlockSpec(shp, lambda bi, si: (0, 0))
    return pl.pallas_call(
        _outproj_body,
        grid=(b, s // tm),
        in_specs=[row_spec(A_WIDTH), row_spec(R_WIDTH), row_spec(F_WIDTH),
                  c2((D_MODEL, D_MODEL)), row_spec(D_MODEL), c2((1, D_MODEL))],
        out_specs=row_spec(D_MODEL),
        out_shape=jax.ShapeDtypeStruct((b, s, D_MODEL), F32),
        compiler_params=pltpu.CompilerParams(
            dimension_semantics=("arbitrary", "arbitrary"), vmem_limit_bytes=VMEM_LIMIT),
        name="outproj",
    )(a, r, f, w, x, post_g)


@functools.lru_cache(maxsize=None)
def _tables(seq):
    half = HEAD_DIM // 2
    lane = np.arange(LANES)
    inv = ROPE_THETA ** (-(np.arange(half, dtype=np.float64) / half))
    ang = np.arange(seq, dtype=np.float64)[:, None] * inv[None, :]
    cos = np.cos(ang)
    sin = np.sin(ang)
    cos_t = cos[:, lane % half]
    sin_t = sin[:, lane % half] * np.where(lane < HEAD_DIM, -1.0, 1.0)[None, :]

    gam = 1.0 - np.exp2(-5.0 - np.arange(R_HEADS, dtype=np.float64))
    log_gam = np.log(gam)
    pos = np.arange(L_RET)
    dist = np.abs(pos[:, None] - pos[None, :])
    allowed = (pos[None, :] // CHUNK) <= (pos[:, None] // CHUNK)
    dmat = np.where(allowed[None], np.exp(log_gam[:, None, None] * dist[None]), 0.0)
    dmat = np.concatenate([dmat[0::2], dmat[1::2]], axis=2)
    head_of_lane = lane // HEAD_DIM
    head_of_qk_lane = (lane // half) % 2
    qdec = np.stack([np.exp(log_gam[2 * p + head_of_lane][None, :] * (pos + 1.0)[:, None])
                     for p in range(HEAD_PAIRS)])
    kdec = np.stack([np.exp(log_gam[2 * p + head_of_qk_lane][None, :] * (L_RET - 1.0 - pos)[:, None])
                     for p in range(HEAD_PAIRS)])
    bmask = (head_of_qk_lane[:, None] == head_of_lane[None, :]).astype(np.float32)
    sdec = np.stack([np.exp(log_gam[2 * p + head_of_qk_lane] * L_RET)[:, None] * bmask
                     for p in range(HEAD_PAIRS)])
    tri = (np.arange(TRI_ROWS)[None, :] <= np.arange(TRI_ROWS)[:, None]).astype(np.float32)
    grp = np.arange(A_WIDTH) // HEAD_DIM
    mavg = (grp[:, None] == grp[None, :]).astype(np.float32) / HEAD_DIM
    place = np.zeros((LANES, F_HEADS * LANES), np.float32)
    for h in range(F_HEADS):
        for term in range(3):
            place[8 * term + h, h * LANES + HEAD_DIM + term] = 1.0
    f = lambda a: np.asarray(a, np.float32)
    return dict(cos=f(cos_t), sin=f(sin_t), dmat=f(dmat), qdec=f(qdec), kdec=f(kdec),
                sdec=f(sdec), bmask=f(bmask), tri=f(tri), mavg=f(mavg),
                place=f(place))


def _layer(x, pre_g, post_g, w_in, b_f, a_ln_g, a_ws, a_bs, w_out, t):
    half = HEAD_DIM // 2
    w_t = w_in.T.astype(BF16)
    qk = w_t[A_COLS:A_COLS + 2 * R_WIDTH].reshape(2 * HEAD_PAIRS, 2, 2, half, D_MODEL)
    qk = jnp.swapaxes(qk, 1, 2).reshape(2 * R_WIDTH, D_MODEL)
    w_main = jnp.concatenate([w_t[:A_COLS], qk, w_t[A_COLS + 2 * R_WIDTH:D_IN_MAIN]], axis=0)
    w_flg = jnp.pad(w_t[D_IN_MAIN:], ((0, LANES - F_HEADS), (0, 0)))
    bf_pad = jnp.pad(b_f, (0, LANES - F_HEADS)).reshape(1, LANES)
    abias = jnp.repeat(a_bs.T, HEAD_DIM, axis=1)
    (a_out, rq, rk, rv, rg, fqt, kaug, vaug, fg, cf_t) = _inproj(
        x, pre_g.reshape(1, D_MODEL), w_main, w_flg, bf_pad, a_ln_g.reshape(1, A_WIDTH), a_ws, abias,
        t["cos"], t["sin"], t["tri"], t["mavg"], t["place"])
    r_out = _retention(rq, rk, rv, rg, t["dmat"], t["qdec"], t["kdec"], t["sdec"], t["bmask"],
                       t["mavg"])
    f_out = _fox(fqt, kaug, vaug, fg, cf_t)
    return _outproj(a_out, r_out, f_out, w_out.astype(BF16), x, post_g.reshape(1, D_MODEL))


def kernel(x, pre_gain, post_gain, w_in, b_forget, a_norm_gain, a_spatial_w, a_spatial_b, w_out):
    bf16_tables = ("tri", "mavg", "place")
    t = {k: jnp.asarray(v, BF16 if k in bf16_tables else F32) for k, v in _tables(x.shape[1]).items()}
    for l in range(pre_gain.shape[0]):
        x = _layer(x, pre_gain[l], post_gain[l], w_in[l], b_forget[l], a_norm_gain[l],
                   a_spatial_w[l], a_spatial_b[l], w_out[l], t)
    return x
```

```python
import functools
import math

import jax
import jax.numpy as jnp
import numpy as np
from jax import lax
from jax.experimental import pallas as pl
from jax.experimental.pallas import tpu as pltpu

F32 = jnp.float32
BF16 = jnp.bfloat16

D_MODEL = 1024
HEAD_DIM = 64
CHUNK = 64
A_WIDTH = 256
R_WIDTH = 384
F_WIDTH = 384
A_GROUPS = 4
R_HEADS = 6
F_HEADS = 6
A_BLOCK = 128
ROPE_THETA = 10000.0
EPS = 1e-6
LANES = 128
HEAD_PAIRS = R_WIDTH // LANES

A_COLS = 3 * A_WIDTH
R_COLS = 4 * R_WIDTH
F_COLS = 4 * F_WIDTH + F_HEADS
D_IN = A_COLS + R_COLS + F_COLS
D_IN_MAIN = A_COLS + R_COLS + 4 * F_WIDTH

TM_IN = 1024
RC_IN = 512
TRI_ROWS = 256
TM_RET = 2048
L_RET = 128
TQ = 512
TK = 256
TM_OUT = 2048
RC_OUT = 512
VMEM_LIMIT = 56 * 1024 * 1024
VAUG_ROWS = HEAD_DIM + 16
LOG2E = 1.4426950408889634
MASK_VALUE = -1e30


def _silu(x):
    return 0.5 * x * (1.0 + jnp.tanh(0.5 * x))


def _gelu_tanh(x):
    c = math.sqrt(2.0 / math.pi)
    return 0.5 * x * (1.0 + jnp.tanh(c * (x + 0.044715 * (x * x * x))))


def _log_sigmoid(x):
    return jnp.minimum(x, 0.0) - jnp.log1p(jnp.exp(-jnp.abs(x)))


def _dot(a, b):
    return jnp.dot(a, b, preferred_element_type=F32)


def _group_norm64(x, mavg):
    mean = _dot(x.astype(BF16), mavg)
    d = x - mean
    var = _dot((d * d).astype(BF16), mavg)
    return d * lax.rsqrt(var + EPS)


def _inproj_body(x_ref, pg_ref, w_ref, wqk_ref, wflg_ref, bf_ref, aln_ref, aws_ref, abias_ref, cos_ref, sin_ref,
                 tri_ref, mavg_ref, place_ref,
                 a_ref, rq_ref, rk_ref, rv_ref, rg_ref, fq_ref, kaug_ref, vaug_ref, fg_ref, cft_ref,
                 carry_ref):
    tm = x_ref.shape[0]

    @pl.when(pl.program_id(1) == 0)
    def _():
        carry_ref[...] = jnp.zeros_like(carry_ref)

    mavg = mavg_ref[...]
    tri = tri_ref[...]
    row = lax.broadcasted_iota(jnp.int32, (A_BLOCK, A_BLOCK), 0)
    col = lax.broadcasted_iota(jnp.int32, (A_BLOCK, A_BLOCK), 1)
    allowed = jnp.logical_not(jnp.logical_and(row < CHUNK, col >= CHUNK))
    wcat = jnp.concatenate(
        [jnp.where(allowed, aws_ref[gi], 0.0).astype(BF16) for gi in range(A_GROUPS)], axis=1)
    lane_grp = lax.shift_right_logical(
        lax.broadcasted_iota(jnp.int32, (A_BLOCK, A_WIDTH), 1), HEAD_DIM.bit_length() - 1)
    zero_bf = jnp.zeros((A_BLOCK, A_WIDTH), BF16)
    lane = lax.broadcasted_iota(jnp.int32, (1, LANES), 1)
    srow = lax.broadcasted_iota(jnp.int32, (VAUG_ROWS - HEAD_DIM, 1), 0)
    ones_rows = jnp.broadcast_to(jnp.where(srow == 0, 1.0, 0.0),
                                 (VAUG_ROWS - HEAD_DIM, RC_IN)).astype(BF16)

    def rope(z, cos_t, sin_t):
        outs = []
        for c in range(HEAD_PAIRS):
            zc = z[:, c * LANES:(c + 1) * LANES]
            outs.append(zc * cos_t + pltpu.roll(zc, HEAD_DIM, 1) * sin_t)
        return jnp.concatenate(outs, axis=1)

    def normed(c):
        x = x_ref[c * RC_IN:(c + 1) * RC_IN, :]
        ms = jnp.mean(x * x, axis=-1, keepdims=True)
        return (x * lax.rsqrt(ms + EPS) * pg_ref[...]).astype(BF16)

    contract_last = (((1,), (1,)), ((), ()))

    def nt_dot(h, w):
        return lax.dot_general(h, w, contract_last, preferred_element_type=F32)

    def wide(h, k):
        if k == 0:
            return nt_dot(h, w_ref[0:A_COLS, :])
        if k == 1:
            return jnp.concatenate(
                [nt_dot(h, wqk_ref[...]),
                 nt_dot(h, w_ref[A_COLS + 2 * R_WIDTH:A_COLS + R_COLS, :])], axis=1)
        return jnp.concatenate(
            [nt_dot(h, w_ref[A_COLS + R_COLS:D_IN_MAIN, :]), nt_dot(h, wflg_ref[...])], axis=1)

    def rows_of(c):
        return slice(c * RC_IN, (c + 1) * RC_IN)

    def a1(c, za, e):
        e["u"] = _gelu_tanh(za[:, 0:A_WIDTH])
        e["v"] = _gelu_tanh(za[:, A_WIDTH:2 * A_WIDTH])
        e["g"] = _silu(za[:, 2 * A_WIDTH:3 * A_WIDTH])
        e["mean"] = _dot(e["v"].astype(BF16), mavg)

    def a2(c, za, e):
        e["d"] = e["v"] - e["mean"]
        e["var"] = _dot((e["d"] * e["d"]).astype(BF16), mavg)

    def a3(c, za, e):
        vn = (e["d"] * lax.rsqrt(e["var"] + EPS) * aln_ref[...]).astype(BF16)
        mixed_blocks = []
        for nb in range(RC_IN // A_BLOCK):
            vb = vn[nb * A_BLOCK:(nb + 1) * A_BLOCK, :]
            vstack = jnp.concatenate(
                [jnp.where(lane_grp == gi, vb, zero_bf) for gi in range(A_GROUPS)], axis=0)
            mixed_blocks.append(_dot(wcat, vstack) + abias_ref[...])
        e["mixed"] = jnp.concatenate(mixed_blocks, axis=0)

    def a4(c, za, e):
        a_ref[rows_of(c), :] = (e["u"] * e["mixed"] * e["g"]).astype(BF16)

    def r1(c, zr, e):
        rows = rows_of(c)
        cos_t = cos_ref[rows, :]
        sin_t = sin_ref[rows, :]
        rq_ref[rows, :] = (rope(zr[:, 0:R_WIDTH], cos_t, sin_t) * (HEAD_DIM ** -0.5)).astype(BF16)
        rk_ref[rows, :] = rope(zr[:, R_WIDTH:2 * R_WIDTH], cos_t, sin_t).astype(BF16)
        rv_ref[rows, :] = zr[:, 2 * R_WIDTH:3 * R_WIDTH].astype(BF16)
        rg_ref[rows, :] = _silu(zr[:, 3 * R_WIDTH:4 * R_WIDTH]).astype(BF16)

    def f1(c, zf, e):
        ls = _log_sigmoid(zf[:, 4 * F_WIDTH:4 * F_WIDTH + LANES] + bf_ref[...])
        ls_hi = ls.astype(BF16)
        ls_lo = (ls - ls_hi.astype(F32)).astype(BF16)
        carry = carry_ref[...]
        blocks = []
        for blk in range(RC_IN // TRI_ROWS):
            rs = slice(blk * TRI_ROWS, (blk + 1) * TRI_ROWS)
            cf_blk = _dot(tri, ls_hi[rs]) + _dot(tri, ls_lo[rs]) + carry
            carry = cf_blk[TRI_ROWS - 1:TRI_ROWS, :]
            blocks.append(cf_blk)
        carry_ref[...] = carry
        e["cf2"] = jnp.concatenate(blocks, axis=0) * LOG2E

    def f2(c, zf, e):
        cf2 = e["cf2"]
        cft_ref[:, rows_of(c)] = cf2.T[:8, :]
        negf = -cf2
        hi = negf.astype(BF16).astype(F32)
        mid = (negf - hi).astype(BF16).astype(F32)
        lo = (negf - hi - mid).astype(BF16).astype(F32)
        packed = jnp.where(lane < 8, hi,
                           jnp.where(lane < 16, pltpu.roll(mid, 8, 1),
                                     jnp.where(lane < 24, pltpu.roll(lo, 16, 1), 0.0)))
        e["placed"] = _dot(packed.astype(BF16), place_ref[...])

    def f3(c, zf, e):
        rows = rows_of(c)
        fq_ref[:, rows] = (zf[:, 0:F_WIDTH] * (HEAD_DIM ** -0.5 * LOG2E)).T.astype(BF16)
        fg_ref[rows, :] = _silu(zf[:, 3 * F_WIDTH:4 * F_WIDTH]).astype(BF16)
        fv_t = zf[:, 2 * F_WIDTH:3 * F_WIDTH].T.astype(BF16)
        for hd in range(F_HEADS):
            p, hh = divmod(hd, 2)
            kk = zf[:, F_WIDTH + p * LANES:F_WIDTH + (p + 1) * LANES]
            if hh == 1:
                kk = pltpu.roll(kk, HEAD_DIM, 1)
            aug = e["placed"][:, hd * LANES:(hd + 1) * LANES]
            kaug_ref[hd, rows, :] = jnp.where(lane < HEAD_DIM, kk, aug).astype(BF16)
            vaug_ref[hd, 0:HEAD_DIM, rows] = fv_t[hd * HEAD_DIM:(hd + 1) * HEAD_DIM, :]
            vaug_ref[hd, HEAD_DIM:VAUG_ROWS, rows] = ones_rows

    nchunks = tm // RC_IN
    hs = [normed(c) for c in range(nchunks)]
    groups = ((0, (a1, a2, a3, a4)), (2, (f1, f2, f3)), (1, (r1,)))
    issued = []

    def run_slot(t):
        for s, (c, z, stages, e) in enumerate(issued):
            j = t - 1 - s
            if 0 <= j < len(stages):
                stages[j](c, z, e)

    for k, stages in groups:
        for c in range(nchunks):
            issued.append((c, wide(hs[c], k), stages, {}))
            run_slot(len(issued) - 1)
    depth = max(len(stages) for _, stages in groups)
    for t in range(len(issued), len(issued) + depth):
        run_slot(t)


def _inproj(x, pre_g, w_t, w_qk, w_flg, bf_pad, aln, aws, abias, cos_t, sin_t, tri, mavg, place):
    b, s, _ = x.shape
    tm = TM_IN
    grid = (b, s // tm)
    row_spec = lambda n: pl.BlockSpec((None, tm, n), lambda bi, si: (bi, si, 0))
    col_spec = lambda n: pl.BlockSpec((None, n, tm), lambda bi, si: (bi, 0, si))
    const2 = lambda shp: pl.BlockSpec(shp, lambda bi, si: (0, 0))
    bf_sds = lambda n: jax.ShapeDtypeStruct((b, s, n), BF16)
    out_shape = ([bf_sds(A_WIDTH)] + [bf_sds(R_WIDTH)] * 4
                 + [jax.ShapeDtypeStruct((b, F_WIDTH, s), BF16),
                    jax.ShapeDtypeStruct((b, F_HEADS, s, LANES), BF16),
                    jax.ShapeDtypeStruct((b, F_HEADS, VAUG_ROWS, s), BF16),
                    bf_sds(F_WIDTH),
                    jax.ShapeDtypeStruct((b, 8, s), F32)])
    out_specs = ([row_spec(A_WIDTH)] + [row_spec(R_WIDTH)] * 4
                 + [col_spec(F_WIDTH),
                    pl.BlockSpec((None, F_HEADS, tm, LANES), lambda bi, si: (bi, 0, si, 0)),
                    pl.BlockSpec((None, F_HEADS, VAUG_ROWS, tm), lambda bi, si: (bi, 0, 0, si)),
                    row_spec(F_WIDTH),
                    col_spec(8)])
    in_specs = [
        row_spec(D_MODEL),
        const2((1, D_MODEL)),
        pl.BlockSpec((D_IN, D_MODEL), lambda bi, si: (0, 0), pipeline_mode=pl.Buffered(1)),
        const2((2 * R_WIDTH, D_MODEL)),
        const2((LANES, D_MODEL)),
        const2((1, LANES)),
        const2((1, A_WIDTH)),
        pl.BlockSpec((A_GROUPS, A_BLOCK, A_BLOCK), lambda bi, si: (0, 0, 0)),
        const2((A_BLOCK, A_WIDTH)),
        pl.BlockSpec((tm, LANES), lambda bi, si: (si, 0)),
        pl.BlockSpec((tm, LANES), lambda bi, si: (si, 0)),
        const2((TRI_ROWS, TRI_ROWS)),
        const2((A_WIDTH, A_WIDTH)),
        const2((LANES, F_HEADS * LANES)),
    ]
    return pl.pallas_call(
        _inproj_body,
        grid=grid,
        in_specs=in_specs,
        out_specs=out_specs,
        out_shape=out_shape,
        scratch_shapes=[pltpu.VMEM((1, LANES), F32)],
        compiler_params=pltpu.CompilerParams(
            dimension_semantics=("arbitrary", "arbitrary"), vmem_limit_bytes=VMEM_LIMIT),
        name="inproj",
    )(x, pre_g, w_t, w_qk, w_flg, bf_pad, aln, aws, abias, cos_t, sin_t, tri, mavg, place)


def _retention_body(q_ref, k_ref, v_ref, g_ref, dmat_ref, qdec_ref, kdec_ref, sdec_ref, bmask_ref,
                    mavg_ref, o_ref, state_ref):
    tm = q_ref.shape[0]

    @pl.when(pl.program_id(1) == 0)
    def _():
        state_ref[...] = jnp.zeros_like(state_ref)

    lane = lax.broadcasted_iota(jnp.int32, (1, LANES), 1)
    head0_v = lane < HEAD_DIM
    head0_qk = jnp.bitwise_and(lane, HEAD_DIM // 2) == 0
    mavg = mavg_ref[...]
    bmask = bmask_ref[...]
    contract_last = (((1,), (1,)), ((), ()))
    contract_first = (((0,), (0,)), ((), ()))
    nblk = tm // L_RET
    units = [(p, j) for p in range(HEAD_PAIRS) for j in range(nblk)]

    def split_heads(x2, head0):
        zero = jnp.zeros_like(x2)
        return jnp.concatenate([jnp.where(head0, x2, zero), jnp.where(head0, zero, x2)], axis=0)

    q2, v2, sc, kv = {}, {}, {}, {}
    for (p, j) in units:
        rs, cs = slice(j * L_RET, (j + 1) * L_RET), slice(p * LANES, (p + 1) * LANES)
        q2[p, j] = q_ref[rs, cs]
        k2 = k_ref[rs, cs]
        v2[p, j] = v_ref[rs, cs]
        sc[p, j] = lax.dot_general(q2[p, j], split_heads(k2, head0_qk), contract_last,
                                   preferred_element_type=F32)
        kd = (k2.astype(F32) * kdec_ref[p]).astype(BF16)
        kv[p, j] = lax.dot_general(kd, v2[p, j], contract_first, preferred_element_type=F32)

    st = {}
    for p in range(HEAD_PAIRS):
        state = state_ref[p]
        for j in range(nblk):
            st[p, j] = state.astype(BF16)
            state = state * sdec_ref[p] + kv[p, j] * bmask
        state_ref[p] = state

    out = {}
    for (p, j) in units:
        pm = (sc[p, j] * dmat_ref[p]).astype(BF16)
        out[p, j] = (_dot(pm, split_heads(v2[p, j], head0_v))
                     + _dot(q2[p, j], st[p, j]) * qdec_ref[p])

    tiles = [jnp.concatenate([out[p, j] for j in range(nblk)], axis=0) for p in range(HEAD_PAIRS)]
    groups = ((jnp.concatenate(tiles[0:2], axis=1), mavg, slice(0, 2 * LANES)),
              (tiles[2], mavg[:LANES, :LANES], slice(2 * LANES, 3 * LANES)))
    means = [_dot(t.astype(BF16), m) for t, m, _ in groups]
    devs = [t - mu for (t, _, _), mu in zip(groups, means)]
    variances = [_dot((d * d).astype(BF16), m) for d, (_, m, _) in zip(devs, groups)]
    for d, var, (_, _, cs) in zip(devs, variances, groups):
        y = d * lax.rsqrt(var + EPS) * g_ref[:, cs].astype(F32)
        o_ref[:, cs] = y.astype(BF16)


def _retention(rq, rk, rv, rg, dmat, qdec, kdec, sdec, bmask, mavg):
    b, s, _ = rq.shape
    tm = TM_RET
    row_spec = pl.BlockSpec((None, tm, R_WIDTH), lambda bi, si: (bi, si, 0))
    c3 = lambda shp: pl.BlockSpec(shp, lambda bi, si: (0, 0, 0))
    c2 = lambda shp: pl.BlockSpec(shp, lambda bi, si: (0, 0))
    return pl.pallas_call(
        _retention_body,
        grid=(b, s // tm),
        in_specs=[row_spec, row_spec, row_spec, row_spec,
                  c3((HEAD_PAIRS, L_RET, 2 * L_RET)), c3((HEAD_PAIRS, L_RET, LANES)),
                  c3((HEAD_PAIRS, L_RET, LANES)), c3((HEAD_PAIRS, LANES, LANES)),
                  c2((LANES, LANES)), c2((2 * LANES, 2 * LANES))],
        out_specs=row_spec,
        out_shape=jax.ShapeDtypeStruct((b, s, R_WIDTH), BF16),
        scratch_shapes=[pltpu.VMEM((HEAD_PAIRS, LANES, LANES), F32)],
        compiler_params=pltpu.CompilerParams(
            dimension_semantics=("arbitrary", "arbitrary"), vmem_limit_bytes=VMEM_LIMIT),
        name="retention",
    )(rq, rk, rv, rg, dmat, qdec, kdec, sdec, bmask, mavg)


def _fox_body(qt_ref, qtn_ref, kaug_ref, vaug_ref, cft_ref, g_ref, o_ref, m_ref, acc_ref, sa_ref,
              sb_ref):
    qi = pl.program_id(1)
    srow_q = lax.broadcasted_iota(jnp.int32, (HEAD_DIM, 1), 0)
    q_ones = jnp.broadcast_to(jnp.where(srow_q < 3, 1.0, 0.0), (HEAD_DIM, TQ)).astype(BF16)
    qta = [jnp.concatenate([qt_ref[h * HEAD_DIM:(h + 1) * HEAD_DIM, :], q_ones], axis=0)
           for h in range(F_HEADS)]
    fq = [cft_ref[h:h + 1, :] for h in range(F_HEADS)]
    krow = lax.broadcasted_iota(jnp.int32, (TK, TQ), 0)
    qcol = lax.broadcasted_iota(jnp.int32, (TK, TQ), 1)
    causal = (krow <= qcol, krow + TK <= qcol)

    for h in range(F_HEADS):
        m_ref[h] = jnp.full((8, TQ), MASK_VALUE, F32)
        acc_ref[h] = jnp.zeros((VAUG_ROWS, TQ), F32)

    def scores(kj, dst_ref, h, last=False):
        k0 = pl.multiple_of(kj * TK, TK)
        kt = kaug_ref[h, pl.ds(k0, TK), :]
        if last:
            dst_ref[h, :, TK:TQ] = _dot(kt, qta[h][:, TK:TQ])
        else:
            dst_ref[h] = _dot(kt, qta[h])

    def consume(kj, src_ref, h, diagonal):
        k0 = pl.multiple_of(kj * TK, TK)
        st = src_ref[h]
        if diagonal:
            st = jnp.where(causal[0], st, MASK_VALUE)
        m_old = m_ref[h][0:1, :]
        m_new = jnp.maximum(m_old, jnp.max(st, axis=0, keepdims=True) + fq[h])
        p = jnp.exp2(st - (m_new - fq[h]))
        alpha = jnp.exp2(m_old - m_new)
        pv = _dot(vaug_ref[h, :, pl.ds(k0, TK)], p.astype(BF16))
        acc_ref[h] = alpha * acc_ref[h] + pv
        m_ref[h] = jnp.broadcast_to(m_new, (8, TQ))

    def consume_last(kj, src_ref, h):
        k0 = pl.multiple_of(kj * TK, TK)
        st = jnp.where(causal[1][:, TK:TQ], src_ref[h, :, TK:TQ], MASK_VALUE)
        m_old = m_ref[h][0:1, TK:TQ]
        fq_h = fq[h][:, TK:TQ]
        m_new = jnp.maximum(m_old, jnp.max(st, axis=0, keepdims=True) + fq_h)
        p = jnp.exp2(st - (m_new - fq_h))
        alpha = jnp.exp2(m_old - m_new)
        pv = _dot(vaug_ref[h, :, pl.ds(k0, TK)], p.astype(BF16))
        acc_ref[h, :, TK:TQ] = alpha * acc_ref[h, :, TK:TQ] + pv

    def overlapped(next_j, dst_ref, cur_j, src_ref, diagonal=False, next_last=False):
        lead = 2
        for h in range(lead):
            scores(next_j, dst_ref, h, next_last)
        for h in range(F_HEADS):
            consume(cur_j, src_ref, h, diagonal)
            if h + lead < F_HEADS:
                scores(next_j, dst_ref, h + lead, next_last)

    @pl.when(qi == 0)
    def _():
        for h in range(F_HEADS):
            scores(0, sa_ref, h)

    def body(i, carry):
        j0 = 2 * i
        overlapped(j0 + 1, sb_ref, j0, sa_ref)
        overlapped(j0 + 2, sa_ref, j0 + 1, sb_ref)
        return carry

    lax.fori_loop(0, qi, body, 0)
    overlapped(2 * qi + 1, sb_ref, 2 * qi, sa_ref, diagonal=True, next_last=True)
    for h in range(F_HEADS):
        consume_last(2 * qi + 1, sb_ref, h)
        qta_next = jnp.concatenate([qtn_ref[h * HEAD_DIM:(h + 1) * HEAD_DIM, :], q_ones], axis=0)
        sa_ref[h] = _dot(kaug_ref[h, 0:TK, :], qta_next)

    for p in range(HEAD_PAIRS):
        halves = []
        for hh in range(2):
            acc = acc_ref[2 * p + hh]
            halves.append(acc[0:HEAD_DIM, :] / acc[HEAD_DIM:HEAD_DIM + 1, :])
        out_t = jnp.concatenate(halves, axis=0)
        cs = slice(p * LANES, (p + 1) * LANES)
        o_ref[:, cs] = (out_t.T * g_ref[:, cs].astype(F32)).astype(BF16)


def _fox(fqt, kaug, vaug, fg, cf_t):
    b, _, s = fqt.shape
    assert TQ == 2 * TK
    row_spec = pl.BlockSpec((None, TQ, F_WIDTH), lambda bi, qi: (bi, qi, 0))
    return pl.pallas_call(
        _fox_body,
        grid=(b, s // TQ),
        in_specs=[pl.BlockSpec((None, F_WIDTH, TQ), lambda bi, qi: (bi, 0, qi)),
                  pl.BlockSpec((None, F_WIDTH, TQ),
                               lambda bi, qi: (bi, 0, jnp.minimum(qi + 1, s // TQ - 1))),
                  pl.BlockSpec((None, F_HEADS, s, LANES), lambda bi, qi: (bi, 0, 0, 0)),
                  pl.BlockSpec((None, F_HEADS, VAUG_ROWS, s), lambda bi, qi: (bi, 0, 0, 0)),
                  pl.BlockSpec((None, 8, TQ), lambda bi, qi: (bi, 0, qi)),
                  row_spec],
        out_specs=row_spec,
        out_shape=jax.ShapeDtypeStruct((b, s, F_WIDTH), BF16),
        scratch_shapes=[pltpu.VMEM((F_HEADS, 8, TQ), F32),
                        pltpu.VMEM((F_HEADS, VAUG_ROWS, TQ), F32),
                        pltpu.VMEM((F_HEADS, TK, TQ), F32),
                        pltpu.VMEM((F_HEADS, TK, TQ), F32)],
        compiler_params=pltpu.CompilerParams(
            dimension_semantics=("arbitrary", "arbitrary"), vmem_limit_bytes=VMEM_LIMIT),
        name="fox",
    )(fqt, fqt, kaug, vaug, cf_t, fg)


def _outproj_body(a_ref, r_ref, f_ref, w_ref, x_ref, pg_ref, o_ref):
    tm = x_ref.shape[0]

    def project(c):
        rows = slice(c * RC_OUT, (c + 1) * RC_OUT)
        y = jnp.concatenate([a_ref[rows, :], r_ref[rows, :], f_ref[rows, :]], axis=1)
        return _dot(y, w_ref[...])

    nchunks = tm // RC_OUT
    o_next = project(0)
    for c in range(nchunks):
        rows = slice(c * RC_OUT, (c + 1) * RC_OUT)
        o = o_next
        if c + 1 < nchunks:
            o_next = project(c + 1)
        ms = jnp.mean(o * o, axis=-1, keepdims=True)
        o_ref[rows, :] = x_ref[rows, :] + o * lax.rsqrt(ms + EPS) * pg_ref[...]


def _outproj(a, r, f, w, x, post_g):
    b, s, _ = x.shape
    tm = TM_OUT
    row_spec = lambda n: pl.BlockSpec((None, tm, n), lambda bi, si: (bi, si, 0))
    c2 = lambda shp: pl.BlockSpec(shp, lambda bi, si: (0, 0))
    return pl.pallas_call(
        _outproj_body,
        grid=(b, s // tm),
        in_specs=[row_spec(A_WIDTH), row_spec(R_WIDTH), row_spec(F_WIDTH),
                  c2((D_MODEL, D_MODEL)), row_spec(D_MODEL), c2((1, D_MODEL))],
        out_specs=row_spec(D_MODEL),
        out_shape=jax.ShapeDtypeStruct((b, s, D_MODEL), F32),
        compiler_params=pltpu.CompilerParams(
            dimension_semantics=("arbitrary", "arbitrary"), vmem_limit_bytes=VMEM_LIMIT),
        name="outproj",
    )(a, r, f, w, x, post_g)


@functools.lru_cache(maxsize=None)
def _tables(seq):
    half = HEAD_DIM // 2
    lane = np.arange(LANES)
    inv = ROPE_THETA ** (-(np.arange(half, dtype=np.float64) / half))
    ang = np.arange(seq, dtype=np.float64)[:, None] * inv[None, :]
    cos = np.cos(ang)
    sin = np.sin(ang)
    cos_t = cos[:, lane % half]
    sin_t = sin[:, lane % half] * np.where(lane < HEAD_DIM, -1.0, 1.0)[None, :]

    gam = 1.0 - np.exp2(-5.0 - np.arange(R_HEADS, dtype=np.float64))
    log_gam = np.log(gam)
    pos = np.arange(L_RET)
    dist = np.abs(pos[:, None] - pos[None, :])
    allowed = (pos[None, :] // CHUNK) <= (pos[:, None] // CHUNK)
    dmat = np.where(allowed[None], np.exp(log_gam[:, None, None] * dist[None]), 0.0)
    dmat = np.concatenate([dmat[0::2], dmat[1::2]], axis=2)
    head_of_lane = lane // HEAD_DIM
    head_of_qk_lane = (lane // half) % 2
    qdec = np.stack([np.exp(log_gam[2 * p + head_of_lane][None, :] * (pos + 1.0)[:, None])
                     for p in range(HEAD_PAIRS)])
    kdec = np.stack([np.exp(log_gam[2 * p + head_of_qk_lane][None, :] * (L_RET - 1.0 - pos)[:, None])
                     for p in range(HEAD_PAIRS)])
    bmask = (head_of_qk_lane[:, None] == head_of_lane[None, :]).astype(np.float32)
    sdec = np.stack([np.exp(log_gam[2 * p + head_of_qk_lane] * L_RET)[:, None] * bmask
                     for p in range(HEAD_PAIRS)])
    tri = (np.arange(TRI_ROWS)[None, :] <= np.arange(TRI_ROWS)[:, None]).astype(np.float32)
    grp = np.arange(A_WIDTH) // HEAD_DIM
    mavg = (grp[:, None] == grp[None, :]).astype(np.float32) / HEAD_DIM
    place = np.zeros((LANES, F_HEADS * LANES), np.float32)
    for h in range(F_HEADS):
        for term in range(3):
            place[8 * term + h, h * LANES + HEAD_DIM + term] = 1.0
    f = lambda a: np.asarray(a, np.float32)
    return dict(cos=f(cos_t), sin=f(sin_t), dmat=f(dmat), qdec=f(qdec), kdec=f(kdec),
                sdec=f(sdec), bmask=f(bmask), tri=f(tri), mavg=f(mavg),
                place=f(place))


def _layer(x, pre_g, post_g, w_in, b_f, a_ln_g, a_ws, a_bs, w_out, t):
    half = HEAD_DIM // 2
    w_t = w_in.T.astype(BF16)
    qk = w_t[A_COLS:A_COLS + 2 * R_WIDTH].reshape(2 * HEAD_PAIRS, 2, 2, half, D_MODEL)
    qk = jnp.swapaxes(qk, 1, 2).reshape(2 * R_WIDTH, D_MODEL)
    w_flg = jnp.pad(w_t[D_IN_MAIN:], ((0, LANES - F_HEADS), (0, 0)))
    bf_pad = jnp.pad(b_f, (0, LANES - F_HEADS)).reshape(1, LANES)
    abias = jnp.repeat(a_bs.T, HEAD_DIM, axis=1)
    (a_out, rq, rk, rv, rg, fqt, kaug, vaug, fg, cf_t) = _inproj(
        x, pre_g.reshape(1, D_MODEL), w_t, qk, w_flg, bf_pad, a_ln_g.reshape(1, A_WIDTH), a_ws,
        abias, t["cos"], t["sin"], t["tri"], t["mavg"], t["place"])
    r_out = _retention(rq, rk, rv, rg, t["dmat"], t["qdec"], t["kdec"], t["sdec"], t["bmask"],
                       t["mavg"])
    f_out = _fox(fqt, kaug, vaug, fg, cf_t)
    return _outproj(a_out, r_out, f_out, w_out.astype(BF16), x, post_g.reshape(1, D_MODEL))


def kernel(x, pre_gain, post_gain, w_in, b_forget, a_norm_gain, a_spatial_w, a_spatial_b, w_out):
    bf16_tables = ("tri", "mavg", "place")
    t = {k: jnp.asarray(v, BF16 if k in bf16_tables else F32) for k, v in _tables(x.shape[1]).items()}
    for l in range(pre_gain.shape[0]):
        x = _layer(x, pre_gain[l], post_gain[l], w_in[l], b_forget[l], a_norm_gain[l],
                   a_spatial_w[l], a_spatial_b[l], w_out[l], t)
    return x
```

```python
import functools
import math

import jax
import jax.numpy as jnp
import numpy as np
from jax import lax
from jax.experimental import pallas as pl
from jax.experimental.pallas import tpu as pltpu

F32 = jnp.float32
BF16 = jnp.bfloat16

D_MODEL = 1024
HEAD_DIM = 64
CHUNK = 64
A_WIDTH = 256
R_WIDTH = 384
F_WIDTH = 384
A_GROUPS = 4
R_HEADS = 6
F_HEADS = 6
A_BLOCK = 128
ROPE_THETA = 10000.0
EPS = 1e-6
LANES = 128
SUBLANES = 8
BF16_SUBLANES = 16
HEAD_PAIRS = R_WIDTH // LANES
N_F_TERMS = 3
F_TERM_STRIDE = 8

A_COLS = 3 * A_WIDTH
R_COLS = 4 * R_WIDTH
F_COLS = 4 * F_WIDTH + F_HEADS
D_IN = A_COLS + R_COLS + F_COLS
D_IN_MAIN = A_COLS + R_COLS + 4 * F_WIDTH

TM_IN = 1024
RC_IN = 512
TRI_ROWS = 256
TM_RET = 2048
L_RET = 128
TQ = 512
TK = 256
TM_OUT = 2048
RC_OUT = 512
VMEM_LIMIT = 56 * 1024 * 1024
VAUG_ROWS = HEAD_DIM + BF16_SUBLANES
LOG2E = 1.4426950408889634
MASK_VALUE = -1e30


def _silu(x):
    return 0.5 * x * (1.0 + jnp.tanh(0.5 * x))


def _gelu_tanh(x):
    c = math.sqrt(2.0 / math.pi)
    return 0.5 * x * (1.0 + jnp.tanh(c * (x + 0.044715 * (x * x * x))))


def _log_sigmoid(x):
    return jnp.minimum(x, 0.0) - jnp.log1p(jnp.exp(-jnp.abs(x)))


def _dot(a, b):
    return jnp.dot(a, b, preferred_element_type=F32)


def _group_norm64(x, mavg):
    mean = _dot(x.astype(BF16), mavg)
    d = x - mean
    var = _dot((d * d).astype(BF16), mavg)
    return d * lax.rsqrt(var + EPS)


def _inproj_body(x_ref, pg_ref, w_ref, wqk_ref, wflg_ref, bf_ref, aln_ref, aws_ref, abias_ref,
                 cos_ref, sin_ref,
                 tri_ref, mavg_ref, place_ref,
                 a_ref, rq_ref, rk_ref, rv_ref, rg_ref, fq_ref, kaug_ref, vaug_ref, fg_ref, cft_ref,
                 carry_ref):
    tm = x_ref.shape[0]

    @pl.when(pl.program_id(1) == 0)
    def _():
        carry_ref[...] = jnp.zeros_like(carry_ref)

    mavg = mavg_ref[...]
    tri = tri_ref[...]
    row = lax.broadcasted_iota(jnp.int32, (A_BLOCK, A_BLOCK), 0)
    col = lax.broadcasted_iota(jnp.int32, (A_BLOCK, A_BLOCK), 1)
    allowed = jnp.logical_not(jnp.logical_and(row < CHUNK, col >= CHUNK))
    wcat = jnp.concatenate(
        [jnp.where(allowed, aws_ref[gi], 0.0).astype(BF16) for gi in range(A_GROUPS)], axis=1)
    lane_grp = lax.shift_right_logical(
        lax.broadcasted_iota(jnp.int32, (A_BLOCK, A_WIDTH), 1), HEAD_DIM.bit_length() - 1)
    zero_bf = jnp.zeros((A_BLOCK, A_WIDTH), BF16)
    lane = lax.broadcasted_iota(jnp.int32, (1, LANES), 1)
    srow = lax.broadcasted_iota(jnp.int32, (VAUG_ROWS - HEAD_DIM, 1), 0)
    ones_rows = jnp.broadcast_to(jnp.where(srow == 0, 1.0, 0.0),
                                 (VAUG_ROWS - HEAD_DIM, RC_IN)).astype(BF16)

    def rope(z, cos_t, sin_t):
        outs = []
        for c in range(HEAD_PAIRS):
            zc = z[:, c * LANES:(c + 1) * LANES]
            outs.append(zc * cos_t + pltpu.roll(zc, HEAD_DIM, 1) * sin_t)
        return jnp.concatenate(outs, axis=1)

    def normed(c):
        x = x_ref[c * RC_IN:(c + 1) * RC_IN, :]
        ms = jnp.mean(x * x, axis=-1, keepdims=True)
        return (x * lax.rsqrt(ms + EPS) * pg_ref[...]).astype(BF16)

    contract_last = (((1,), (1,)), ((), ()))

    def nt_dot(h, w):
        return lax.dot_general(h, w, contract_last, preferred_element_type=F32)

    def wide(h, k):
        if k == 0:
            return nt_dot(h, w_ref[0:A_COLS, :])
        if k == 1:
            return jnp.concatenate(
                [nt_dot(h, wqk_ref[...]),
                 nt_dot(h, w_ref[A_COLS + 2 * R_WIDTH:A_COLS + R_COLS, :])], axis=1)
        return jnp.concatenate(
            [nt_dot(h, w_ref[A_COLS + R_COLS:D_IN_MAIN, :]), nt_dot(h, wflg_ref[...])], axis=1)

    def rows_of(c):
        return slice(c * RC_IN, (c + 1) * RC_IN)

    def a1(c, za, e):
        e["u"] = _gelu_tanh(za[:, 0:A_WIDTH])
        e["v"] = _gelu_tanh(za[:, A_WIDTH:2 * A_WIDTH])
        e["g"] = _silu(za[:, 2 * A_WIDTH:3 * A_WIDTH])
        e["mean"] = _dot(e["v"].astype(BF16), mavg)

    def a2(c, za, e):
        e["d"] = e["v"] - e["mean"]
        e["var"] = _dot((e["d"] * e["d"]).astype(BF16), mavg)

    def a3(c, za, e):
        vn = (e["d"] * lax.rsqrt(e["var"] + EPS) * aln_ref[...]).astype(BF16)
        mixed_blocks = []
        for nb in range(RC_IN // A_BLOCK):
            vb = vn[nb * A_BLOCK:(nb + 1) * A_BLOCK, :]
            vstack = jnp.concatenate(
                [jnp.where(lane_grp == gi, vb, zero_bf) for gi in range(A_GROUPS)], axis=0)
            mixed_blocks.append(_dot(wcat, vstack) + abias_ref[...])
        e["mixed"] = jnp.concatenate(mixed_blocks, axis=0)

    def a4(c, za, e):
        a_ref[rows_of(c), :] = (e["u"] * e["mixed"] * e["g"]).astype(BF16)

    def r1(c, zr, e):
        rows = rows_of(c)
        cos_t = cos_ref[rows, :]
        sin_t = sin_ref[rows, :]
        rq_ref[rows, :] = (rope(zr[:, 0:R_WIDTH], cos_t, sin_t) * (HEAD_DIM ** -0.5)).astype(BF16)
        rk_ref[rows, :] = rope(zr[:, R_WIDTH:2 * R_WIDTH], cos_t, sin_t).astype(BF16)
        rv_ref[rows, :] = zr[:, 2 * R_WIDTH:3 * R_WIDTH].astype(BF16)
        rg_ref[rows, :] = _silu(zr[:, 3 * R_WIDTH:4 * R_WIDTH]).astype(BF16)

    def f1(c, zf, e):
        ls = _log_sigmoid(zf[:, 4 * F_WIDTH:4 * F_WIDTH + LANES] + bf_ref[...])
        ls_hi = ls.astype(BF16)
        ls_lo = (ls - ls_hi.astype(F32)).astype(BF16)
        carry = carry_ref[...]
        blocks = []
        for blk in range(RC_IN // TRI_ROWS):
            rs = slice(blk * TRI_ROWS, (blk + 1) * TRI_ROWS)
            cf_blk = _dot(tri, ls_hi[rs]) + _dot(tri, ls_lo[rs]) + carry
            carry = cf_blk[TRI_ROWS - 1:TRI_ROWS, :]
            blocks.append(cf_blk)
        carry_ref[...] = carry
        e["cf2"] = jnp.concatenate(blocks, axis=0) * LOG2E

    def f2(c, zf, e):
        cf2 = e["cf2"]
        cft_ref[:, rows_of(c)] = cf2.T[:SUBLANES, :]
        negf = -cf2
        hi = negf.astype(BF16).astype(F32)
        mid = (negf - hi).astype(BF16).astype(F32)
        lo = (negf - hi - mid).astype(BF16).astype(F32)
        packed = jnp.where(
            lane < F_TERM_STRIDE, hi,
            jnp.where(lane < 2 * F_TERM_STRIDE, pltpu.roll(mid, F_TERM_STRIDE, 1),
                      jnp.where(lane < 3 * F_TERM_STRIDE, pltpu.roll(lo, 2 * F_TERM_STRIDE, 1),
                                0.0)))
        e["placed"] = _dot(packed.astype(BF16), place_ref[...])

    def f3(c, zf, e):
        rows = rows_of(c)
        fq_ref[:, rows] = (zf[:, 0:F_WIDTH] * (HEAD_DIM ** -0.5 * LOG2E)).T.astype(BF16)
        fg_ref[rows, :] = _silu(zf[:, 3 * F_WIDTH:4 * F_WIDTH]).astype(BF16)
        fv_t = zf[:, 2 * F_WIDTH:3 * F_WIDTH].T.astype(BF16)
        for hd in range(F_HEADS):
            p, hh = divmod(hd, 2)
            kk = zf[:, F_WIDTH + p * LANES:F_WIDTH + (p + 1) * LANES]
            if hh == 1:
                kk = pltpu.roll(kk, HEAD_DIM, 1)
            aug = e["placed"][:, hd * LANES:(hd + 1) * LANES]
            kaug_ref[hd, rows, :] = jnp.where(lane < HEAD_DIM, kk, aug).astype(BF16)
            vaug_ref[hd, 0:HEAD_DIM, rows] = fv_t[hd * HEAD_DIM:(hd + 1) * HEAD_DIM, :]
            vaug_ref[hd, HEAD_DIM:VAUG_ROWS, rows] = ones_rows

    nchunks = tm // RC_IN
    hs = [normed(c) for c in range(nchunks)]
    groups = ((0, (a1, a2, a3, a4)), (2, (f1, f2, f3)), (1, (r1,)))
    issued = []

    def run_slot(t):
        for s, (c, z, stages, e) in enumerate(issued):
            j = t - 1 - s
            if 0 <= j < len(stages):
                stages[j](c, z, e)

    for k, stages in groups:
        for c in range(nchunks):
            issued.append((c, wide(hs[c], k), stages, {}))
            run_slot(len(issued) - 1)
    depth = max(len(stages) for _, stages in groups)
    for t in range(len(issued), len(issued) + depth):
        run_slot(t)


def _inproj(x, pre_g, w_t, w_qk, w_flg, bf_pad, aln, aws, abias, cos_t, sin_t, tri, mavg, place):
    b, s, _ = x.shape
    tm = TM_IN
    grid = (b, s // tm)
    row_spec = lambda n: pl.BlockSpec((None, tm, n), lambda bi, si: (bi, si, 0))
    col_spec = lambda n: pl.BlockSpec((None, n, tm), lambda bi, si: (bi, 0, si))
    const2 = lambda shp: pl.BlockSpec(shp, lambda bi, si: (0, 0))
    bf_sds = lambda n: jax.ShapeDtypeStruct((b, s, n), BF16)
    out_shape = ([bf_sds(A_WIDTH)] + [bf_sds(R_WIDTH)] * 4
                 + [jax.ShapeDtypeStruct((b, F_WIDTH, s), BF16),
                    jax.ShapeDtypeStruct((b, F_HEADS, s, LANES), BF16),
                    jax.ShapeDtypeStruct((b, F_HEADS, VAUG_ROWS, s), BF16),
                    bf_sds(F_WIDTH),
                    jax.ShapeDtypeStruct((b, SUBLANES, s), F32)])
    out_specs = ([row_spec(A_WIDTH)] + [row_spec(R_WIDTH)] * 4
                 + [col_spec(F_WIDTH),
                    pl.BlockSpec((None, F_HEADS, tm, LANES), lambda bi, si: (bi, 0, si, 0)),
                    pl.BlockSpec((None, F_HEADS, VAUG_ROWS, tm), lambda bi, si: (bi, 0, 0, si)),
                    row_spec(F_WIDTH),
                    col_spec(SUBLANES)])
    in_specs = [
        row_spec(D_MODEL),
        const2((1, D_MODEL)),
        pl.BlockSpec((D_IN, D_MODEL), lambda bi, si: (0, 0), pipeline_mode=pl.Buffered(1)),
        const2((2 * R_WIDTH, D_MODEL)),
        const2((LANES, D_MODEL)),
        const2((1, LANES)),
        const2((1, A_WIDTH)),
        pl.BlockSpec((A_GROUPS, A_BLOCK, A_BLOCK), lambda bi, si: (0, 0, 0)),
        const2((A_BLOCK, A_WIDTH)),
        pl.BlockSpec((tm, LANES), lambda bi, si: (si, 0)),
        pl.BlockSpec((tm, LANES), lambda bi, si: (si, 0)),
        const2((TRI_ROWS, TRI_ROWS)),
        const2((A_WIDTH, A_WIDTH)),
        const2((LANES, F_HEADS * LANES)),
    ]
    return pl.pallas_call(
        _inproj_body,
        grid=grid,
        in_specs=in_specs,
        out_specs=out_specs,
        out_shape=out_shape,
        scratch_shapes=[pltpu.VMEM((1, LANES), F32)],
        compiler_params=pltpu.CompilerParams(
            dimension_semantics=("arbitrary", "arbitrary"), vmem_limit_bytes=VMEM_LIMIT),
        name="inproj",
    )(x, pre_g, w_t, w_qk, w_flg, bf_pad, aln, aws, abias, cos_t, sin_t, tri, mavg, place)


def _retention_body(q_ref, k_ref, v_ref, g_ref, dmat_ref, qdec_ref, kdec_ref, sdec_ref, bmask_ref,
                    mavg_ref, o_ref, state_ref):
    tm = q_ref.shape[0]

    @pl.when(pl.program_id(1) == 0)
    def _():
        state_ref[...] = jnp.zeros_like(state_ref)

    lane = lax.broadcasted_iota(jnp.int32, (1, LANES), 1)
    head0_v = lane < HEAD_DIM
    head0_qk = jnp.bitwise_and(lane, HEAD_DIM // 2) == 0
    mavg = mavg_ref[...]
    bmask = bmask_ref[...]
    contract_last = (((1,), (1,)), ((), ()))
    contract_first = (((0,), (0,)), ((), ()))
    nblk = tm // L_RET
    units = [(p, j) for p in range(HEAD_PAIRS) for j in range(nblk)]

    def split_heads(x2, head0):
        zero = jnp.zeros_like(x2)
        return jnp.concatenate([jnp.where(head0, x2, zero), jnp.where(head0, zero, x2)], axis=0)

    q2, v2, sc, kv = {}, {}, {}, {}
    for (p, j) in units:
        rs, cs = slice(j * L_RET, (j + 1) * L_RET), slice(p * LANES, (p + 1) * LANES)
        q2[p, j] = q_ref[rs, cs]
        k2 = k_ref[rs, cs]
        v2[p, j] = v_ref[rs, cs]
        sc[p, j] = lax.dot_general(q2[p, j], split_heads(k2, head0_qk), contract_last,
                                   preferred_element_type=F32)
        kd = (k2.astype(F32) * kdec_ref[p]).astype(BF16)
        kv[p, j] = lax.dot_general(kd, v2[p, j], contract_first, preferred_element_type=F32)

    st = {}
    for p in range(HEAD_PAIRS):
        state = state_ref[p]
        for j in range(nblk):
            st[p, j] = state.astype(BF16)
            state = state * sdec_ref[p] + kv[p, j] * bmask
        state_ref[p] = state

    out = {}
    for (p, j) in units:
        pm = (sc[p, j] * dmat_ref[p]).astype(BF16)
        out[p, j] = (_dot(pm, split_heads(v2[p, j], head0_v))
                     + _dot(q2[p, j], st[p, j]) * qdec_ref[p])

    tiles = [jnp.concatenate([out[p, j] for j in range(nblk)], axis=0) for p in range(HEAD_PAIRS)]
    groups = ((jnp.concatenate(tiles[0:2], axis=1), mavg, slice(0, 2 * LANES)),
              (tiles[2], mavg[:LANES, :LANES], slice(2 * LANES, 3 * LANES)))
    means = [_dot(t.astype(BF16), m) for t, m, _ in groups]
    devs = [t - mu for (t, _, _), mu in zip(groups, means)]
    variances = [_dot((d * d).astype(BF16), m) for d, (_, m, _) in zip(devs, groups)]
    for d, var, (_, _, cs) in zip(devs, variances, groups):
        y = d * lax.rsqrt(var + EPS) * g_ref[:, cs].astype(F32)
        o_ref[:, cs] = y.astype(BF16)


def _retention(rq, rk, rv, rg, dmat, qdec, kdec, sdec, bmask, mavg):
    b, s, _ = rq.shape
    tm = TM_RET
    row_spec = pl.BlockSpec((None, tm, R_WIDTH), lambda bi, si: (bi, si, 0))
    c3 = lambda shp: pl.BlockSpec(shp, lambda bi, si: (0, 0, 0))
    c2 = lambda shp: pl.BlockSpec(shp, lambda bi, si: (0, 0))
    return pl.pallas_call(
        _retention_body,
        grid=(b, s // tm),
        in_specs=[row_spec, row_spec, row_spec, row_spec,
                  c3((HEAD_PAIRS, L_RET, 2 * L_RET)), c3((HEAD_PAIRS, L_RET, LANES)),
                  c3((HEAD_PAIRS, L_RET, LANES)), c3((HEAD_PAIRS, LANES, LANES)),
                  c2((LANES, LANES)), c2((2 * LANES, 2 * LANES))],
        out_specs=row_spec,
        out_shape=jax.ShapeDtypeStruct((b, s, R_WIDTH), BF16),
        scratch_shapes=[pltpu.VMEM((HEAD_PAIRS, LANES, LANES), F32)],
        compiler_params=pltpu.CompilerParams(
            dimension_semantics=("arbitrary", "arbitrary"), vmem_limit_bytes=VMEM_LIMIT),
        name="retention",
    )(rq, rk, rv, rg, dmat, qdec, kdec, sdec, bmask, mavg)


def _fox_body(qt_ref, qtn_ref, kaug_ref, vaug_ref, cft_ref, g_ref, o_ref, m_ref, acc_ref, sa_ref,
              sb_ref):
    qi = pl.program_id(1)
    srow_q = lax.broadcasted_iota(jnp.int32, (HEAD_DIM, 1), 0)
    q_ones = jnp.broadcast_to(jnp.where(srow_q < N_F_TERMS, 1.0, 0.0),
                              (HEAD_DIM, TQ)).astype(BF16)
    qta = [jnp.concatenate([qt_ref[h * HEAD_DIM:(h + 1) * HEAD_DIM, :], q_ones], axis=0)
           for h in range(F_HEADS)]
    fq = [cft_ref[h:h + 1, :] for h in range(F_HEADS)]
    krow = lax.broadcasted_iota(jnp.int32, (TK, TQ), 0)
    qcol = lax.broadcasted_iota(jnp.int32, (TK, TQ), 1)
    causal = (krow <= qcol, krow + TK <= qcol)

    for h in range(F_HEADS):
        m_ref[h] = jnp.full((SUBLANES, TQ), MASK_VALUE, F32)
        acc_ref[h] = jnp.zeros((VAUG_ROWS, TQ), F32)

    def scores(kj, dst_ref, h, last=False):
        k0 = pl.multiple_of(kj * TK, TK)
        kt = kaug_ref[h, pl.ds(k0, TK), :]
        if last:
            dst_ref[h, :, TK:TQ] = _dot(kt, qta[h][:, TK:TQ])
        else:
            dst_ref[h] = _dot(kt, qta[h])

    def consume(kj, src_ref, h, diagonal):
        k0 = pl.multiple_of(kj * TK, TK)
        st = src_ref[h]
        if diagonal:
            st = jnp.where(causal[0], st, MASK_VALUE)
        m_old = m_ref[h][0:1, :]
        m_new = jnp.maximum(m_old, jnp.max(st, axis=0, keepdims=True) + fq[h])
        p = jnp.exp2(st - (m_new - fq[h]))
        alpha = jnp.exp2(m_old - m_new)
        pv = _dot(vaug_ref[h, :, pl.ds(k0, TK)], p.astype(BF16))
        acc_ref[h] = alpha * acc_ref[h] + pv
        m_ref[h] = jnp.broadcast_to(m_new, (SUBLANES, TQ))

    def consume_last(kj, src_ref, h):
        k0 = pl.multiple_of(kj * TK, TK)
        st = jnp.where(causal[1][:, TK:TQ], src_ref[h, :, TK:TQ], MASK_VALUE)
        m_old = m_ref[h][0:1, TK:TQ]
        fq_h = fq[h][:, TK:TQ]
        m_new = jnp.maximum(m_old, jnp.max(st, axis=0, keepdims=True) + fq_h)
        p = jnp.exp2(st - (m_new - fq_h))
        alpha = jnp.exp2(m_old - m_new)
        pv = _dot(vaug_ref[h, :, pl.ds(k0, TK)], p.astype(BF16))
        acc_ref[h, :, TK:TQ] = alpha * acc_ref[h, :, TK:TQ] + pv

    def overlapped(next_j, dst_ref, cur_j, src_ref, diagonal=False, next_last=False):
        lead = 2
        for h in range(lead):
            scores(next_j, dst_ref, h, next_last)
        for h in range(F_HEADS):
            consume(cur_j, src_ref, h, diagonal)
            if h + lead < F_HEADS:
                scores(next_j, dst_ref, h + lead, next_last)

    @pl.when(qi == 0)
    def _():
        for h in range(F_HEADS):
            scores(0, sa_ref, h)

    def body(i, carry):
        j0 = 2 * i
        overlapped(j0 + 1, sb_ref, j0, sa_ref)
        overlapped(j0 + 2, sa_ref, j0 + 1, sb_ref)
        return carry

    lax.fori_loop(0, qi, body, 0)
    overlapped(2 * qi + 1, sb_ref, 2 * qi, sa_ref, diagonal=True, next_last=True)
    for h in range(F_HEADS):
        consume_last(2 * qi + 1, sb_ref, h)
        qta_next = jnp.concatenate([qtn_ref[h * HEAD_DIM:(h + 1) * HEAD_DIM, :], q_ones], axis=0)
        sa_ref[h] = _dot(kaug_ref[h, 0:TK, :], qta_next)

    for p in range(HEAD_PAIRS):
        halves = []
        for hh in range(2):
            acc = acc_ref[2 * p + hh]
            halves.append(acc[0:HEAD_DIM, :] / acc[HEAD_DIM:HEAD_DIM + 1, :])
        out_t = jnp.concatenate(halves, axis=0)
        cs = slice(p * LANES, (p + 1) * LANES)
        o_ref[:, cs] = (out_t.T * g_ref[:, cs].astype(F32)).astype(BF16)


def _fox(fqt, kaug, vaug, fg, cf_t):
    b, _, s = fqt.shape
    assert TQ == 2 * TK
    row_spec = pl.BlockSpec((None, TQ, F_WIDTH), lambda bi, qi: (bi, qi, 0))
    return pl.pallas_call(
        _fox_body,
        grid=(b, s // TQ),
        in_specs=[pl.BlockSpec((None, F_WIDTH, TQ), lambda bi, qi: (bi, 0, qi)),
                  pl.BlockSpec((None, F_WIDTH, TQ),
                               lambda bi, qi: (bi, 0, jnp.minimum(qi + 1, s // TQ - 1))),
                  pl.BlockSpec((None, F_HEADS, s, LANES), lambda bi, qi: (bi, 0, 0, 0)),
                  pl.BlockSpec((None, F_HEADS, VAUG_ROWS, s), lambda bi, qi: (bi, 0, 0, 0)),
                  pl.BlockSpec((None, SUBLANES, TQ), lambda bi, qi: (bi, 0, qi)),
                  row_spec],
        out_specs=row_spec,
        out_shape=jax.ShapeDtypeStruct((b, s, F_WIDTH), BF16),
        scratch_shapes=[pltpu.VMEM((F_HEADS, SUBLANES, TQ), F32),
                        pltpu.VMEM((F_HEADS, VAUG_ROWS, TQ), F32),
                        pltpu.VMEM((F_HEADS, TK, TQ), F32),
                        pltpu.VMEM((F_HEADS, TK, TQ), F32)],
        compiler_params=pltpu.CompilerParams(
            dimension_semantics=("arbitrary", "arbitrary"), vmem_limit_bytes=VMEM_LIMIT),
        name="fox",
    )(fqt, fqt, kaug, vaug, cf_t, fg)


def _outproj_body(a_ref, r_ref, f_ref, w_ref, x_ref, pg_ref, o_ref):
    tm = x_ref.shape[0]

    def project(c):
        rows = slice(c * RC_OUT, (c + 1) * RC_OUT)
        y = jnp.concatenate([a_ref[rows, :], r_ref[rows, :], f_ref[rows, :]], axis=1)
        return _dot(y, w_ref[...])

    nchunks = tm // RC_OUT
    o_next = project(0)
    for c in range(nchunks):
        rows = slice(c * RC_OUT, (c + 1) * RC_OUT)
        o = o_next
        if c + 1 < nchunks:
            o_next = project(c + 1)
        ms = jnp.mean(o * o, axis=-1, keepdims=True)
        o_ref[rows, :] = x_ref[rows, :] + o * lax.rsqrt(ms + EPS) * pg_ref[...]


def _outproj(a, r, f, w, x, post_g):
    b, s, _ = x.shape
    tm = TM_OUT
    row_spec = lambda n: pl.BlockSpec((None, tm, n), lambda bi, si: (bi, si, 0))
    c2 = lambda shp: pl.BlockSpec(shp, lambda bi, si: (0, 0))
    return pl.pallas_call(
        _outproj_body,
        grid=(b, s // tm),
        in_specs=[row_spec(A_WIDTH), row_spec(R_WIDTH), row_spec(F_WIDTH),
                  c2((D_MODEL, D_MODEL)), row_spec(D_MODEL), c2((1, D_MODEL))],
        out_specs=row_spec(D_MODEL),
        out_shape=jax.ShapeDtypeStruct((b, s, D_MODEL), F32),
        compiler_params=pltpu.CompilerParams(
            dimension_semantics=("arbitrary", "arbitrary"), vmem_limit_bytes=VMEM_LIMIT),
        name="outproj",
    )(a, r, f, w, x, post_g)


@functools.lru_cache(maxsize=None)
def _tables(seq):
    half = HEAD_DIM // 2
    lane = np.arange(LANES)
    inv = ROPE_THETA ** (-(np.arange(half, dtype=np.float64) / half))
    ang = np.arange(seq, dtype=np.float64)[:, None] * inv[None, :]
    cos = np.cos(ang)
    sin = np.sin(ang)
    cos_t = cos[:, lane % half]
    sin_t = sin[:, lane % half] * np.where(lane < HEAD_DIM, -1.0, 1.0)[None, :]

    gam = 1.0 - np.exp2(-5.0 - np.arange(R_HEADS, dtype=np.float64))
    log_gam = np.log(gam)
    pos = np.arange(L_RET)
    dist = np.abs(pos[:, None] - pos[None, :])
    allowed = (pos[None, :] // CHUNK) <= (pos[:, None] // CHUNK)
    dmat = np.where(allowed[None], np.exp(log_gam[:, None, None] * dist[None]), 0.0)
    dmat = np.concatenate([dmat[0::2], dmat[1::2]], axis=2)
    head_of_lane = lane // HEAD_DIM
    head_of_qk_lane = (lane // half) % 2
    qdec = np.stack([np.exp(log_gam[2 * p + head_of_lane][None, :] * (pos + 1.0)[:, None])
                     for p in range(HEAD_PAIRS)])
    kdec = np.stack([np.exp(log_gam[2 * p + head_of_qk_lane][None, :] * (L_RET - 1.0 - pos)[:, None])
                     for p in range(HEAD_PAIRS)])
    bmask = (head_of_qk_lane[:, None] == head_of_lane[None, :]).astype(np.float32)
    sdec = np.stack([np.exp(log_gam[2 * p + head_of_qk_lane] * L_RET)[:, None] * bmask
                     for p in range(HEAD_PAIRS)])
    tri = (np.arange(TRI_ROWS)[None, :] <= np.arange(TRI_ROWS)[:, None]).astype(np.float32)
    grp = np.arange(A_WIDTH) // HEAD_DIM
    mavg = (grp[:, None] == grp[None, :]).astype(np.float32) / HEAD_DIM
    place = np.zeros((LANES, F_HEADS * LANES), np.float32)
    for h in range(F_HEADS):
        for term in range(N_F_TERMS):
            place[F_TERM_STRIDE * term + h, h * LANES + HEAD_DIM + term] = 1.0
    f = lambda a: np.asarray(a, np.float32)
    return dict(cos=f(cos_t), sin=f(sin_t), dmat=f(dmat), qdec=f(qdec), kdec=f(kdec),
                sdec=f(sdec), bmask=f(bmask), tri=f(tri), mavg=f(mavg),
                place=f(place))


def _layer(x, pre_g, post_g, w_in, b_f, a_ln_g, a_ws, a_bs, w_out, t):
    half = HEAD_DIM // 2
    w_t = w_in.T.astype(BF16)
    qk = w_t[A_COLS:A_COLS + 2 * R_WIDTH].reshape(2 * HEAD_PAIRS, 2, 2, half, D_MODEL)
    qk = jnp.swapaxes(qk, 1, 2).reshape(2 * R_WIDTH, D_MODEL)
    w_flg = jnp.pad(w_t[D_IN_MAIN:], ((0, LANES - F_HEADS), (0, 0)))
    bf_pad = jnp.pad(b_f, (0, LANES - F_HEADS)).reshape(1, LANES)
    abias = jnp.repeat(a_bs.T, HEAD_DIM, axis=1)
    (a_out, rq, rk, rv, rg, fqt, kaug, vaug, fg, cf_t) = _inproj(
        x, pre_g.reshape(1, D_MODEL), w_t, qk, w_flg, bf_pad, a_ln_g.reshape(1, A_WIDTH), a_ws,
        abias, t["cos"], t["sin"], t["tri"], t["mavg"], t["place"])
    r_out = _retention(rq, rk, rv, rg, t["dmat"], t["qdec"], t["kdec"], t["sdec"], t["bmask"],
                       t["mavg"])
    f_out = _fox(fqt, kaug, vaug, fg, cf_t)
    return _outproj(a_out, r_out, f_out, w_out.astype(BF16), x, post_g.reshape(1, D_MODEL))


def kernel(x, pre_gain, post_gain, w_in, b_forget, a_norm_gain, a_spatial_w, a_spatial_b, w_out):
    _, s, d = x.shape
    assert d == D_MODEL and all(s % t == 0 for t in (TM_IN, TM_RET, TM_OUT, TQ)), x.shape
    assert w_in.shape[1:] == (D_MODEL, D_IN) and w_out.shape[1:] == (D_MODEL, D_MODEL)
    bf16_tables = ("tri", "mavg", "place")
    t = {k: jnp.asarray(v, BF16 if k in bf16_tables else F32) for k, v in _tables(x.shape[1]).items()}
    for l in range(pre_gain.shape[0]):
        x = _layer(x, pre_gain[l], post_gain[l], w_in[l], b_forget[l], a_norm_gain[l],
                   a_spatial_w[l], a_spatial_b[l], w_out[l], t)
    return x
```

```python
import functools
import math

import jax
import jax.numpy as jnp
import numpy as np
from jax import lax
from jax.experimental import pallas as pl
from jax.experimental.pallas import tpu as pltpu

F32 = jnp.float32
BF16 = jnp.bfloat16

D_MODEL = 1024
HEAD_DIM = 64
CHUNK = 64
A_WIDTH = 256
R_WIDTH = 384
F_WIDTH = 384
A_GROUPS = 4
R_HEADS = 6
F_HEADS = 6
A_BLOCK = 128
ROPE_THETA = 10000.0
EPS = 1e-6
LANES = 128
SUBLANES = 8
BF16_SUBLANES = 16
HEAD_PAIRS = R_WIDTH // LANES
N_F_TERMS = 3
F_TERM_STRIDE = 8

A_COLS = 3 * A_WIDTH
R_COLS = 4 * R_WIDTH
F_COLS = 4 * F_WIDTH + F_HEADS
D_IN = A_COLS + R_COLS + F_COLS
D_IN_MAIN = A_COLS + R_COLS + 4 * F_WIDTH

TM_IN = 1024
RC_IN = 512
TRI_ROWS = 256
TM_RET = 2048
L_RET = 128
TQ = 512
TK = 256
TM_OUT = 2048
RC_OUT = 512
VMEM_LIMIT = 56 * 1024 * 1024
VAUG_ROWS = HEAD_DIM + BF16_SUBLANES
LOG2E = 1.4426950408889634
MASK_VALUE = -1e30


def _silu(x):
    return 0.5 * x * (1.0 + jnp.tanh(0.5 * x))


def _gelu_tanh(x):
    c = math.sqrt(2.0 / math.pi)
    return 0.5 * x * (1.0 + jnp.tanh(c * (x + 0.044715 * (x * x * x))))


def _log_sigmoid(x):
    return jnp.minimum(x, 0.0) - jnp.log1p(jnp.exp(-jnp.abs(x)))


def _dot(a, b):
    return jnp.dot(a, b, preferred_element_type=F32)


def _group_norm64(x, mavg):
    mean = _dot(x.astype(BF16), mavg)
    d = x - mean
    var = _dot((d * d).astype(BF16), mavg)
    return d * lax.rsqrt(var + EPS)


def _inproj_body(x_ref, pg_ref, w_ref, wqk_ref, wflg_ref, bf_ref, aln_ref, aws_ref, abias_ref,
                 cos_ref, sin_ref,
                 tri_ref, mavg_ref, place_ref,
                 a_ref, rq_ref, rk_ref, rv_ref, rg_ref, fq_ref, kaug_ref, vaug_ref, fg_ref, cft_ref,
                 carry_ref):
    tm = x_ref.shape[0]

    @pl.when(pl.program_id(1) == 0)
    def _():
        carry_ref[...] = jnp.zeros_like(carry_ref)

    mavg = mavg_ref[...]
    tri = tri_ref[...]
    row = lax.broadcasted_iota(jnp.int32, (A_BLOCK, A_BLOCK), 0)
    col = lax.broadcasted_iota(jnp.int32, (A_BLOCK, A_BLOCK), 1)
    allowed = jnp.logical_not(jnp.logical_and(row < CHUNK, col >= CHUNK))
    wcat = jnp.concatenate(
        [jnp.where(allowed, aws_ref[gi], 0.0).astype(BF16) for gi in range(A_GROUPS)], axis=1)
    lane_grp = lax.shift_right_logical(
        lax.broadcasted_iota(jnp.int32, (A_BLOCK, A_WIDTH), 1), HEAD_DIM.bit_length() - 1)
    zero_bf = jnp.zeros((A_BLOCK, A_WIDTH), BF16)
    lane = lax.broadcasted_iota(jnp.int32, (1, LANES), 1)
    srow = lax.broadcasted_iota(jnp.int32, (VAUG_ROWS - HEAD_DIM, 1), 0)
    ones_rows = jnp.broadcast_to(jnp.where(srow == 0, 1.0, 0.0),
                                 (VAUG_ROWS - HEAD_DIM, RC_IN)).astype(BF16)

    def rope(z, cos_t, sin_t):
        outs = []
        for c in range(HEAD_PAIRS):
            zc = z[:, c * LANES:(c + 1) * LANES]
            outs.append(zc * cos_t + pltpu.roll(zc, HEAD_DIM, 1) * sin_t)
        return jnp.concatenate(outs, axis=1)

    def normed(c):
        x = x_ref[c * RC_IN:(c + 1) * RC_IN, :]
        ms = jnp.mean(x * x, axis=-1, keepdims=True)
        return (x * lax.rsqrt(ms + EPS) * pg_ref[...]).astype(BF16)

    contract_last = (((1,), (1,)), ((), ()))

    def nt_dot(h, w):
        return lax.dot_general(h, w, contract_last, preferred_element_type=F32)

    def wide(h, k):
        if k == 0:
            return nt_dot(h, w_ref[0:A_COLS, :])
        if k == 1:
            return jnp.concatenate(
                [nt_dot(h, wqk_ref[...]),
                 nt_dot(h, w_ref[A_COLS + 2 * R_WIDTH:A_COLS + R_COLS, :])], axis=1)
        return jnp.concatenate(
            [nt_dot(h, w_ref[A_COLS + R_COLS:D_IN_MAIN, :]), nt_dot(h, wflg_ref[...])], axis=1)

    def rows_of(c):
        return slice(c * RC_IN, (c + 1) * RC_IN)

    def a1(c, za, e):
        e["u"] = _gelu_tanh(za[:, 0:A_WIDTH])
        e["v"] = _gelu_tanh(za[:, A_WIDTH:2 * A_WIDTH])
        e["g"] = _silu(za[:, 2 * A_WIDTH:3 * A_WIDTH])
        e["mean"] = _dot(e["v"].astype(BF16), mavg)

    def a2(c, za, e):
        e["d"] = e["v"] - e["mean"]
        e["var"] = _dot((e["d"] * e["d"]).astype(BF16), mavg)

    def a3(c, za, e):
        vn = (e["d"] * lax.rsqrt(e["var"] + EPS) * aln_ref[...]).astype(BF16)
        mixed_blocks = []
        for nb in range(RC_IN // A_BLOCK):
            vb = vn[nb * A_BLOCK:(nb + 1) * A_BLOCK, :]
            vstack = jnp.concatenate(
                [jnp.where(lane_grp == gi, vb, zero_bf) for gi in range(A_GROUPS)], axis=0)
            mixed_blocks.append(_dot(wcat, vstack) + abias_ref[...])
        e["mixed"] = jnp.concatenate(mixed_blocks, axis=0)

    def a4(c, za, e):
        a_ref[rows_of(c), :] = (e["u"] * e["mixed"] * e["g"]).astype(BF16)

    def r1(c, zr, e):
        rows = rows_of(c)
        cos_t = cos_ref[rows, :]
        sin_t = sin_ref[rows, :]
        rq_ref[rows, :] = (rope(zr[:, 0:R_WIDTH], cos_t, sin_t) * (HEAD_DIM ** -0.5)).astype(BF16)
        rk_ref[rows, :] = rope(zr[:, R_WIDTH:2 * R_WIDTH], cos_t, sin_t).astype(BF16)
        rv_ref[rows, :] = zr[:, 2 * R_WIDTH:3 * R_WIDTH].astype(BF16)
        rg_ref[rows, :] = _silu(zr[:, 3 * R_WIDTH:4 * R_WIDTH]).astype(BF16)

    def f1(c, zf, e):
        ls = _log_sigmoid(zf[:, 4 * F_WIDTH:4 * F_WIDTH + LANES] + bf_ref[...])
        ls_hi = ls.astype(BF16)
        ls_lo = (ls - ls_hi.astype(F32)).astype(BF16)
        carry = carry_ref[...]
        blocks = []
        for blk in range(RC_IN // TRI_ROWS):
            rs = slice(blk * TRI_ROWS, (blk + 1) * TRI_ROWS)
            cf_blk = _dot(tri, ls_hi[rs]) + _dot(tri, ls_lo[rs]) + carry
            carry = cf_blk[TRI_ROWS - 1:TRI_ROWS, :]
            blocks.append(cf_blk)
        carry_ref[...] = carry
        e["cf2"] = jnp.concatenate(blocks, axis=0) * LOG2E

    def f2(c, zf, e):
        cf2 = e["cf2"]
        cft_ref[:, rows_of(c)] = cf2.T[:SUBLANES, :]
        negf = -cf2
        hi = negf.astype(BF16).astype(F32)
        mid = (negf - hi).astype(BF16).astype(F32)
        lo = (negf - hi - mid).astype(BF16).astype(F32)
        packed = jnp.where(
            lane < F_TERM_STRIDE, hi,
            jnp.where(lane < 2 * F_TERM_STRIDE, pltpu.roll(mid, F_TERM_STRIDE, 1),
                      jnp.where(lane < 3 * F_TERM_STRIDE, pltpu.roll(lo, 2 * F_TERM_STRIDE, 1),
                                0.0)))
        e["placed"] = _dot(packed.astype(BF16), place_ref[...])

    def f3(c, zf, e):
        rows = rows_of(c)
        fq_ref[:, rows] = (zf[:, 0:F_WIDTH] * (HEAD_DIM ** -0.5 * LOG2E)).T.astype(BF16)
        fg_ref[rows, :] = _silu(zf[:, 3 * F_WIDTH:4 * F_WIDTH]).astype(BF16)
        fv_t = zf[:, 2 * F_WIDTH:3 * F_WIDTH].T.astype(BF16)
        for hd in range(F_HEADS):
            p, hh = divmod(hd, 2)
            kk = zf[:, F_WIDTH + p * LANES:F_WIDTH + (p + 1) * LANES]
            if hh == 1:
                kk = pltpu.roll(kk, HEAD_DIM, 1)
            aug = e["placed"][:, hd * LANES:(hd + 1) * LANES]
            kaug_ref[hd, rows, :] = jnp.where(lane < HEAD_DIM, kk, aug).astype(BF16)
            vaug_ref[hd, 0:HEAD_DIM, rows] = fv_t[hd * HEAD_DIM:(hd + 1) * HEAD_DIM, :]
            vaug_ref[hd, HEAD_DIM:VAUG_ROWS, rows] = ones_rows

    nchunks = tm // RC_IN
    hs = [normed(c) for c in range(nchunks)]
    groups = ((0, (a1, a2, a3, a4)), (2, (f1, f2, f3)), (1, (r1,)))
    issued = []

    def run_slot(t):
        for s, (c, z, stages, e) in enumerate(issued):
            j = t - 1 - s
            if 0 <= j < len(stages):
                stages[j](c, z, e)

    for k, stages in groups:
        for c in range(nchunks):
            issued.append((c, wide(hs[c], k), stages, {}))
            run_slot(len(issued) - 1)
    depth = max(len(stages) for _, stages in groups)
    for t in range(len(issued), len(issued) + depth):
        run_slot(t)


def _inproj(x, pre_g, w_t, w_qk, w_flg, bf_pad, aln, aws, abias, cos_t, sin_t, tri, mavg, place):
    b, s, _ = x.shape
    tm = TM_IN
    grid = (b, s // tm)
    row_spec = lambda n: pl.BlockSpec((None, tm, n), lambda bi, si: (bi, si, 0))
    col_spec = lambda n: pl.BlockSpec((None, n, tm), lambda bi, si: (bi, 0, si))
    const2 = lambda shp: pl.BlockSpec(shp, lambda bi, si: (0, 0))
    bf_sds = lambda n: jax.ShapeDtypeStruct((b, s, n), BF16)
    out_shape = ([bf_sds(A_WIDTH)] + [bf_sds(R_WIDTH)] * 4
                 + [jax.ShapeDtypeStruct((b, F_WIDTH, s), BF16),
                    jax.ShapeDtypeStruct((b, F_HEADS, s, LANES), BF16),
                    jax.ShapeDtypeStruct((b, F_HEADS, VAUG_ROWS, s), BF16),
                    bf_sds(F_WIDTH),
                    jax.ShapeDtypeStruct((b, SUBLANES, s), F32)])
    out_specs = ([row_spec(A_WIDTH)] + [row_spec(R_WIDTH)] * 4
                 + [col_spec(F_WIDTH),
                    pl.BlockSpec((None, F_HEADS, tm, LANES), lambda bi, si: (bi, 0, si, 0)),
                    pl.BlockSpec((None, F_HEADS, VAUG_ROWS, tm), lambda bi, si: (bi, 0, 0, si)),
                    row_spec(F_WIDTH),
                    col_spec(SUBLANES)])
    in_specs = [
        row_spec(D_MODEL),
        const2((1, D_MODEL)),
        pl.BlockSpec((D_IN, D_MODEL), lambda bi, si: (0, 0), pipeline_mode=pl.Buffered(1)),
        const2((2 * R_WIDTH, D_MODEL)),
        const2((LANES, D_MODEL)),
        const2((1, LANES)),
        const2((1, A_WIDTH)),
        pl.BlockSpec((A_GROUPS, A_BLOCK, A_BLOCK), lambda bi, si: (0, 0, 0)),
        const2((A_BLOCK, A_WIDTH)),
        pl.BlockSpec((tm, LANES), lambda bi, si: (si, 0)),
        pl.BlockSpec((tm, LANES), lambda bi, si: (si, 0)),
        const2((TRI_ROWS, TRI_ROWS)),
        const2((A_WIDTH, A_WIDTH)),
        const2((LANES, F_HEADS * LANES)),
    ]
    return pl.pallas_call(
        _inproj_body,
        grid=grid,
        in_specs=in_specs,
        out_specs=out_specs,
        out_shape=out_shape,
        scratch_shapes=[pltpu.VMEM((1, LANES), F32)],
        compiler_params=pltpu.CompilerParams(
            dimension_semantics=("arbitrary", "arbitrary"), vmem_limit_bytes=VMEM_LIMIT),
        name="inproj",
    )(x, pre_g, w_t, w_qk, w_flg, bf_pad, aln, aws, abias, cos_t, sin_t, tri, mavg, place)


def _retention_body(q_ref, k_ref, v_ref, g_ref, dmat_ref, qdec_ref, kdec_ref, sdec_ref, bmask_ref,
                    mavg_ref, o_ref, state_ref):
    tm = q_ref.shape[0]

    @pl.when(pl.program_id(1) == 0)
    def _():
        state_ref[...] = jnp.zeros_like(state_ref)

    lane = lax.broadcasted_iota(jnp.int32, (1, LANES), 1)
    head0_v = lane < HEAD_DIM
    head0_qk = jnp.bitwise_and(lane, HEAD_DIM // 2) == 0
    mavg = mavg_ref[...]
    bmask = bmask_ref[...]
    contract_last = (((1,), (1,)), ((), ()))
    contract_first = (((0,), (0,)), ((), ()))
    nblk = tm // L_RET
    units = [(p, j) for p in range(HEAD_PAIRS) for j in range(nblk)]

    def split_heads(x2, head0):
        zero = jnp.zeros_like(x2)
        return jnp.concatenate([jnp.where(head0, x2, zero), jnp.where(head0, zero, x2)], axis=0)

    q2, v2, sc, kv = {}, {}, {}, {}
    for (p, j) in units:
        rs, cs = slice(j * L_RET, (j + 1) * L_RET), slice(p * LANES, (p + 1) * LANES)
        q2[p, j] = q_ref[rs, cs]
        k2 = k_ref[rs, cs]
        v2[p, j] = v_ref[rs, cs]
        sc[p, j] = lax.dot_general(q2[p, j], split_heads(k2, head0_qk), contract_last,
                                   preferred_element_type=F32)
        kd = (k2.astype(F32) * kdec_ref[p]).astype(BF16)
        kv[p, j] = lax.dot_general(kd, v2[p, j], contract_first, preferred_element_type=F32)

    st = {}
    for p in range(HEAD_PAIRS):
        state = state_ref[p]
        for j in range(nblk):
            st[p, j] = state.astype(BF16)
            state = state * sdec_ref[p] + kv[p, j] * bmask
        state_ref[p] = state

    out = {}
    for (p, j) in units:
        pm = (sc[p, j] * dmat_ref[p]).astype(BF16)
        out[p, j] = (_dot(pm, split_heads(v2[p, j], head0_v))
                     + _dot(q2[p, j], st[p, j]) * qdec_ref[p])

    tiles = [jnp.concatenate([out[p, j] for j in range(nblk)], axis=0) for p in range(HEAD_PAIRS)]
    groups = ((jnp.concatenate(tiles[0:2], axis=1), mavg, slice(0, 2 * LANES)),
              (tiles[2], mavg[:LANES, :LANES], slice(2 * LANES, 3 * LANES)))
    means = [_dot(t.astype(BF16), m) for t, m, _ in groups]
    devs = [t - mu for (t, _, _), mu in zip(groups, means)]
    variances = [_dot((d * d).astype(BF16), m) for d, (_, m, _) in zip(devs, groups)]
    for d, var, (_, _, cs) in zip(devs, variances, groups):
        y = d * lax.rsqrt(var + EPS) * g_ref[:, cs].astype(F32)
        o_ref[:, cs] = y.astype(BF16)


def _retention(rq, rk, rv, rg, dmat, qdec, kdec, sdec, bmask, mavg):
    b, s, _ = rq.shape
    tm = TM_RET
    row_spec = pl.BlockSpec((None, tm, R_WIDTH), lambda bi, si: (bi, si, 0))
    c3 = lambda shp: pl.BlockSpec(shp, lambda bi, si: (0, 0, 0))
    c2 = lambda shp: pl.BlockSpec(shp, lambda bi, si: (0, 0))
    return pl.pallas_call(
        _retention_body,
        grid=(b, s // tm),
        in_specs=[row_spec, row_spec, row_spec, row_spec,
                  c3((HEAD_PAIRS, L_RET, 2 * L_RET)), c3((HEAD_PAIRS, L_RET, LANES)),
                  c3((HEAD_PAIRS, L_RET, LANES)), c3((HEAD_PAIRS, LANES, LANES)),
                  c2((LANES, LANES)), c2((2 * LANES, 2 * LANES))],
        out_specs=row_spec,
        out_shape=jax.ShapeDtypeStruct((b, s, R_WIDTH), BF16),
        scratch_shapes=[pltpu.VMEM((HEAD_PAIRS, LANES, LANES), F32)],
        compiler_params=pltpu.CompilerParams(
            dimension_semantics=("arbitrary", "arbitrary"), vmem_limit_bytes=VMEM_LIMIT),
        name="retention",
    )(rq, rk, rv, rg, dmat, qdec, kdec, sdec, bmask, mavg)


def _fox_body(qt_ref, qtn_ref, kaug_ref, vaug_ref, cft_ref, g_ref, o_ref, m_ref, acc_ref, sa_ref,
              sb_ref):
    qi = pl.program_id(1)
    srow_q = lax.broadcasted_iota(jnp.int32, (HEAD_DIM, 1), 0)
    q_ones = jnp.broadcast_to(jnp.where(srow_q < N_F_TERMS, 1.0, 0.0),
                              (HEAD_DIM, TQ)).astype(BF16)
    qta = [jnp.concatenate([qt_ref[h * HEAD_DIM:(h + 1) * HEAD_DIM, :], q_ones], axis=0)
           for h in range(F_HEADS)]
    fq = [cft_ref[h:h + 1, :] for h in range(F_HEADS)]
    krow = lax.broadcasted_iota(jnp.int32, (TK, TQ), 0)
    qcol = lax.broadcasted_iota(jnp.int32, (TK, TQ), 1)
    causal = (krow <= qcol, krow + TK <= qcol)

    for h in range(F_HEADS):
        m_ref[h] = jnp.full((SUBLANES, TQ), MASK_VALUE, F32)
        acc_ref[h] = jnp.zeros((VAUG_ROWS, TQ), F32)

    def scores(kj, dst_ref, h, last=False):
        k0 = pl.multiple_of(kj * TK, TK)
        kt = kaug_ref[h, pl.ds(k0, TK), :]
        if last:
            dst_ref[h, :, TK:TQ] = _dot(kt, qta[h][:, TK:TQ])
        else:
            dst_ref[h] = _dot(kt, qta[h])

    def consume(kj, src_ref, h, diagonal):
        k0 = pl.multiple_of(kj * TK, TK)
        st = src_ref[h]
        if diagonal:
            st = jnp.where(causal[0], st, MASK_VALUE)
        m_old = m_ref[h][0:1, :]
        m_new = jnp.maximum(m_old, jnp.max(st, axis=0, keepdims=True) + fq[h])
        p = jnp.exp2(st - (m_new - fq[h]))
        alpha = jnp.exp2(m_old - m_new)
        pv = _dot(vaug_ref[h, :, pl.ds(k0, TK)], p.astype(BF16))
        acc_ref[h] = alpha * acc_ref[h] + pv
        m_ref[h] = jnp.broadcast_to(m_new, (SUBLANES, TQ))

    def consume_last(kj, src_ref, h):
        k0 = pl.multiple_of(kj * TK, TK)
        st = jnp.where(causal[1][:, TK:TQ], src_ref[h, :, TK:TQ], MASK_VALUE)
        m_old = m_ref[h][0:1, TK:TQ]
        fq_h = fq[h][:, TK:TQ]
        m_new = jnp.maximum(m_old, jnp.max(st, axis=0, keepdims=True) + fq_h)
        p = jnp.exp2(st - (m_new - fq_h))
        alpha = jnp.exp2(m_old - m_new)
        pv = _dot(vaug_ref[h, :, pl.ds(k0, TK)], p.astype(BF16))
        acc_ref[h, :, TK:TQ] = alpha * acc_ref[h, :, TK:TQ] + pv

    def overlapped(next_j, dst_ref, cur_j, src_ref, diagonal=False, next_last=False):
        lead = 2
        for h in range(lead):
            scores(next_j, dst_ref, h, next_last)
        for h in range(F_HEADS):
            consume(cur_j, src_ref, h, diagonal)
            if h + lead < F_HEADS:
                scores(next_j, dst_ref, h + lead, next_last)

    @pl.when(qi == 0)
    def _():
        for h in range(F_HEADS):
            scores(0, sa_ref, h)

    def two_tiles(j0):
        overlapped(j0 + 1, sb_ref, j0, sa_ref)
        overlapped(j0 + 2, sa_ref, j0 + 1, sb_ref)

    def body(i, carry):
        two_tiles(4 * i)
        two_tiles(4 * i + 2)
        return carry

    lax.fori_loop(0, lax.shift_right_logical(qi, 1), body, 0)

    @pl.when(jnp.bitwise_and(qi, 1) == 1)
    def _():
        two_tiles(2 * qi - 2)
    overlapped(2 * qi + 1, sb_ref, 2 * qi, sa_ref, diagonal=True, next_last=True)
    for h in range(F_HEADS):
        consume_last(2 * qi + 1, sb_ref, h)
        qta_next = jnp.concatenate([qtn_ref[h * HEAD_DIM:(h + 1) * HEAD_DIM, :], q_ones], axis=0)
        sa_ref[h] = _dot(kaug_ref[h, 0:TK, :], qta_next)

    for p in range(HEAD_PAIRS):
        halves = []
        for hh in range(2):
            acc = acc_ref[2 * p + hh]
            halves.append(acc[0:HEAD_DIM, :] / acc[HEAD_DIM:HEAD_DIM + 1, :])
        out_t = jnp.concatenate(halves, axis=0)
        cs = slice(p * LANES, (p + 1) * LANES)
        o_ref[:, cs] = (out_t.T * g_ref[:, cs].astype(F32)).astype(BF16)


def _fox(fqt, kaug, vaug, fg, cf_t):
    b, _, s = fqt.shape
    assert TQ == 2 * TK
    row_spec = pl.BlockSpec((None, TQ, F_WIDTH), lambda bi, qi: (bi, qi, 0))
    return pl.pallas_call(
        _fox_body,
        grid=(b, s // TQ),
        in_specs=[pl.BlockSpec((None, F_WIDTH, TQ), lambda bi, qi: (bi, 0, qi)),
                  pl.BlockSpec((None, F_WIDTH, TQ),
                               lambda bi, qi: (bi, 0, jnp.minimum(qi + 1, s // TQ - 1))),
                  pl.BlockSpec((None, F_HEADS, s, LANES), lambda bi, qi: (bi, 0, 0, 0)),
                  pl.BlockSpec((None, F_HEADS, VAUG_ROWS, s), lambda bi, qi: (bi, 0, 0, 0)),
                  pl.BlockSpec((None, SUBLANES, TQ), lambda bi, qi: (bi, 0, qi)),
                  row_spec],
        out_specs=row_spec,
        out_shape=jax.ShapeDtypeStruct((b, s, F_WIDTH), BF16),
        scratch_shapes=[pltpu.VMEM((F_HEADS, SUBLANES, TQ), F32),
                        pltpu.VMEM((F_HEADS, VAUG_ROWS, TQ), F32),
                        pltpu.VMEM((F_HEADS, TK, TQ), F32),
                        pltpu.VMEM((F_HEADS, TK, TQ), F32)],
        compiler_params=pltpu.CompilerParams(
            dimension_semantics=("arbitrary", "arbitrary"), vmem_limit_bytes=VMEM_LIMIT),
        name="fox",
    )(fqt, fqt, kaug, vaug, cf_t, fg)


def _outproj_body(a_ref, r_ref, f_ref, w_ref, x_ref, pg_ref, o_ref):
    tm = x_ref.shape[0]

    def project(c):
        rows = slice(c * RC_OUT, (c + 1) * RC_OUT)
        y = jnp.concatenate([a_ref[rows, :], r_ref[rows, :], f_ref[rows, :]], axis=1)
        return _dot(y, w_ref[...])

    nchunks = tm // RC_OUT
    o_next = project(0)
    for c in range(nchunks):
        rows = slice(c * RC_OUT, (c + 1) * RC_OUT)
        o = o_next
        if c + 1 < nchunks:
            o_next = project(c + 1)
        ms = jnp.mean(o * o, axis=-1, keepdims=True)
        o_ref[rows, :] = x_ref[rows, :] + o * lax.rsqrt(ms + EPS) * pg_ref[...]


def _outproj(a, r, f, w, x, post_g):
    b, s, _ = x.shape
    tm = TM_OUT
    row_spec = lambda n: pl.BlockSpec((None, tm, n), lambda bi, si: (bi, si, 0))
    c2 = lambda shp: pl.BlockSpec(shp, lambda bi, si: (0, 0))
    return pl.pallas_call(
        _outproj_body,
        grid=(b, s // tm),
        in_specs=[row_spec(A_WIDTH), row_spec(R_WIDTH), row_spec(F_WIDTH),
                  c2((D_MODEL, D_MODEL)), row_spec(D_MODEL), c2((1, D_MODEL))],
        out_specs=row_spec(D_MODEL),
        out_shape=jax.ShapeDtypeStruct((b, s, D_MODEL), F32),
        compiler_params=pltpu.CompilerParams(
            dimension_semantics=("arbitrary", "arbitrary"), vmem_limit_bytes=VMEM_LIMIT),
        name="outproj",
    )(a, r, f, w, x, post_g)


@functools.lru_cache(maxsize=None)
def _tables(seq):
    half = HEAD_DIM // 2
    lane = np.arange(LANES)
    inv = ROPE_THETA ** (-(np.arange(half, dtype=np.float64) / half))
    ang = np.arange(seq, dtype=np.float64)[:, None] * inv[None, :]
    cos = np.cos(ang)
    sin = np.sin(ang)
    cos_t = cos[:, lane % half]
    sin_t = sin[:, lane % half] * np.where(lane < HEAD_DIM, -1.0, 1.0)[None, :]

    gam = 1.0 - np.exp2(-5.0 - np.arange(R_HEADS, dtype=np.float64))
    log_gam = np.log(gam)
    pos = np.arange(L_RET)
    dist = np.abs(pos[:, None] - pos[None, :])
    allowed = (pos[None, :] // CHUNK) <= (pos[:, None] // CHUNK)
    dmat = np.where(allowed[None], np.exp(log_gam[:, None, None] * dist[None]), 0.0)
    dmat = np.concatenate([dmat[0::2], dmat[1::2]], axis=2)
    head_of_lane = lane // HEAD_DIM
    head_of_qk_lane = (lane // half) % 2
    qdec = np.stack([np.exp(log_gam[2 * p + head_of_lane][None, :] * (pos + 1.0)[:, None])
                     for p in range(HEAD_PAIRS)])
    kdec = np.stack([np.exp(log_gam[2 * p + head_of_qk_lane][None, :] * (L_RET - 1.0 - pos)[:, None])
                     for p in range(HEAD_PAIRS)])
    bmask = (head_of_qk_lane[:, None] == head_of_lane[None, :]).astype(np.float32)
    sdec = np.stack([np.exp(log_gam[2 * p + head_of_qk_lane] * L_RET)[:, None] * bmask
                     for p in range(HEAD_PAIRS)])
    tri = (np.arange(TRI_ROWS)[None, :] <= np.arange(TRI_ROWS)[:, None]).astype(np.float32)
    grp = np.arange(A_WIDTH) // HEAD_DIM
    mavg = (grp[:, None] == grp[None, :]).astype(np.float32) / HEAD_DIM
    place = np.zeros((LANES, F_HEADS * LANES), np.float32)
    for h in range(F_HEADS):
        for term in range(N_F_TERMS):
            place[F_TERM_STRIDE * term + h, h * LANES + HEAD_DIM + term] = 1.0
    f = lambda a: np.asarray(a, np.float32)
    return dict(cos=f(cos_t), sin=f(sin_t), dmat=f(dmat), qdec=f(qdec), kdec=f(kdec),
                sdec=f(sdec), bmask=f(bmask), tri=f(tri), mavg=f(mavg),
                place=f(place))


def _layer(x, pre_g, post_g, w_in, b_f, a_ln_g, a_ws, a_bs, w_out, t):
    half = HEAD_DIM // 2
    w_t = w_in.T.astype(BF16)
    qk = w_t[A_COLS:A_COLS + 2 * R_WIDTH].reshape(2 * HEAD_PAIRS, 2, 2, half, D_MODEL)
    qk = jnp.swapaxes(qk, 1, 2).reshape(2 * R_WIDTH, D_MODEL)
    w_flg = jnp.pad(w_t[D_IN_MAIN:], ((0, LANES - F_HEADS), (0, 0)))
    bf_pad = jnp.pad(b_f, (0, LANES - F_HEADS)).reshape(1, LANES)
    abias = jnp.repeat(a_bs.T, HEAD_DIM, axis=1)
    (a_out, rq, rk, rv, rg, fqt, kaug, vaug, fg, cf_t) = _inproj(
        x, pre_g.reshape(1, D_MODEL), w_t, qk, w_flg, bf_pad, a_ln_g.reshape(1, A_WIDTH), a_ws,
        abias, t["cos"], t["sin"], t["tri"], t["mavg"], t["place"])
    r_out = _retention(rq, rk, rv, rg, t["dmat"], t["qdec"], t["kdec"], t["sdec"], t["bmask"],
                       t["mavg"])
    f_out = _fox(fqt, kaug, vaug, fg, cf_t)
    return _outproj(a_out, r_out, f_out, w_out.astype(BF16), x, post_g.reshape(1, D_MODEL))


def kernel(x, pre_gain, post_gain, w_in, b_forget, a_norm_gain, a_spatial_w, a_spatial_b, w_out):
    _, s, d = x.shape
    assert d == D_MODEL and all(s % t == 0 for t in (TM_IN, TM_RET, TM_OUT, TQ)), x.shape
    assert w_in.shape[1:] == (D_MODEL, D_IN) and w_out.shape[1:] == (D_MODEL, D_MODEL)
    bf16_tables = ("tri", "mavg", "place")
    t = {k: jnp.asarray(v, BF16 if k in bf16_tables else F32) for k, v in _tables(x.shape[1]).items()}
    for l in range(pre_gain.shape[0]):
        x = _layer(x, pre_gain[l], post_gain[l], w_in[l], b_forget[l], a_norm_gain[l],
                   a_spatial_w[l], a_spatial_b[l], w_out[l], t)
    return x
```

```python
import functools
import math

import jax
import jax.numpy as jnp
import numpy as np
from jax import lax
from jax.experimental import pallas as pl
from jax.experimental.pallas import tpu as pltpu

F32 = jnp.float32
BF16 = jnp.bfloat16

D_MODEL = 1024
HEAD_DIM = 64
CHUNK = 64
A_WIDTH = 256
R_WIDTH = 384
F_WIDTH = 384
A_GROUPS = 4
R_HEADS = 6
F_HEADS = 6
A_BLOCK = 128
ROPE_THETA = 10000.0
EPS = 1e-6
LANES = 128
SUBLANES = 8
BF16_SUBLANES = 16
HEAD_PAIRS = R_WIDTH // LANES
N_F_TERMS = 3
F_TERM_STRIDE = 8

A_COLS = 3 * A_WIDTH
R_COLS = 4 * R_WIDTH
F_COLS = 4 * F_WIDTH + F_HEADS
D_IN = A_COLS + R_COLS + F_COLS
D_IN_MAIN = A_COLS + R_COLS + 4 * F_WIDTH

TM_IN = 1024
RC_IN = 512
TRI_ROWS = 256
TM_RET = 2048
L_RET = 128
TQ = 512
TK = 256
TM_OUT = 2048
RC_OUT = 512
VMEM_LIMIT = 56 * 1024 * 1024
VAUG_ROWS = HEAD_DIM + BF16_SUBLANES
LOG2E = 1.4426950408889634
MASK_VALUE = -1e30


def _silu(x):
    return 0.5 * x * (1.0 + jnp.tanh(0.5 * x))


def _gelu_tanh(x):
    c = math.sqrt(2.0 / math.pi)
    return 0.5 * x * (1.0 + jnp.tanh(c * (x + 0.044715 * (x * x * x))))


def _log_sigmoid(x):
    return jnp.minimum(x, 0.0) - jnp.log1p(jnp.exp(-jnp.abs(x)))


def _dot(a, b):
    return jnp.dot(a, b, preferred_element_type=F32)


def _group_norm64(x, mavg):
    mean = _dot(x.astype(BF16), mavg)
    d = x - mean
    var = _dot((d * d).astype(BF16), mavg)
    return d * lax.rsqrt(var + EPS)


def _inproj_body(x_ref, pg_ref, w_ref, wqk_ref, wflg_ref, bf_ref, aln_ref, aws_ref, abias_ref,
                 cos_ref, sin_ref,
                 tri_ref, mavg_ref, place_ref,
                 a_ref, rq_ref, rk_ref, rv_ref, rg_ref, fq_ref, kaug_ref, vaug_ref, fg_ref, cft_ref,
                 carry_ref):
    tm = x_ref.shape[0]

    @pl.when(pl.program_id(1) == 0)
    def _():
        carry_ref[...] = jnp.zeros_like(carry_ref)

    mavg = mavg_ref[...]
    tri = tri_ref[...]
    row = lax.broadcasted_iota(jnp.int32, (A_BLOCK, A_BLOCK), 0)
    col = lax.broadcasted_iota(jnp.int32, (A_BLOCK, A_BLOCK), 1)
    allowed = jnp.logical_not(jnp.logical_and(row < CHUNK, col >= CHUNK))
    wcat = jnp.concatenate(
        [jnp.where(allowed, aws_ref[gi], 0.0).astype(BF16) for gi in range(A_GROUPS)], axis=1)
    lane_grp = lax.shift_right_logical(
        lax.broadcasted_iota(jnp.int32, (A_BLOCK, A_WIDTH), 1), HEAD_DIM.bit_length() - 1)
    zero_bf = jnp.zeros((A_BLOCK, A_WIDTH), BF16)
    lane = lax.broadcasted_iota(jnp.int32, (1, LANES), 1)
    srow = lax.broadcasted_iota(jnp.int32, (VAUG_ROWS - HEAD_DIM, 1), 0)
    ones_rows = jnp.broadcast_to(jnp.where(srow == 0, 1.0, 0.0),
                                 (VAUG_ROWS - HEAD_DIM, RC_IN)).astype(BF16)

    def rope(z, cos_t, sin_t):
        outs = []
        for c in range(HEAD_PAIRS):
            zc = z[:, c * LANES:(c + 1) * LANES]
            outs.append(zc * cos_t + pltpu.roll(zc, HEAD_DIM, 1) * sin_t)
        return jnp.concatenate(outs, axis=1)

    def normed(c):
        x = x_ref[c * RC_IN:(c + 1) * RC_IN, :]
        ms = jnp.mean(x * x, axis=-1, keepdims=True)
        return (x * lax.rsqrt(ms + EPS) * pg_ref[...]).astype(BF16)

    contract_last = (((1,), (1,)), ((), ()))

    def nt_dot(h, w):
        return lax.dot_general(h, w, contract_last, preferred_element_type=F32)

    def wide(h, k):
        if k == 0:
            return nt_dot(h, w_ref[0:A_COLS, :])
        if k == 1:
            return jnp.concatenate(
                [nt_dot(h, wqk_ref[...]),
                 nt_dot(h, w_ref[A_COLS + 2 * R_WIDTH:A_COLS + R_COLS, :])], axis=1)
        return jnp.concatenate(
            [nt_dot(h, w_ref[A_COLS + R_COLS:D_IN_MAIN, :]), nt_dot(h, wflg_ref[...])], axis=1)

    def rows_of(c):
        return slice(c * RC_IN, (c + 1) * RC_IN)

    def a1(c, za, e):
        e["u"] = _gelu_tanh(za[:, 0:A_WIDTH])
        e["v"] = _gelu_tanh(za[:, A_WIDTH:2 * A_WIDTH])
        e["g"] = _silu(za[:, 2 * A_WIDTH:3 * A_WIDTH])
        e["mean"] = _dot(e["v"].astype(BF16), mavg)

    def a2(c, za, e):
        e["d"] = e["v"] - e["mean"]
        e["var"] = _dot((e["d"] * e["d"]).astype(BF16), mavg)

    def a3(c, za, e):
        vn = (e["d"] * lax.rsqrt(e["var"] + EPS) * aln_ref[...]).astype(BF16)
        mixed_blocks = []
        for nb in range(RC_IN // A_BLOCK):
            vb = vn[nb * A_BLOCK:(nb + 1) * A_BLOCK, :]
            vstack = jnp.concatenate(
                [jnp.where(lane_grp == gi, vb, zero_bf) for gi in range(A_GROUPS)], axis=0)
            mixed_blocks.append(_dot(wcat, vstack) + abias_ref[...])
        e["mixed"] = jnp.concatenate(mixed_blocks, axis=0)

    def a4(c, za, e):
        a_ref[rows_of(c), :] = (e["u"] * e["mixed"] * e["g"]).astype(BF16)

    def r1(c, zr, e):
        rows = rows_of(c)
        cos_t = cos_ref[rows, :]
        sin_t = sin_ref[rows, :]
        rq_ref[rows, :] = (rope(zr[:, 0:R_WIDTH], cos_t, sin_t) * (HEAD_DIM ** -0.5)).astype(BF16)
        rk_ref[rows, :] = rope(zr[:, R_WIDTH:2 * R_WIDTH], cos_t, sin_t).astype(BF16)
        rv_ref[rows, :] = zr[:, 2 * R_WIDTH:3 * R_WIDTH].astype(BF16)
        rg_ref[rows, :] = _silu(zr[:, 3 * R_WIDTH:4 * R_WIDTH]).astype(BF16)

    def f1(c, zf, e):
        ls = _log_sigmoid(zf[:, 4 * F_WIDTH:4 * F_WIDTH + LANES] + bf_ref[...])
        ls_hi = ls.astype(BF16)
        ls_lo = (ls - ls_hi.astype(F32)).astype(BF16)
        carry = carry_ref[...]
        blocks = []
        for blk in range(RC_IN // TRI_ROWS):
            rs = slice(blk * TRI_ROWS, (blk + 1) * TRI_ROWS)
            cf_blk = _dot(tri, ls_hi[rs]) + _dot(tri, ls_lo[rs]) + carry
            carry = cf_blk[TRI_ROWS - 1:TRI_ROWS, :]
            blocks.append(cf_blk)
        carry_ref[...] = carry
        e["cf2"] = jnp.concatenate(blocks, axis=0) * LOG2E

    def f2(c, zf, e):
        cf2 = e["cf2"]
        cft_ref[:, rows_of(c)] = cf2.T[:SUBLANES, :]
        negf = -cf2
        hi = negf.astype(BF16).astype(F32)
        mid = (negf - hi).astype(BF16).astype(F32)
        lo = (negf - hi - mid).astype(BF16).astype(F32)
        packed = jnp.where(
            lane < F_TERM_STRIDE, hi,
            jnp.where(lane < 2 * F_TERM_STRIDE, pltpu.roll(mid, F_TERM_STRIDE, 1),
                      jnp.where(lane < 3 * F_TERM_STRIDE, pltpu.roll(lo, 2 * F_TERM_STRIDE, 1),
                                0.0)))
        e["placed"] = _dot(packed.astype(BF16), place_ref[...])

    def f3(c, zf, e):
        rows = rows_of(c)
        fq_ref[:, rows] = (zf[:, 0:F_WIDTH] * (HEAD_DIM ** -0.5 * LOG2E)).T.astype(BF16)
        fg_ref[rows, :] = _silu(zf[:, 3 * F_WIDTH:4 * F_WIDTH]).astype(BF16)
        fv_t = zf[:, 2 * F_WIDTH:3 * F_WIDTH].T.astype(BF16)
        for hd in range(F_HEADS):
            p, hh = divmod(hd, 2)
            kk = zf[:, F_WIDTH + p * LANES:F_WIDTH + (p + 1) * LANES]
            if hh == 1:
                kk = pltpu.roll(kk, HEAD_DIM, 1)
            aug = e["placed"][:, hd * LANES:(hd + 1) * LANES]
            kaug_ref[hd, rows, :] = jnp.where(lane < HEAD_DIM, kk, aug).astype(BF16)
            vaug_ref[hd, 0:HEAD_DIM, rows] = fv_t[hd * HEAD_DIM:(hd + 1) * HEAD_DIM, :]
            vaug_ref[hd, HEAD_DIM:VAUG_ROWS, rows] = ones_rows

    nchunks = tm // RC_IN
    hs = [normed(c) for c in range(nchunks)]
    groups = ((0, (a1, a2, a3, a4)), (2, (f1, f2, f3)), (1, (r1,)))
    issued = []

    def run_slot(t):
        for s, (c, z, stages, e) in enumerate(issued):
            j = t - 1 - s
            if 0 <= j < len(stages):
                stages[j](c, z, e)

    for k, stages in groups:
        for c in range(nchunks):
            issued.append((c, wide(hs[c], k), stages, {}))
            run_slot(len(issued) - 1)
    depth = max(len(stages) for _, stages in groups)
    for t in range(len(issued), len(issued) + depth):
        run_slot(t)


def _inproj(x, pre_g, w_t, w_qk, w_flg, bf_pad, aln, aws, abias, cos_t, sin_t, tri, mavg, place):
    b, s, _ = x.shape
    tm = TM_IN
    grid = (b, s // tm)
    row_spec = lambda n: pl.BlockSpec((None, tm, n), lambda bi, si: (bi, si, 0))
    col_spec = lambda n: pl.BlockSpec((None, n, tm), lambda bi, si: (bi, 0, si))
    const2 = lambda shp: pl.BlockSpec(shp, lambda bi, si: (0, 0))
    bf_sds = lambda n: jax.ShapeDtypeStruct((b, s, n), BF16)
    out_shape = ([bf_sds(A_WIDTH)] + [bf_sds(R_WIDTH)] * 4
                 + [jax.ShapeDtypeStruct((b, F_WIDTH, s), BF16),
                    jax.ShapeDtypeStruct((b, F_HEADS, s, LANES), BF16),
                    jax.ShapeDtypeStruct((b, F_HEADS, VAUG_ROWS, s), BF16),
                    bf_sds(F_WIDTH),
                    jax.ShapeDtypeStruct((b, SUBLANES, s), F32)])
    out_specs = ([row_spec(A_WIDTH)] + [row_spec(R_WIDTH)] * 4
                 + [col_spec(F_WIDTH),
                    pl.BlockSpec((None, F_HEADS, tm, LANES), lambda bi, si: (bi, 0, si, 0)),
                    pl.BlockSpec((None, F_HEADS, VAUG_ROWS, tm), lambda bi, si: (bi, 0, 0, si)),
                    row_spec(F_WIDTH),
                    col_spec(SUBLANES)])
    in_specs = [
        row_spec(D_MODEL),
        const2((1, D_MODEL)),
        pl.BlockSpec((D_IN, D_MODEL), lambda bi, si: (0, 0), pipeline_mode=pl.Buffered(1)),
        const2((2 * R_WIDTH, D_MODEL)),
        const2((LANES, D_MODEL)),
        const2((1, LANES)),
        const2((1, A_WIDTH)),
        pl.BlockSpec((A_GROUPS, A_BLOCK, A_BLOCK), lambda bi, si: (0, 0, 0)),
        const2((A_BLOCK, A_WIDTH)),
        pl.BlockSpec((tm, LANES), lambda bi, si: (si, 0)),
        pl.BlockSpec((tm, LANES), lambda bi, si: (si, 0)),
        const2((TRI_ROWS, TRI_ROWS)),
        const2((A_WIDTH, A_WIDTH)),
        const2((LANES, F_HEADS * LANES)),
    ]
    return pl.pallas_call(
        _inproj_body,
        grid=grid,
        in_specs=in_specs,
        out_specs=out_specs,
        out_shape=out_shape,
        scratch_shapes=[pltpu.VMEM((1, LANES), F32)],
        compiler_params=pltpu.CompilerParams(
            dimension_semantics=("arbitrary", "arbitrary"), vmem_limit_bytes=VMEM_LIMIT),
        name="inproj",
    )(x, pre_g, w_t, w_qk, w_flg, bf_pad, aln, aws, abias, cos_t, sin_t, tri, mavg, place)


def _retention_body(q_ref, k_ref, v_ref, g_ref, dmat_ref, qdec_ref, kdec_ref, sdec_ref, bmask_ref,
                    mavg_ref, o_ref, state_ref):
    tm = q_ref.shape[0]

    @pl.when(pl.program_id(1) == 0)
    def _():
        state_ref[...] = jnp.zeros_like(state_ref)

    lane = lax.broadcasted_iota(jnp.int32, (1, LANES), 1)
    head0_v = lane < HEAD_DIM
    head0_qk = jnp.bitwise_and(lane, HEAD_DIM // 2) == 0
    mavg = mavg_ref[...]
    bmask = bmask_ref[...]
    contract_last = (((1,), (1,)), ((), ()))
    contract_first = (((0,), (0,)), ((), ()))
    nblk = tm // L_RET
    units = [(p, j) for p in range(HEAD_PAIRS) for j in range(nblk)]

    def split_heads(x2, head0):
        zero = jnp.zeros_like(x2)
        return jnp.concatenate([jnp.where(head0, x2, zero), jnp.where(head0, zero, x2)], axis=0)

    q2, v2, sc, kv = {}, {}, {}, {}
    for (p, j) in units:
        rs, cs = slice(j * L_RET, (j + 1) * L_RET), slice(p * LANES, (p + 1) * LANES)
        q2[p, j] = q_ref[rs, cs]
        k2 = k_ref[rs, cs]
        v2[p, j] = v_ref[rs, cs]
        sc[p, j] = lax.dot_general(q2[p, j], split_heads(k2, head0_qk), contract_last,
                                   preferred_element_type=F32)
        kd = (k2.astype(F32) * kdec_ref[p]).astype(BF16)
        kv[p, j] = lax.dot_general(kd, v2[p, j], contract_first, preferred_element_type=F32)

    st = {}
    for p in range(HEAD_PAIRS):
        state = state_ref[p]
        for j in range(nblk):
            st[p, j] = state.astype(BF16)
            state = state * sdec_ref[p] + kv[p, j] * bmask
        state_ref[p] = state

    out = {}
    for (p, j) in units:
        pm = (sc[p, j] * dmat_ref[p]).astype(BF16)
        out[p, j] = (_dot(pm, split_heads(v2[p, j], head0_v))
                     + _dot(q2[p, j], st[p, j]) * qdec_ref[p])

    tiles = [jnp.concatenate([out[p, j] for j in range(nblk)], axis=0) for p in range(HEAD_PAIRS)]
    groups = ((jnp.concatenate(tiles[0:2], axis=1), mavg, slice(0, 2 * LANES)),
              (tiles[2], mavg[:LANES, :LANES], slice(2 * LANES, 3 * LANES)))
    means = [_dot(t.astype(BF16), m) for t, m, _ in groups]
    devs = [t - mu for (t, _, _), mu in zip(groups, means)]
    variances = [_dot((d * d).astype(BF16), m) for d, (_, m, _) in zip(devs, groups)]
    for d, var, (_, _, cs) in zip(devs, variances, groups):
        y = d * lax.rsqrt(var + EPS) * g_ref[:, cs].astype(F32)
        o_ref[:, cs] = y.astype(BF16)


def _retention(rq, rk, rv, rg, dmat, qdec, kdec, sdec, bmask, mavg):
    b, s, _ = rq.shape
    tm = TM_RET
    row_spec = pl.BlockSpec((None, tm, R_WIDTH), lambda bi, si: (bi, si, 0))
    c3 = lambda shp: pl.BlockSpec(shp, lambda bi, si: (0, 0, 0))
    c2 = lambda shp: pl.BlockSpec(shp, lambda bi, si: (0, 0))
    return pl.pallas_call(
        _retention_body,
        grid=(b, s // tm),
        in_specs=[row_spec, row_spec, row_spec, row_spec,
                  c3((HEAD_PAIRS, L_RET, 2 * L_RET)), c3((HEAD_PAIRS, L_RET, LANES)),
                  c3((HEAD_PAIRS, L_RET, LANES)), c3((HEAD_PAIRS, LANES, LANES)),
                  c2((LANES, LANES)), c2((2 * LANES, 2 * LANES))],
        out_specs=row_spec,
        out_shape=jax.ShapeDtypeStruct((b, s, R_WIDTH), BF16),
        scratch_shapes=[pltpu.VMEM((HEAD_PAIRS, LANES, LANES), F32)],
        compiler_params=pltpu.CompilerParams(
            dimension_semantics=("arbitrary", "arbitrary"), vmem_limit_bytes=VMEM_LIMIT),
        name="retention",
    )(rq, rk, rv, rg, dmat, qdec, kdec, sdec, bmask, mavg)


def _fox_body(qt_ref, qtn_ref, kaug_ref, vaug_ref, cft_ref, g_ref, o_ref, m_ref, acc_ref, sa_ref,
              sb_ref):
    qi = pl.program_id(1)
    srow_q = lax.broadcasted_iota(jnp.int32, (HEAD_DIM, 1), 0)
    q_ones = jnp.broadcast_to(jnp.where(srow_q < N_F_TERMS, 1.0, 0.0),
                              (HEAD_DIM, TQ)).astype(BF16)
    qta = [jnp.concatenate([qt_ref[h * HEAD_DIM:(h + 1) * HEAD_DIM, :], q_ones], axis=0)
           for h in range(F_HEADS)]
    fq = [cft_ref[h:h + 1, :] for h in range(F_HEADS)]
    krow = lax.broadcasted_iota(jnp.int32, (TK, TQ), 0)
    qcol = lax.broadcasted_iota(jnp.int32, (TK, TQ), 1)
    causal = (krow <= qcol, krow + TK <= qcol)

    for h in range(F_HEADS):
        m_ref[h] = jnp.full((SUBLANES, TQ), MASK_VALUE, F32)
        acc_ref[h] = jnp.zeros((VAUG_ROWS, TQ), F32)

    def scores(kj, dst_ref, h, last=False):
        k0 = pl.multiple_of(kj * TK, TK)
        kt = kaug_ref[h, pl.ds(k0, TK), :]
        if last:
            dst_ref[h, :, TK:TQ] = _dot(kt, qta[h][:, TK:TQ])
        else:
            dst_ref[h] = _dot(kt, qta[h])

    def consume(kj, src_ref, h, diagonal):
        k0 = pl.multiple_of(kj * TK, TK)
        st = src_ref[h]
        if diagonal:
            st = jnp.where(causal[0], st, MASK_VALUE)
        m_old = m_ref[h][0:1, :]
        m_new = jnp.maximum(m_old, jnp.max(st, axis=0, keepdims=True) + fq[h])
        p = jnp.exp2(st - (m_new - fq[h]))
        alpha = jnp.exp2(m_old - m_new)
        pv = _dot(vaug_ref[h, :, pl.ds(k0, TK)], p.astype(BF16))
        acc_ref[h] = alpha * acc_ref[h] + pv
        m_ref[h] = jnp.broadcast_to(m_new, (SUBLANES, TQ))

    def consume_last(kj, src_ref, h):
        k0 = pl.multiple_of(kj * TK, TK)
        st = jnp.where(causal[1][:, TK:TQ], src_ref[h, :, TK:TQ], MASK_VALUE)
        m_old = m_ref[h][0:1, TK:TQ]
        fq_h = fq[h][:, TK:TQ]
        m_new = jnp.maximum(m_old, jnp.max(st, axis=0, keepdims=True) + fq_h)
        p = jnp.exp2(st - (m_new - fq_h))
        alpha = jnp.exp2(m_old - m_new)
        pv = _dot(vaug_ref[h, :, pl.ds(k0, TK)], p.astype(BF16))
        acc_ref[h, :, TK:TQ] = alpha * acc_ref[h, :, TK:TQ] + pv

    def overlapped(next_j, dst_ref, cur_j, src_ref, diagonal=False, next_last=False):
        lead = 2
        for h in range(lead):
            scores(next_j, dst_ref, h, next_last)
        for h in range(F_HEADS):
            consume(cur_j, src_ref, h, diagonal)
            if h + lead < F_HEADS:
                scores(next_j, dst_ref, h + lead, next_last)

    @pl.when(qi == 0)
    def _():
        for h in range(F_HEADS):
            scores(0, sa_ref, h)

    def two_tiles(j0):
        overlapped(j0 + 1, sb_ref, j0, sa_ref)
        overlapped(j0 + 2, sa_ref, j0 + 1, sb_ref)

    def body(i, carry):
        two_tiles(4 * i)
        two_tiles(4 * i + 2)
        return carry

    lax.fori_loop(0, lax.shift_right_logical(qi, 1), body, 0)

    @pl.when(jnp.bitwise_and(qi, 1) == 1)
    def _():
        two_tiles(2 * qi - 2)
    overlapped(2 * qi + 1, sb_ref, 2 * qi, sa_ref, diagonal=True, next_last=True)
    for h in range(F_HEADS):
        consume_last(2 * qi + 1, sb_ref, h)
        qta_next = jnp.concatenate([qtn_ref[h * HEAD_DIM:(h + 1) * HEAD_DIM, :], q_ones], axis=0)
        sa_ref[h] = _dot(kaug_ref[h, 0:TK, :], qta_next)

    for p in range(HEAD_PAIRS):
        halves = []
        for hh in range(2):
            acc = acc_ref[2 * p + hh]
            halves.append(acc[0:HEAD_DIM, :] / acc[HEAD_DIM:HEAD_DIM + 1, :])
        out_t = jnp.concatenate(halves, axis=0)
        cs = slice(p * LANES, (p + 1) * LANES)
        o_ref[:, cs] = (out_t.T * g_ref[:, cs].astype(F32)).astype(BF16)


def _fox(fqt, kaug, vaug, fg, cf_t):
    b, _, s = fqt.shape
    assert TQ == 2 * TK
    row_spec = pl.BlockSpec((None, TQ, F_WIDTH), lambda bi, qi: (bi, qi, 0))
    return pl.pallas_call(
        _fox_body,
        grid=(b, s // TQ),
        in_specs=[pl.BlockSpec((None, F_WIDTH, TQ), lambda bi, qi: (bi, 0, qi)),
                  pl.BlockSpec((None, F_WIDTH, TQ),
                               lambda bi, qi: (bi, 0, jnp.minimum(qi + 1, s // TQ - 1))),
                  pl.BlockSpec((None, F_HEADS, s, LANES), lambda bi, qi: (bi, 0, 0, 0)),
                  pl.BlockSpec((None, F_HEADS, VAUG_ROWS, s), lambda bi, qi: (bi, 0, 0, 0)),
                  pl.BlockSpec((None, SUBLANES, TQ), lambda bi, qi: (bi, 0, qi)),
                  row_spec],
        out_specs=row_spec,
        out_shape=jax.ShapeDtypeStruct((b, s, F_WIDTH), BF16),
        scratch_shapes=[pltpu.VMEM((F_HEADS, SUBLANES, TQ), F32),
                        pltpu.VMEM((F_HEADS, VAUG_ROWS, TQ), F32),
                        pltpu.VMEM((F_HEADS, TK, TQ), F32),
                        pltpu.VMEM((F_HEADS, TK, TQ), F32)],
        compiler_params=pltpu.CompilerParams(
            dimension_semantics=("arbitrary", "arbitrary"), vmem_limit_bytes=VMEM_LIMIT),
        name="fox",
    )(fqt, fqt, kaug, vaug, cf_t, fg)


def _outproj_body(a_ref, r_ref, f_ref, w_ref, x_ref, pg_ref, o_ref):
    tm = x_ref.shape[0]
    w = w_ref[...].astype(BF16)

    def project(c):
        rows = slice(c * RC_OUT, (c + 1) * RC_OUT)
        y = jnp.concatenate([a_ref[rows, :], r_ref[rows, :], f_ref[rows, :]], axis=1)
        return _dot(y, w)

    nchunks = tm // RC_OUT
    o_next = project(0)
    for c in range(nchunks):
        rows = slice(c * RC_OUT, (c + 1) * RC_OUT)
        o = o_next
        if c + 1 < nchunks:
            o_next = project(c + 1)
        ms = jnp.mean(o * o, axis=-1, keepdims=True)
        o_ref[rows, :] = x_ref[rows, :] + o * lax.rsqrt(ms + EPS) * pg_ref[...]


def _outproj(a, r, f, w_all, layer, x, post_g):
    b, s, _ = x.shape
    tm = TM_OUT
    row_spec = lambda n: pl.BlockSpec((None, tm, n), lambda bi, si: (bi, si, 0))
    c2 = lambda shp: pl.BlockSpec(shp, lambda bi, si: (0, 0))
    return pl.pallas_call(
        _outproj_body,
        grid=(b, s // tm),
        in_specs=[row_spec(A_WIDTH), row_spec(R_WIDTH), row_spec(F_WIDTH),
                  pl.BlockSpec((None, D_MODEL, D_MODEL), lambda bi, si: (layer, 0, 0)),
                  row_spec(D_MODEL), c2((1, D_MODEL))],
        out_specs=row_spec(D_MODEL),
        out_shape=jax.ShapeDtypeStruct((b, s, D_MODEL), F32),
        compiler_params=pltpu.CompilerParams(
            dimension_semantics=("arbitrary", "arbitrary"), vmem_limit_bytes=VMEM_LIMIT),
        name="outproj",
    )(a, r, f, w_all, x, post_g)


@functools.lru_cache(maxsize=None)
def _tables(seq):
    half = HEAD_DIM // 2
    lane = np.arange(LANES)
    inv = ROPE_THETA ** (-(np.arange(half, dtype=np.float64) / half))
    ang = np.arange(seq, dtype=np.float64)[:, None] * inv[None, :]
    cos = np.cos(ang)
    sin = np.sin(ang)
    cos_t = cos[:, lane % half]
    sin_t = sin[:, lane % half] * np.where(lane < HEAD_DIM, -1.0, 1.0)[None, :]

    gam = 1.0 - np.exp2(-5.0 - np.arange(R_HEADS, dtype=np.float64))
    log_gam = np.log(gam)
    pos = np.arange(L_RET)
    dist = np.abs(pos[:, None] - pos[None, :])
    allowed = (pos[None, :] // CHUNK) <= (pos[:, None] // CHUNK)
    dmat = np.where(allowed[None], np.exp(log_gam[:, None, None] * dist[None]), 0.0)
    dmat = np.concatenate([dmat[0::2], dmat[1::2]], axis=2)
    head_of_lane = lane // HEAD_DIM
    head_of_qk_lane = (lane // half) % 2
    qdec = np.stack([np.exp(log_gam[2 * p + head_of_lane][None, :] * (pos + 1.0)[:, None])
                     for p in range(HEAD_PAIRS)])
    kdec = np.stack([np.exp(log_gam[2 * p + head_of_qk_lane][None, :] * (L_RET - 1.0 - pos)[:, None])
                     for p in range(HEAD_PAIRS)])
    bmask = (head_of_qk_lane[:, None] == head_of_lane[None, :]).astype(np.float32)
    sdec = np.stack([np.exp(log_gam[2 * p + head_of_qk_lane] * L_RET)[:, None] * bmask
                     for p in range(HEAD_PAIRS)])
    tri = (np.arange(TRI_ROWS)[None, :] <= np.arange(TRI_ROWS)[:, None]).astype(np.float32)
    grp = np.arange(A_WIDTH) // HEAD_DIM
    mavg = (grp[:, None] == grp[None, :]).astype(np.float32) / HEAD_DIM
    place = np.zeros((LANES, F_HEADS * LANES), np.float32)
    for h in range(F_HEADS):
        for term in range(N_F_TERMS):
            place[F_TERM_STRIDE * term + h, h * LANES + HEAD_DIM + term] = 1.0
    f = lambda a: np.asarray(a, np.float32)
    return dict(cos=f(cos_t), sin=f(sin_t), dmat=f(dmat), qdec=f(qdec), kdec=f(kdec),
                sdec=f(sdec), bmask=f(bmask), tri=f(tri), mavg=f(mavg),
                place=f(place))


def _layer(x, pre_g, post_g, w_in, b_f, a_ln_g, a_ws, a_bs, w_out_all, layer, t):
    half = HEAD_DIM // 2
    w_t = w_in.T.astype(BF16)
    qk = w_t[A_COLS:A_COLS + 2 * R_WIDTH].reshape(2 * HEAD_PAIRS, 2, 2, half, D_MODEL)
    qk = jnp.swapaxes(qk, 1, 2).reshape(2 * R_WIDTH, D_MODEL)
    w_flg = jnp.pad(w_t[D_IN_MAIN:], ((0, LANES - F_HEADS), (0, 0)))
    bf_pad = jnp.pad(b_f, (0, LANES - F_HEADS)).reshape(1, LANES)
    abias = jnp.repeat(a_bs.T, HEAD_DIM, axis=1)
    (a_out, rq, rk, rv, rg, fqt, kaug, vaug, fg, cf_t) = _inproj(
        x, pre_g.reshape(1, D_MODEL), w_t, qk, w_flg, bf_pad, a_ln_g.reshape(1, A_WIDTH), a_ws,
        abias, t["cos"], t["sin"], t["tri"], t["mavg"], t["place"])
    r_out = _retention(rq, rk, rv, rg, t["dmat"], t["qdec"], t["kdec"], t["sdec"], t["bmask"],
                       t["mavg"])
    f_out = _fox(fqt, kaug, vaug, fg, cf_t)
    return _outproj(a_out, r_out, f_out, w_out_all, layer, x, post_g.reshape(1, D_MODEL))


def kernel(x, pre_gain, post_gain, w_in, b_forget, a_norm_gain, a_spatial_w, a_spatial_b, w_out):
    _, s, d = x.shape
    assert d == D_MODEL and all(s % t == 0 for t in (TM_IN, TM_RET, TM_OUT, TQ)), x.shape
    assert w_in.shape[1:] == (D_MODEL, D_IN) and w_out.shape[1:] == (D_MODEL, D_MODEL)
    bf16_tables = ("tri", "mavg", "place")
    t = {k: jnp.asarray(v, BF16 if k in bf16_tables else F32) for k, v in _tables(x.shape[1]).items()}
    for l in range(pre_gain.shape[0]):
        x = _layer(x, pre_gain[l], post_gain[l], w_in[l], b_forget[l], a_norm_gain[l],
                   a_spatial_w[l], a_spatial_b[l], w_out, l, t)
    return x
```
